```python
import math
import jax, jax.numpy as jnp
from jax import lax
import numpy as np

D_MODEL = 2048
BATCH = 1
SEQ = 8192
DEPTH = 2
DEC_BATCH = 32
DEC_SEQ = 8
PAST_LEN = 8192
PAGE_SIZE = 128

N_A_LAYERS = DEPTH // 2
N_B_LAYERS = DEPTH - N_A_LAYERS
HEAD_DIM = 128
HG_HEADS = 12
HG_DK = 128
HG_DV = 128
HG_WIDTH = HG_HEADS * HG_DK
HG_CHUNK = 64
MB_HEADS = 12
MB_WIDTH = MB_HEADS * HEAD_DIM
MB_BLOCK = 256
MB_TOPK = 3
MB_QCHUNK = 32
MEM_LEN = 256
MEM_HEADS = 4
MEM_WIDTH = MEM_HEADS * HEAD_DIM
N_BUCKETS = 32
MAX_DISTANCE = 1024
EPS = 1e-6
A_IN_WIDTH = 4 * HG_WIDTH + 2 * MEM_WIDTH
B_IN_WIDTH = 2 * MB_WIDTH + 2 * MEM_WIDTH
A_MIX_WIDTH = HG_WIDTH + MEM_WIDTH
B_MIX_WIDTH = MB_WIDTH + MEM_WIDTH

kernel_name = 'yoco_hgrn2_moba_step'


def rmsnorm(x, g):
    xf = x.astype(jnp.float32)
    y = xf * lax.rsqrt(jnp.mean(xf * xf, axis=-1, keepdims=True) + EPS)
    return (y * g.astype(jnp.float32)).astype(x.dtype)


def t5_bucket(dist):
    exact = N_BUCKETS // 2
    d = jnp.maximum(dist, exact).astype(jnp.float32)
    large = exact + (jnp.log(d / exact) / math.log(MAX_DISTANCE / exact) * (N_BUCKETS - exact)).astype(jnp.int32)
    return jnp.where(dist < exact, dist, jnp.minimum(large, N_BUCKETS - 1))


def hgrn2_scan(q, k, logf, v, s0):
    B, T, H, _ = q.shape
    DV = v.shape[-1]
    c = HG_CHUNK if T % HG_CHUNK == 0 else T
    n = T // c

    def chunks(a):
        return a.astype(jnp.float32).reshape(B, n, c, H, a.shape[-1]).transpose(1, 0, 3, 2, 4)

    tri = jnp.tril(jnp.ones((c, c), dtype=bool))[:, :, None]

    def step(S, inp):
        qi, ki, gi, vi = inp
        b = jnp.cumsum(gi, axis=2)
        diff = jnp.where(tri, b[:, :, :, None, :] - b[:, :, None, :, :], -jnp.inf)
        a = jnp.einsum('bhtd,bhsd,bhtsd->bhts', qi, ki, jnp.exp(diff))
        o = jnp.einsum('bhts,bhsv->bhtv', a, vi) + jnp.einsum('bhtd,bhdv->bhtv', qi * jnp.exp(b), S)
        b_end = b[:, :, -1]
        S = S * jnp.exp(b_end)[..., None] + jnp.einsum('bhsd,bhsv->bhdv', ki * jnp.exp(b_end[:, :, None] - b), vi)
        return S, o

    S, o = lax.scan(step, s0.astype(jnp.float32), tuple(map(chunks, (q, k, logf, v))))
    return o.transpose(1, 0, 3, 2, 4).reshape(B, T, H, DV), S


def mem_attend(q, mk, mv):
    logits = jnp.einsum('bthd,bmhd->bhtm', q, mk).astype(jnp.float32) * HEAD_DIM ** -0.5
    p = jax.nn.softmax(logits, axis=-1)
    return jnp.einsum('bhtm,bmhd->bthd', p.astype(mv.dtype), mv)


def mem_kv(mem, g, w):
    B, M, _ = mem.shape
    k, v = jnp.split(rmsnorm(mem, g) @ w, 2, axis=-1)
    return k.reshape(B, M, MEM_HEADS, HEAD_DIM), v.reshape(B, M, MEM_HEADS, HEAD_DIM)


def shared_kv(x, g_kv, w_kv):
    B, T, _ = x.shape
    k, v = jnp.split(rmsnorm(x, g_kv) @ w_kv, 2, axis=-1)
    return k.reshape(B, T, MB_HEADS, HEAD_DIM), v.reshape(B, T, MB_HEADS, HEAD_DIM)


def to_blocks(parts):
    B = parts[0].shape[0]
    L = sum(p.shape[1] for p in parts)
    nb = -(-L // MB_BLOCK)
    pad = nb * MB_BLOCK - L
    parts = [p.astype(parts[0].dtype) for p in parts]
    if pad:
        parts = parts + [jnp.zeros((B, pad) + parts[0].shape[2:], parts[0].dtype)]
    a = jnp.concatenate(parts, axis=1) if len(parts) > 1 else parts[0]
    return a.reshape(B, nb, MB_BLOCK, MB_HEADS, HEAD_DIM)


def moba_block(q, pos, kb, vb, kmean, rel_bias):
    C = q.shape[0]
    nb = kb.shape[0]
    k_eff = min(MB_TOPK, nb)
    cur = pos // MB_BLOCK
    gate = jnp.einsum('chd,nhd->chn', q.astype(jnp.float32), kmean)
    past = jnp.arange(nb)[None, None, :] < cur[:, None, None]
    _, top = lax.top_k(jnp.where(past, gate, -jnp.inf), k_eff)
    sel = jnp.concatenate([top, jnp.broadcast_to(cur[:, None, None], (C, MB_HEADS, 1))], axis=-1)
    flag = jnp.concatenate([top < cur[:, None, None], jnp.ones((C, MB_HEADS, 1), bool)], axis=-1)
    h_idx = jnp.arange(MB_HEADS)[None, :, None]
    ks = kb[sel, :, h_idx]
    vs = vb[sel, :, h_idx]
    key_pos = sel[..., None] * MB_BLOCK + jnp.arange(MB_BLOCK)
    qpos = pos[:, None, None, None]
    valid = flag[..., None] & (key_pos <= qpos)
    bias = rel_bias[t5_bucket(jnp.maximum(qpos - key_pos, 0)), h_idx[..., None]].astype(jnp.float32)
    logits = jnp.einsum('chd,chjkd->chjk', q, ks).astype(jnp.float32) * HEAD_DIM ** -0.5 + bias
    p = jax.nn.softmax(jnp.where(valid, logits, -jnp.inf), axis=(-2, -1))
    return jnp.einsum('chjk,chjkd->chd', p.astype(vs.dtype), vs)


def moba_attend(q, pos, kb, vb, kmean, rel_bias):
    T = q.shape[1]
    c = MB_QCHUNK if T % MB_QCHUNK == 0 else T
    n = T // c
    pos_c = pos.reshape(n, c)

    def one_seq(args):
        qs, kbs, vbs, kms = args
        out = lax.map(lambda a: moba_block(a[0], a[1], kbs, vbs, kms, rel_bias),
                      (qs.reshape(n, c, MB_HEADS, HEAD_DIM), pos_c))
        return out.reshape(T, MB_HEADS, HEAD_DIM)

    return lax.map(one_seq, (q, kb, vb, kmean))


def layer_a(x, mk, mv, s0, g_n, w_in, lb, g_o, w_out):
    B, T, _ = x.shape
    z = rmsnorm(x, g_n) @ w_in
    hq, hf, hi, hg, mq, mg = jnp.split(z, [HG_WIDTH, 2 * HG_WIDTH, 3 * HG_WIDTH, 4 * HG_WIDTH,
                                            4 * HG_WIDTH + MEM_WIDTH], axis=-1)
    q = jax.nn.silu(hq.astype(jnp.float32)).reshape(B, T, HG_HEADS, HG_DK)
    f = (lb + (1.0 - lb) * jax.nn.sigmoid(hf.astype(jnp.float32))).reshape(B, T, HG_HEADS, HG_DK)
    o, s_new = hgrn2_scan(q, 1.0 - f, jnp.log(f), hi.reshape(B, T, HG_HEADS, HG_DV), s0)
    o = rmsnorm(o, g_o.reshape(HG_HEADS, HG_DV)).reshape(B, T, HG_WIDTH) * jax.nn.silu(hg.astype(jnp.float32))
    om = mem_attend(mq.reshape(B, T, MEM_HEADS, HEAD_DIM), mk, mv).reshape(B, T, MEM_WIDTH) * jax.nn.silu(mg)
    y = jnp.concatenate([o.astype(x.dtype), om.astype(x.dtype)], axis=-1) @ w_out
    return x + y, s_new


def layer_b(x, pos, kb, vb, kmean, mk, mv, g_n, w_in, w_out, rel_bias):
    B, T, _ = x.shape
    z = rmsnorm(x, g_n) @ w_in
    hq, hg, mq, mg = jnp.split(z, [MB_WIDTH, 2 * MB_WIDTH, 2 * MB_WIDTH + MEM_WIDTH], axis=-1)
    o = moba_attend(hq.reshape(B, T, MB_HEADS, HEAD_DIM), pos, kb, vb, kmean, rel_bias)
    o = o.reshape(B, T, MB_WIDTH) * jax.nn.silu(hg)
    om = mem_attend(mq.reshape(B, T, MEM_HEADS, HEAD_DIM), mk, mv).reshape(B, T, MEM_WIDTH) * jax.nn.silu(mg)
    y = jnp.concatenate([o.astype(x.dtype), om.astype(x.dtype)], axis=-1) @ w_out
    return x + y


def setup_inputs(seed: int = 0) -> dict:
    key = jax.random.key(seed)
    ks = jax.random.split(key, 24)
    n_pages = PAST_LEN // PAGE_SIZE
    n_used = DEC_BATCH * n_pages
    n_pool = n_used + -(-n_used // 4)
    f32 = jnp.float32

    def nrm(k, shape, scale):
        return jax.random.normal(k, shape, f32) * scale

    def gain(k, shape):
        return 1.0 + 0.02 * jax.random.normal(k, shape, f32)

    page_table = jax.random.permutation(ks[7], n_pool)[:n_used].reshape(DEC_BATCH, n_pages).astype(jnp.int32)
    return {
        'x_prompt': nrm(ks[0], (BATCH, SEQ, D_MODEL), 1.0),
        'x_sample': nrm(ks[1], (DEC_BATCH, DEC_SEQ, D_MODEL), 1.0),
        'cache_k': nrm(ks[2], (n_pool, PAGE_SIZE, MB_HEADS, HEAD_DIM), 1.0),
        'cache_v': nrm(ks[3], (n_pool, PAGE_SIZE, MB_HEADS, HEAD_DIM), 1.0),
        'cache_mem_k': nrm(ks[4], (DEPTH, DEC_BATCH, MEM_LEN, MEM_HEADS, HEAD_DIM), 1.0),
        'cache_mem_v': nrm(ks[5], (DEPTH, DEC_BATCH, MEM_LEN, MEM_HEADS, HEAD_DIM), 1.0),
        'state_hgrn': nrm(ks[6], (N_A_LAYERS, DEC_BATCH, HG_HEADS, HG_DK, HG_DV), 0.3),
        'page_table': page_table,
        'mem_prompt': nrm(ks[8], (BATCH, MEM_LEN, D_MODEL), 1.0),
        'g_norm': gain(ks[9], (DEPTH, D_MODEL)),
        'w_in_a': nrm(ks[10], (N_A_LAYERS, D_MODEL, A_IN_WIDTH), D_MODEL ** -0.5),
        'hg_lb': nrm(ks[11], (N_A_LAYERS + 1, HG_WIDTH), 0.1),
        'g_hg_out': gain(ks[12], (N_A_LAYERS, HG_WIDTH)),
        'w_out_a': nrm(ks[13], (N_A_LAYERS, A_MIX_WIDTH, D_MODEL), A_MIX_WIDTH ** -0.5),
        'w_in_b': nrm(ks[14], (N_B_LAYERS, D_MODEL, B_IN_WIDTH), D_MODEL ** -0.5),
        'w_out_b': nrm(ks[15], (N_B_LAYERS, B_MIX_WIDTH, D_MODEL), B_MIX_WIDTH ** -0.5),
        'g_kv': gain(ks[16], (D_MODEL,)),
        'w_kv': nrm(ks[17], (D_MODEL, 2 * MB_WIDTH), D_MODEL ** -0.5),
        'rel_bias': nrm(ks[18], (N_BUCKETS, MB_HEADS), 0.3),
        'g_mem': gain(ks[19], (DEPTH, D_MODEL)),
        'w_mem_kv': nrm(ks[20], (DEPTH, D_MODEL, 2 * MEM_WIDTH), D_MODEL ** -0.5),
        'g_final': gain(ks[21], (D_MODEL,)),
    }


def reference(x_prompt, x_sample, cache_k, cache_v, cache_mem_k, cache_mem_v, state_hgrn, page_table, mem_prompt,
              g_norm, w_in_a, hg_lb, g_hg_out, w_out_a, w_in_b, w_out_b, g_kv, w_kv, rel_bias, g_mem, w_mem_kv,
              g_final):
    n_pages = PAST_LEN // PAGE_SIZE
    bp, tp = x_prompt.shape[0], x_prompt.shape[1]
    bs, ts = x_sample.shape[0], x_sample.shape[1]
    pos_p = jnp.arange(tp, dtype=jnp.int32)
    pos_s = PAST_LEN + jnp.arange(ts, dtype=jnp.int32)
    lb_all = jnp.cumsum(jax.nn.softmax(hg_lb.astype(jnp.float32), axis=0), axis=0)
    xp, xs = x_prompt, x_sample
    hg_p, hg_s, memk_p, memv_p = [], [], [], []
    for l in range(DEPTH):
        mk_p, mv_p = mem_kv(mem_prompt, g_mem[l], w_mem_kv[l])
        memk_p.append(mk_p)
        memv_p.append(mv_p)
        mk_s, mv_s = cache_mem_k[l], cache_mem_v[l]
        if l < N_A_LAYERS:
            s0p = jnp.zeros((bp, HG_HEADS, HG_DK, HG_DV), jnp.float32)
            xp, sp = layer_a(xp, mk_p, mv_p, s0p, g_norm[l], w_in_a[l], lb_all[l], g_hg_out[l], w_out_a[l])
            xs, ss = layer_a(xs, mk_s, mv_s, state_hgrn[l], g_norm[l], w_in_a[l], lb_all[l], g_hg_out[l], w_out_a[l])
            hg_p.append(sp)
            hg_s.append(ss)
        else:
            if l == N_A_LAYERS:
                k_p, v_p = shared_kv(xp, g_kv, w_kv)
                k_s, v_s = shared_kv(xs, g_kv, w_kv)
                kb_p, vb_p = to_blocks([k_p]), to_blocks([v_p])
                past_k = cache_k[page_table].reshape(bs, n_pages * PAGE_SIZE, MB_HEADS, HEAD_DIM)
                past_v = cache_v[page_table].reshape(bs, n_pages * PAGE_SIZE, MB_HEADS, HEAD_DIM)
                kb_s, vb_s = to_blocks([past_k, k_s]), to_blocks([past_v, v_s])
                km_p = jnp.mean(kb_p.astype(jnp.float32), axis=2)
                km_s = jnp.mean(kb_s.astype(jnp.float32), axis=2)
            b = l - N_A_LAYERS
            xp = layer_b(xp, pos_p, kb_p, vb_p, km_p, mk_p, mv_p, g_norm[l], w_in_b[b], w_out_b[b], rel_bias)
            xs = layer_b(xs, pos_s, kb_s, vb_s, km_s, mk_s, mv_s, g_norm[l], w_in_b[b], w_out_b[b], rel_bias)
    y_prompt = rmsnorm(xp, g_final)
    y_sample = rmsnorm(xs, g_final)
    hgrn_prompt = jnp.stack(hg_p).astype(state_hgrn.dtype)
    hgrn_sample = jnp.stack(hg_s).astype(state_hgrn.dtype)
    mem_k_prompt = jnp.stack(memk_p)
    mem_v_prompt = jnp.stack(memv_p)
    return (y_prompt, y_sample, k_p, v_p, k_s, v_s, hgrn_prompt, hgrn_sample, mem_k_prompt, mem_v_prompt)
```

```python
import functools
import math

import jax
import jax.numpy as jnp
from jax import lax
from jax.experimental import pallas as pl
from jax.experimental.pallas import tpu as pltpu

F32 = jnp.float32
BF16 = jnp.bfloat16

HEAD_DIM = 128
HG_HEADS = 12
MB_HEADS = 12
MEM_HEADS = 4
MB_BLOCK = 256
MB_TOPK = 3
PAGE_SIZE = 128
N_BUCKETS = 32
MAX_DISTANCE = 1024
EPS = 1e-6
HG_WIDTH = HG_HEADS * HEAD_DIM
MB_WIDTH = MB_HEADS * HEAD_DIM
MEM_WIDTH = MEM_HEADS * HEAD_DIM

HG_CHUNK = 128
HG_BLOCK_CHUNKS = 4
EXP_CLAMP = 80.0
NEAR_BLOCKS = 5
NEAR_PAGES = (NEAR_BLOCKS - 1) * MB_BLOCK // PAGE_SIZE
PAGES_PER_STEP = 4
LANES = 128
VMEM_LIMIT = 48 * 1024 * 1024

NT_DIMS = (((1,), (1,)), ((), ()))
TN_DIMS = (((0,), (0,)), ((), ()))


def _dot(a, b):
    return jnp.dot(a, b, preferred_element_type=F32)


def _dot_nt(a, b):
    return lax.dot_general(a, b, NT_DIMS, preferred_element_type=F32)


def _dot_tn(a, b):
    return lax.dot_general(a, b, TN_DIMS, preferred_element_type=F32)


def _sigmoid(x):
    return 1.0 / (1.0 + jnp.exp(-x))


def _silu(x):
    return x * _sigmoid(x)


def _split2(x):
    hi = x.astype(BF16)
    lo = (x - hi.astype(F32)).astype(BF16)
    return hi, lo


def _split3(x):
    hi = x.astype(BF16)
    r = x - hi.astype(F32)
    mid = r.astype(BF16)
    lo = (r - mid.astype(F32)).astype(BF16)
    return hi, mid, lo


def _row_tile(m, cap, mult):
    best = None
    for t in range(mult, min(m, cap) + 1, mult):
        if m % t == 0:
            best = t
    assert best is not None, (m, cap, mult)
    return best


def _params(*sem):
    return pltpu.CompilerParams(dimension_semantics=sem, vmem_limit_bytes=VMEM_LIMIT)


def _norm_matmul_kernel(x_ref, g_ref, w_ref, *rest):
    *o_refs, xn_ref = rest

    @pl.when(pl.program_id(1) == 0)
    def _():
        rows = x_ref.shape[0]
        step = 128 if rows % 128 == 0 else rows
        for r in range(0, rows, step):
            x = x_ref[r:r + step, :]
            ms = jnp.mean(x * x, axis=-1, keepdims=True)
            xn_ref[r:r + step, :] = (x * lax.rsqrt(ms + EPS) * g_ref[...]).astype(BF16)

    y = _dot(xn_ref[...], w_ref[...])
    for o_ref in o_refs:
        o_ref[...] = y.astype(o_ref.dtype)


def _norm_matmul(x, g, w, *, also_bf16=False, name):
    m, d = x.shape
    n = w.shape[1]
    tm = _row_tile(m, 768, 256)
    tn = _row_tile(n, 1024, 256)
    out_shape = [jax.ShapeDtypeStruct((m, n), F32)]
    out_specs = [pl.BlockSpec((tm, tn), lambda i, j: (i, j))]
    if also_bf16:
        out_shape.append(jax.ShapeDtypeStruct((m, n), BF16))
        out_specs.append(pl.BlockSpec((tm, tn), lambda i, j: (i, j)))
    res = pl.pallas_call(
        _norm_matmul_kernel,
        grid=(m // tm, n // tn),
        in_specs=[pl.BlockSpec((tm, d), lambda i, j: (i, 0)),
                  pl.BlockSpec((1, d), lambda i, j: (0, 0)),
                  pl.BlockSpec((d, tn), lambda i, j: (0, j))],
        out_specs=out_specs,
        out_shape=out_shape,
        scratch_shapes=[pltpu.VMEM((tm, d), BF16)],
        compiler_params=_params("parallel", "arbitrary"),
        name=name,
    )(x, g.reshape(1, d), w)
    return res if also_bf16 else res[0]


def _out_proj_kernel(a_ref, b_ref, wa_ref, wb_ref, x_ref, *rest, final_norm):
    if final_norm:
        g_ref, o_ref = rest
    else:
        (o_ref,) = rest
    y = _dot(a_ref[...].astype(BF16), wa_ref[...]) + _dot(b_ref[...].astype(BF16), wb_ref[...])
    y = x_ref[...] + y
    if final_norm:
        ms = jnp.mean(y * y, axis=-1, keepdims=True)
        y = y * lax.rsqrt(ms + EPS) * g_ref[...]
    o_ref[...] = y


def _out_proj(mix_main, mix_mem, w, x, g_final=None, *, name):
    m, d = x.shape
    wm, wmem = mix_main.shape[1], mix_mem.shape[1]
    assert wm % wmem == 0
    tm = _row_tile(m, 256, 8)
    in_specs = [pl.BlockSpec((tm, wm), lambda i: (i, 0)),
                pl.BlockSpec((tm, wmem), lambda i: (i, 0)),
                pl.BlockSpec((wm, d), lambda i: (0, 0)),
                pl.BlockSpec((wmem, d), lambda i: (wm // wmem, 0)),
                pl.BlockSpec((tm, d), lambda i: (i, 0))]
    args = [mix_main, mix_mem, w, w, x]
    if g_final is not None:
        in_specs.append(pl.BlockSpec((1, d), lambda i: (0, 0)))
        args.append(g_final.reshape(1, d))
    return pl.pallas_call(
        functools.partial(_out_proj_kernel, final_norm=g_final is not None),
        grid=(m // tm,),
        in_specs=in_specs,
        out_specs=pl.BlockSpec((tm, d), lambda i: (i, 0)),
        out_shape=jax.ShapeDtypeStruct((m, d), F32),
        compiler_params=_params("parallel"),
        name=name,
    )(*args)


def _mem_attn_kernel(q_ref, gate_ref, mk_ref, mv_ref, *rest):
    o_ref = rest[-1]
    scale = HEAD_DIM ** -0.5
    for h in range(MEM_HEADS):
        cols = slice(h * HEAD_DIM, (h + 1) * HEAD_DIM)
        q = q_ref[:, cols].astype(BF16)
        k = mk_ref[0, :, cols].astype(BF16)
        v = mv_ref[0, :, cols].astype(BF16)
        s = _dot_nt(q, k) * scale
        e = jnp.exp(s - jnp.max(s, axis=-1, keepdims=True))
        o = _dot(e.astype(BF16), v) / jnp.sum(e, axis=-1, keepdims=True)
        o_ref[:, cols] = o * _silu(gate_ref[:, cols])


def _mem_attn(z, col_q, mk, mv, row0, rows_per_batch, out_prev, m_total, *, name):
    nb, mem_len, _ = mk.shape
    tm = _row_tile(rows_per_batch, 512, 8)
    tiles = rows_per_batch // tm
    base = row0 // tm
    assert row0 % tm == 0
    row_map = lambda b, t: base + b * tiles + t
    in_specs = [pl.BlockSpec((tm, MEM_WIDTH), lambda b, t: (row_map(b, t), col_q)),
                pl.BlockSpec((tm, MEM_WIDTH), lambda b, t: (row_map(b, t), col_q + 1)),
                pl.BlockSpec((1, mem_len, MEM_WIDTH), lambda b, t: (b, 0, 0)),
                pl.BlockSpec((1, mem_len, MEM_WIDTH), lambda b, t: (b, 0, 0))]
    args = [z, z, mk, mv]
    aliases = {}
    if out_prev is not None:
        in_specs.append(pl.BlockSpec(memory_space=pl.ANY))
        args.append(out_prev)
        aliases = {4: 0}
    return pl.pallas_call(
        _mem_attn_kernel,
        grid=(nb, tiles),
        in_specs=in_specs,
        out_specs=pl.BlockSpec((tm, MEM_WIDTH), lambda b, t: (row_map(b, t), 0)),
        out_shape=jax.ShapeDtypeStruct((m_total, MEM_WIDTH), F32),
        input_output_aliases=aliases,
        compiler_params=_params("parallel", "parallel"),
        name=name,
    )(*args)


def _cumsum_rows(x, seg):
    row = lax.broadcasted_iota(jnp.int32, x.shape, 0) & (seg - 1)
    s = 1
    while s < seg:
        x = x + jnp.where(row >= s, pltpu.roll(x, s, 0), 0.0)
        s *= 2
    return x


def _lower_bound(lbp, layer):
    e = jnp.exp(lbp - jnp.max(lbp, axis=0, keepdims=True))
    return jnp.sum(e[:layer + 1], axis=0, keepdims=True) / jnp.sum(e, axis=0, keepdims=True)


def _hgrn_gates(hq, hf, lb):
    q = _silu(hq)
    f = lb + (1.0 - lb) * _sigmoid(hf)
    return q, 1.0 - f, jnp.log(f)


def _hgrn_finish(o, hg, go):
    ms = jnp.mean(o * o, axis=-1, keepdims=True)
    return o * lax.rsqrt(ms + EPS) * go * _silu(hg)


def _exp_c(x):
    return jnp.exp(jnp.minimum(x, EXP_CLAMP))


def _hgrn_chunk_prompt(q, k, g, v, st):
    c = HG_CHUNK
    half = c // 2
    b = _cumsum_rows(g, c)
    row = lax.broadcasted_iota(jnp.int32, (c, c), 0)
    col = lax.broadcasted_iota(jnp.int32, (c, c), 1)
    rowv = lax.broadcasted_iota(jnp.int32, (c, HEAD_DIM), 0)
    ref_d = jnp.where(rowv < half, b[half // 2 - 1:half // 2, :], b[half + half // 2 - 1:half + half // 2, :])
    ref_o = b[half - 1:half, :]
    b_end = b[c - 1:c, :]
    a_d = _dot_nt((q * _exp_c(b - ref_d)).astype(BF16), (k * _exp_c(ref_d - b)).astype(BF16))
    a_o = _dot_nt((q * _exp_c(b - ref_o)).astype(BF16), (k * _exp_c(ref_o - b)).astype(BF16))
    same_half = (row < half) == (col < half)
    a = jnp.where(same_half, jnp.where(row >= col, a_d, 0.0), jnp.where(row >= half, a_o, 0.0))
    vb = v.astype(BF16)
    o = _dot(a.astype(BF16), vb) + _dot_nt((q * jnp.exp(b)).astype(BF16), st.astype(BF16))
    st_new = st * jnp.exp(b_end) + _dot_tn(vb, (k * jnp.exp(b_end - b)).astype(BF16))
    return o, st_new


def _hgrn_prompt_kernel(hq_ref, hf_ref, hi_ref, hg_ref, lbp_ref, go_ref, mix_ref, sout_ref, st_ref, *, layer):
    n = pl.program_id(1)

    @pl.when(n == 0)
    def _():
        st_ref[...] = jnp.zeros_like(st_ref)

    lb = _lower_bound(lbp_ref[...], layer)
    go = go_ref[...]
    for c in range(HG_BLOCK_CHUNKS):
        rows = slice(c * HG_CHUNK, (c + 1) * HG_CHUNK)
        q, k, g = _hgrn_gates(hq_ref[rows, :], hf_ref[rows, :], lb)
        o, st_new = _hgrn_chunk_prompt(q, k, g, hi_ref[rows, :], st_ref[...])
        st_ref[...] = st_new
        mix_ref[rows, :] = _hgrn_finish(o, hg_ref[rows, :], go)

    @pl.when(n == pl.num_programs(1) - 1)
    def _():
        sout_ref[0] = st_ref[...].T


def _hgrn_prompt(z, hg_lb, g_o, tp, m_total, layer, *, name):
    rb = HG_CHUNK * HG_BLOCK_CHUNKS
    assert tp % rb == 0
    h_ = HG_HEADS
    n_lb = hg_lb.shape[0]
    zspec = lambda off: pl.BlockSpec((rb, HEAD_DIM), lambda h, n: (n, off + h))
    return pl.pallas_call(
        functools.partial(_hgrn_prompt_kernel, layer=layer),
        grid=(h_, tp // rb),
        in_specs=[zspec(0), zspec(h_), zspec(2 * h_), zspec(3 * h_),
                  pl.BlockSpec((n_lb, HEAD_DIM), lambda h, n: (0, h)),
                  pl.BlockSpec((1, HEAD_DIM), lambda h, n: (0, h))],
        out_specs=[pl.BlockSpec((rb, HEAD_DIM), lambda h, n: (n, h)),
                   pl.BlockSpec((1, HEAD_DIM, HEAD_DIM), lambda h, n: (h, 0, 0))],
        out_shape=[jax.ShapeDtypeStruct((m_total, HG_WIDTH), F32),
                   jax.ShapeDtypeStruct((h_, HEAD_DIM, HEAD_DIM), F32)],
        scratch_shapes=[pltpu.VMEM((HEAD_DIM, HEAD_DIM), F32)],
        compiler_params=_params("parallel", "arbitrary"),
        name=name,
    )(z, z, z, z, hg_lb, g_o.reshape(1, HG_WIDTH))


def _pad_rows(x, rows):
    return jnp.concatenate([x, jnp.zeros((rows - x.shape[0], x.shape[1]), x.dtype)], axis=0)


def _hgrn_sample_kernel(hq_ref, hf_ref, hi_ref, hg_ref, lbp_ref, go_ref, s0_ref, prev_ref, mix_ref, sout_ref,
                        *, layer, ts, nb):
    del prev_ref
    lb = _lower_bound(lbp_ref[...], layer)
    go = go_ref[...]
    row = lax.broadcasted_iota(jnp.int32, (ts, LANES), 0)
    col = lax.broadcasted_iota(jnp.int32, (ts, LANES), 1)

    def body(bi, carry):
        rows = pl.ds(pl.multiple_of(bi * ts, ts), ts)
        q, k, g = _hgrn_gates(hq_ref[rows, :], hf_ref[rows, :], lb)
        v = hi_ref[rows, :]
        b = _cumsum_rows(g, ts)
        b_mid = b[ts // 2 - 1:ts // 2, :]
        b_end = b[ts - 1:ts, :]
        st = s0_ref[bi, 0].T
        kt = _pad_rows(k * _exp_c(b_mid - b), LANES).astype(BF16)
        vb = _pad_rows(v, LANES).astype(BF16)
        kh = _pad_rows(k * jnp.exp(b_end - b), LANES).astype(BF16)
        a = _dot_nt((q * _exp_c(b - b_mid)).astype(BF16), kt)
        a = jnp.where(row >= col, a, 0.0)
        o = _dot(a.astype(BF16), vb) + _dot_nt((q * jnp.exp(b)).astype(BF16), st.astype(BF16))
        st_new = st * jnp.exp(b_end) + _dot_tn(vb, kh)
        sout_ref[bi, 0] = st_new.T
        mix_ref[rows, :] = _hgrn_finish(o, hg_ref[rows, :], go)
        return carry

    lax.fori_loop(0, nb, body, 0)


def _hgrn_sample(z, hg_lb, g_o, s0, mix_prev, tp, ts, layer, *, name):
    nb = s0.shape[0]
    rows = nb * ts
    assert tp % rows == 0 and ts & (ts - 1) == 0 and ts <= 8
    h_ = HG_HEADS
    n_lb = hg_lb.shape[0]
    rblk = tp // rows
    zspec = lambda off: pl.BlockSpec((rows, HEAD_DIM), lambda h: (rblk, off + h))
    return pl.pallas_call(
        functools.partial(_hgrn_sample_kernel, layer=layer, ts=ts, nb=nb),
        grid=(h_,),
        in_specs=[zspec(0), zspec(h_), zspec(2 * h_), zspec(3 * h_),
                  pl.BlockSpec((n_lb, HEAD_DIM), lambda h: (0, h)),
                  pl.BlockSpec((1, HEAD_DIM), lambda h: (0, h)),
                  pl.BlockSpec((nb, 1, HEAD_DIM, HEAD_DIM), lambda h: (0, h, 0, 0)),
                  pl.BlockSpec(memory_space=pl.ANY)],
        out_specs=[pl.BlockSpec((rows, HEAD_DIM), lambda h: (rblk, h)),
                   pl.BlockSpec((nb, 1, HEAD_DIM, HEAD_DIM), lambda h: (0, h, 0, 0))],
        out_shape=[jax.ShapeDtypeStruct(mix_prev.shape, F32),
                   jax.ShapeDtypeStruct(s0.shape, F32)],
        input_output_aliases={7: 0},
        compiler_params=_params("parallel"),
        name=name,
    )(z, z, z, z, hg_lb, g_o.reshape(1, HG_WIDTH), s0, mix_prev)


def _t5_bucket(dist):
    exact = N_BUCKETS // 2
    d = jnp.maximum(dist, exact).astype(F32)
    large = exact + (jnp.log(d / exact) / math.log(MAX_DISTANCE / exact) * (N_BUCKETS - exact)).astype(jnp.int32)
    return jnp.where(dist < exact, dist, jnp.minimum(large, N_BUCKETS - 1))


def _one_hot_buckets(dist):
    bucket = _t5_bucket(jnp.maximum(dist, 0).astype(jnp.int32))
    return (bucket[None, :] == jnp.arange(LANES, dtype=jnp.int32)[:, None]).astype(BF16)


def _bias_table_kernel(rb_ref, oh_ref, o_ref):
    hi, mid, lo = _split3(rb_ref[...])
    oh = oh_ref[...]
    o_ref[...] = _dot(hi, oh) + _dot(mid, oh) + _dot(lo, oh)


def _bias_table(rel_bias, dist):
    n = dist.shape[0]
    rb = jnp.zeros((16, LANES), F32).at[:MB_HEADS, :N_BUCKETS].set(rel_bias.T)
    return pl.pallas_call(
        _bias_table_kernel,
        out_shape=jax.ShapeDtypeStruct((16, n), F32),
        name="bias_table",
    )(rb, _one_hot_buckets(dist))


def _block_mean_kernel(k_ref, o_ref):
    o_ref[0] = jnp.sum(k_ref[...], axis=0, keepdims=True) * (1.0 / MB_BLOCK)


def _block_mean(kv, n_blocks):
    out = pl.pallas_call(
        _block_mean_kernel,
        grid=(n_blocks,),
        in_specs=[pl.BlockSpec((MB_BLOCK, MB_WIDTH), lambda n: (n, 0))],
        out_specs=pl.BlockSpec((1, 1, MB_WIDTH), lambda n: (n, 0, 0)),
        out_shape=jax.ShapeDtypeStruct((n_blocks, 1, MB_WIDTH), F32),
        compiler_params=_params("parallel"),
        name="block_mean",
    )(kv)
    return out.reshape(n_blocks, MB_WIDTH)


def _select_topk(gate, axis, n_blocks):
    idx = lax.broadcasted_iota(jnp.int32, gate.shape, axis)
    sel = jnp.zeros(gate.shape, F32)
    for _ in range(min(MB_TOPK, n_blocks)):
        mx = jnp.max(gate, axis=axis, keepdims=True)
        first = jnp.min(jnp.where(gate == mx, idx, n_blocks), axis=axis, keepdims=True)
        pick = idx == first
        sel = jnp.where(pick & (mx > -jnp.inf), 1.0, sel)
        gate = jnp.where(pick, -jnp.inf, gate)
    return sel


def _moba_prompt_kernel(q_ref, gate_ref, k_ref, v_ref, km_ref, brow_ref, o_ref, bias_ref, acc_ref, m_ref, l_ref,
                        *, n_blocks):
    i = pl.program_id(1)
    blk = MB_BLOCK
    scale = HEAD_DIM ** -0.5
    near = bias_ref.shape[0]

    @pl.when(i == 0)
    def _build_bias():
        rowi = lax.broadcasted_iota(jnp.int32, (blk, 2 * blk), 0)
        for d in range(near):
            x = jnp.broadcast_to(brow_ref[0, d:d + 1, :], (blk, 2 * blk))
            for bit in range(blk.bit_length() - 1):
                x = jnp.where(((rowi >> bit) & 1) == 1, pltpu.roll(x, 1 << bit, 1), x)
            bias_ref[d] = x[:, blk:]

    q = q_ref[...]
    qb = q.astype(BF16)

    qh, ql = _split2(q)
    kmh, kml = _split2(_pad_rows(km_ref[...], LANES))
    gate = _dot_nt(qh, kmh) + _dot_nt(qh, kml) + _dot_nt(ql, kmh)
    blk_id = lax.broadcasted_iota(jnp.int32, gate.shape, 1)
    sel = _select_topk(jnp.where(blk_id < i, gate, -jnp.inf), 1, n_blocks).astype(BF16)

    rows = pl.ds(pl.multiple_of(i * blk, blk), blk)
    s = _dot_nt(qb, k_ref[rows, :]) * scale + bias_ref[0]
    tq = lax.broadcasted_iota(jnp.int32, (blk, blk), 0)
    tk = lax.broadcasted_iota(jnp.int32, (blk, blk), 1)
    s = jnp.where(tq >= tk, s, -jnp.inf)
    m0 = jnp.max(s, axis=-1, keepdims=True)
    p = jnp.exp(s - m0)
    m_ref[...] = m0
    l_ref[...] = jnp.sum(p, axis=-1, keepdims=True)
    acc_ref[...] = _dot(p.astype(BF16), v_ref[rows, :])

    def step(j, bias):
        rows_j = pl.ds(pl.multiple_of(j * blk, blk), blk)
        s = _dot_nt(qb, k_ref[rows_j, :]) * scale + bias
        pick_j = (lax.broadcasted_iota(jnp.int32, (LANES, blk), 0) == j).astype(BF16)
        s = jnp.where(_dot(sel, pick_j) > 0.5, s, -jnp.inf)
        m_old = m_ref[...]
        m_new = jnp.maximum(m_old, jnp.max(s, axis=-1, keepdims=True))
        alpha = jnp.exp(m_old - m_new)
        p = jnp.exp(s - m_new)
        l_ref[...] = alpha * l_ref[...] + jnp.sum(p, axis=-1, keepdims=True)
        acc_ref[...] = alpha * acc_ref[...] + _dot(p.astype(BF16), v_ref[rows_j, :])
        m_ref[...] = m_new

    first_near = jnp.maximum(i - (near - 1), 0)

    def near_body(j, carry):
        step(j, bias_ref[i - j])
        return carry

    def far_body(j, carry):
        step(j, brow_ref[0, near - 1:near, 0:1])
        return carry

    lax.fori_loop(first_near, i, near_body, 0)
    lax.fori_loop(0, first_near, far_body, 0)

    o_ref[...] = acc_ref[...] / l_ref[...] * _silu(gate_ref[...])


def _moba_prompt(z, kvb, kmean, brows, tp, m_total, *, name):
    nq = tp // MB_BLOCK
    h_ = MB_HEADS
    return pl.pallas_call(
        functools.partial(_moba_prompt_kernel, n_blocks=nq),
        grid=(h_, nq),
        in_specs=[pl.BlockSpec((MB_BLOCK, HEAD_DIM), lambda h, i: (i, h)),
                  pl.BlockSpec((MB_BLOCK, HEAD_DIM), lambda h, i: (i, h_ + h)),
                  pl.BlockSpec((tp, HEAD_DIM), lambda h, i: (0, h)),
                  pl.BlockSpec((tp, HEAD_DIM), lambda h, i: (0, h_ + h)),
                  pl.BlockSpec((nq, HEAD_DIM), lambda h, i: (0, h)),
                  pl.BlockSpec((1, NEAR_BLOCKS, 2 * MB_BLOCK), lambda h, i: (h, 0, 0))],
        out_specs=pl.BlockSpec((MB_BLOCK, HEAD_DIM), lambda h, i: (i, h)),
        out_shape=jax.ShapeDtypeStruct((m_total, MB_WIDTH), F32),
        scratch_shapes=[pltpu.VMEM((NEAR_BLOCKS, MB_BLOCK, MB_BLOCK), F32),
                        pltpu.VMEM((MB_BLOCK, HEAD_DIM), F32),
                        pltpu.VMEM((MB_BLOCK, 1), F32),
                        pltpu.VMEM((MB_BLOCK, 1), F32)],
        compiler_params=_params("parallel", "arbitrary"),
        name=name,
    )(z, z, kvb, kvb, kmean, brows)


def _moba_sample_kernel(pt_ref, q_ref, gate_ref, knew_ref, vnew_ref, *rest, n_pages, ts):
    del pt_ref
    pps = PAGES_PER_STEP
    kc = rest[:pps]
    vc = rest[pps:2 * pps]
    (bs_ref, cfar_ref, prev_ref, o_ref,
     wq_ref, wqh_ref, wql_ref, s_ref, km_ref, sel_ref, acc_ref, vpad_ref, lrow_ref) = rest[2 * pps:]
    del prev_ref
    ph = pl.program_id(1)
    g = pl.program_id(2)
    n_steps = n_pages // pps
    n_blocks = n_pages * PAGE_SIZE // MB_BLOCK
    near_pages = min(NEAR_PAGES, n_pages)
    first_near_step = (n_pages - near_pages) // pps
    scale = HEAD_DIM ** -0.5
    pg = PAGE_SIZE
    cur_rows = pl.ds(n_pages * pg, pg)

    @pl.when((ph == 0) & (g == 0))
    def _start_sequence():
        q = q_ref[...]
        rep = _pad_rows(jnp.concatenate([q] * MB_HEADS, axis=0), LANES)
        r_h = lax.broadcasted_iota(jnp.int32, rep.shape, 0) // ts
        c_h = lax.broadcasted_iota(jnp.int32, rep.shape, 1) // HEAD_DIM
        wq = jnp.where(r_h == c_h, rep, 0.0)
        hi, lo = _split2(wq)
        wq_ref[...] = hi
        wqh_ref[...] = hi
        wql_ref[...] = lo
        km_ref[...] = jnp.zeros_like(km_ref)

    @pl.when(ph == 0)
    def _scores():
        is_near = g >= first_near_step
        for u in range(pps):
            p = g * pps + u
            kp = kc[u][0]
            st = _dot_nt(kp.astype(BF16), wq_ref[...])
            near_idx = jnp.maximum(p - (n_pages - near_pages), 0)
            b_near = bs_ref[pl.ds(pl.multiple_of(near_idx * pg, pg), pg), :]
            bias = jnp.where(is_near, b_near, cfar_ref[...])
            s_ref[pl.ds(pl.multiple_of(p * pg, pg), pg), :] = st * scale + bias
            n = g * (pps * pg // MB_BLOCK) + (u * pg) // MB_BLOCK
            km_ref[pl.ds(n, 1), :] = km_ref[pl.ds(n, 1), :] + jnp.sum(kp, axis=0, keepdims=True) * (1.0 / MB_BLOCK)

    @pl.when((ph == 0) & (g == n_steps - 1))
    def _softmax():
        st = _dot_nt(_pad_rows(knew_ref[...], pg).astype(BF16), wq_ref[...])
        krow = lax.broadcasted_iota(jnp.int32, (pg, LANES), 0)
        qcol = lax.broadcasted_iota(jnp.int32, (pg, LANES), 1)
        valid = (krow < ts) & (krow <= (qcol & (ts - 1)))
        s_cur = jnp.where(valid, st * scale + bs_ref[pl.ds(near_pages * pg, pg), :], -jnp.inf)
        vpad_ref[...] = _pad_rows(vnew_ref[...], pg).astype(BF16)

        kmh, kml = _split2(km_ref[...])
        gate = _dot_nt(kmh, wqh_ref[...]) + _dot_nt(kmh, wql_ref[...]) + _dot_nt(kml, wqh_ref[...])
        sel_ref[...] = _select_topk(gate, 0, n_blocks)
        ppb = MB_BLOCK // pg

        def max_body(n, m):
            keep = sel_ref[pl.ds(n, 1), :] > 0.5
            for u in range(ppb):
                tile = s_ref[pl.ds(pl.multiple_of((n * ppb + u) * pg, pg), pg), :]
                m = jnp.maximum(m, jnp.where(keep, tile, -jnp.inf))
            return m

        m = lax.fori_loop(0, n_blocks, max_body, s_cur)
        mrow = jnp.max(m, axis=0, keepdims=True)

        def exp_body(n, l):
            keep = sel_ref[pl.ds(n, 1), :] > 0.5
            for u in range(ppb):
                rows = pl.ds(pl.multiple_of((n * ppb + u) * pg, pg), pg)
                e = jnp.exp(jnp.where(keep, s_ref[rows, :] - mrow, -jnp.inf))
                s_ref[rows, :] = e
                l = l + e
            return l

        e_cur = jnp.exp(s_cur - mrow)
        s_ref[cur_rows, :] = e_cur
        l = lax.fori_loop(0, n_blocks, exp_body, e_cur)
        lrow_ref[...] = jnp.sum(l, axis=0, keepdims=True)

    @pl.when((ph == 1) & (g == 0))
    def _own_block_values():
        acc_ref[...] = _dot_tn(s_ref[cur_rows, :].astype(BF16), vpad_ref[...])

    @pl.when(ph == 1)
    def _values():
        for u in range(pps):
            p = g * pps + u
            pr = s_ref[pl.ds(pl.multiple_of(p * pg, pg), pg), :].astype(BF16)
            acc_ref[...] = acc_ref[...] + _dot_tn(pr, vc[u][0].astype(BF16))

    @pl.when((ph == 1) & (g == n_steps - 1))
    def _finish():
        r = lax.broadcasted_iota(jnp.int32, (LANES, LANES), 0)
        c = lax.broadcasted_iota(jnp.int32, (LANES, LANES), 1)
        lcol = jnp.sum(jnp.where(r == c, jnp.broadcast_to(lrow_ref[...], (LANES, LANES)), 0.0), axis=1, keepdims=True)
        for h in range(MB_HEADS):
            cols = slice(h * HEAD_DIM, (h + 1) * HEAD_DIM)
            o = acc_ref[h * ts:(h + 1) * ts, cols] / lcol[h * ts:(h + 1) * ts, :]
            o_ref[:, cols] = o * _silu(gate_ref[:, cols])


def _moba_sample(z, kv, cache_k, cache_v, page_table, bs, cfar, mix_prev, tp, ts, *, name):
    nb, n_pages = page_table.shape
    pps = PAGES_PER_STEP
    assert n_pages % pps == 0 and (pps * PAGE_SIZE) % MB_BLOCK == 0
    assert ts & (ts - 1) == 0 and MB_HEADS * ts <= LANES and tp % ts == 0
    n_steps = n_pages // pps
    n_blocks = n_pages * PAGE_SIZE // MB_BLOCK
    near_pages = min(NEAR_PAGES, n_pages)
    assert (n_pages - near_pages) % pps == 0
    kc = cache_k.reshape(cache_k.shape[0], PAGE_SIZE, MB_WIDTH)
    vc = cache_v.reshape(cache_v.shape[0], PAGE_SIZE, MB_WIDTH)
    rblk = tp // ts
    row_spec = lambda col: pl.BlockSpec((ts, MB_WIDTH), lambda b, ph, g, pt: (rblk + b, col))

    def k_spec(u):
        def imap(b, ph, g, pt):
            p = jnp.where(ph == 0, g * pps + u, n_pages - pps + u)
            return (pt[b, p], 0, 0)
        return pl.BlockSpec((1, PAGE_SIZE, MB_WIDTH), imap)

    def v_spec(u):
        def imap(b, ph, g, pt):
            p = jnp.where(ph == 1, g * pps + u, u)
            return (pt[b, p], 0, 0)
        return pl.BlockSpec((1, PAGE_SIZE, MB_WIDTH), imap)

    const2 = lambda b, ph, g, pt: (0, 0)
    grid_spec = pltpu.PrefetchScalarGridSpec(
        num_scalar_prefetch=1,
        grid=(nb, 2, n_steps),
        in_specs=[row_spec(0), row_spec(1), row_spec(0), row_spec(1)]
                 + [k_spec(u) for u in range(pps)] + [v_spec(u) for u in range(pps)]
                 + [pl.BlockSpec(bs.shape, const2), pl.BlockSpec(cfar.shape, const2),
                    pl.BlockSpec(memory_space=pl.ANY)],
        out_specs=pl.BlockSpec((ts, MB_WIDTH), lambda b, ph, g, pt: (rblk + b, 0)),
        scratch_shapes=[pltpu.VMEM((LANES, MB_WIDTH), BF16),
                        pltpu.VMEM((LANES, MB_WIDTH), BF16),
                        pltpu.VMEM((LANES, MB_WIDTH), BF16),
                        pltpu.VMEM(((n_pages + 1) * PAGE_SIZE, LANES), F32),
                        pltpu.VMEM((n_blocks, MB_WIDTH), F32),
                        pltpu.VMEM((n_blocks, LANES), F32),
                        pltpu.VMEM((LANES, MB_WIDTH), F32),
                        pltpu.VMEM((PAGE_SIZE, MB_WIDTH), BF16),
                        pltpu.VMEM((1, LANES), F32)],
    )
    n_in = 1 + 4 + 2 * pps + 3
    return pl.pallas_call(
        functools.partial(_moba_sample_kernel, n_pages=n_pages, ts=ts),
        grid_spec=grid_spec,
        out_shape=jax.ShapeDtypeStruct(mix_prev.shape, F32),
        input_output_aliases={n_in - 1: 0},
        compiler_params=_params("arbitrary", "arbitrary", "arbitrary"),
        name=name,
    )(page_table, z, z, kv, kv, *([kc] * pps), *([vc] * pps), bs, cfar, mix_prev)


def kernel(x_prompt, x_sample, cache_k, cache_v, cache_mem_k, cache_mem_v, state_hgrn, page_table, mem_prompt,
           g_norm, w_in_a, hg_lb, g_hg_out, w_out_a, w_in_b, w_out_b, g_kv, w_kv, rel_bias, g_mem, w_mem_kv,
           g_final):
    bp, tp, d = x_prompt.shape
    bs_, ts, _ = x_sample.shape
    assert bp == 1 and w_in_a.shape[0] == 1 and w_in_b.shape[0] == 1
    n_pages = page_table.shape[1]
    past_len = n_pages * PAGE_SIZE
    assert past_len % MB_BLOCK == 0 and tp % MB_BLOCK == 0
    rows_s = bs_ * ts
    m = tp + rows_s

    x0 = jnp.concatenate([x_prompt.reshape(tp, d), x_sample.reshape(rows_s, d)], axis=0)
    bf = lambda w: w.astype(BF16)

    mem_k, mem_v = [], []
    for l in range(2):
        mkv = _norm_matmul(mem_prompt.reshape(-1, d), g_mem[l], bf(w_mem_kv[l]), name=f"mem_kv_{l}")
        mem_k.append(mkv[:, :MEM_WIDTH])
        mem_v.append(mkv[:, MEM_WIDTH:])
    mem_len = mem_k[0].shape[0]

    z = _norm_matmul(x0, g_norm[0], bf(w_in_a[0]), name="in_proj_a")
    mix_h, s_prompt = _hgrn_prompt(z, hg_lb, g_hg_out[0], tp, m, 0, name="hgrn_prompt")
    mix_h, s_sample = _hgrn_sample(z, hg_lb, g_hg_out[0], state_hgrn[0], mix_h, tp, ts, 0, name="hgrn_sample")
    col_mq = 4 * HG_WIDTH // MEM_WIDTH
    mix_m = _mem_attn(z, col_mq, mem_k[0][None], mem_v[0][None], 0, tp, None, m, name="mem_attn_a_prompt")
    mix_m = _mem_attn(z, col_mq, cache_mem_k[0].reshape(bs_, mem_len, MEM_WIDTH),
                      cache_mem_v[0].reshape(bs_, mem_len, MEM_WIDTH), tp, ts, mix_m, m, name="mem_attn_a_sample")
    x1 = _out_proj(mix_h, mix_m, bf(w_out_a[0]), x0, name="out_proj_a")

    kv, kvb = _norm_matmul(x1, g_kv, bf(w_kv), also_bf16=True, name="shared_kv")
    kmean = _block_mean(kv, tp // MB_BLOCK)

    blk = MB_BLOCK
    c = jnp.arange(2 * blk, dtype=jnp.int32)
    dist_p = (jnp.arange(NEAR_BLOCKS, dtype=jnp.int32)[:, None] * blk + blk - c[None, :]).reshape(-1)
    brows = _bias_table(rel_bias, dist_p)[:MB_HEADS].reshape(MB_HEADS, NEAR_BLOCKS, 2 * blk)
    near_pages = min(NEAR_PAGES, n_pages)
    n_keys = (near_pages + 1) * PAGE_SIZE
    key_x = jnp.arange(n_keys, dtype=jnp.int32)
    dist_s = (near_pages * PAGE_SIZE - key_x[None, :] + jnp.arange(ts, dtype=jnp.int32)[:, None]).reshape(-1)
    bs_tab = _bias_table(rel_bias, dist_s)[:MB_HEADS].reshape(MB_HEADS, ts, n_keys)
    bs_tab = jnp.pad(bs_tab.transpose(2, 0, 1).reshape(n_keys, MB_HEADS * ts), ((0, 0), (0, LANES - MB_HEADS * ts)))
    cfar = jnp.pad(jnp.repeat(brows[:, NEAR_BLOCKS - 1, 0], ts), (0, LANES - MB_HEADS * ts)).reshape(1, LANES)

    zb = _norm_matmul(x1, g_norm[1], bf(w_in_b[0]), name="in_proj_b")
    mix_o = _moba_prompt(zb, kvb, kmean, brows, tp, m, name="moba_prompt")
    mix_o = _moba_sample(zb, kv, cache_k, cache_v, page_table, bs_tab, cfar, mix_o, tp, ts, name="moba_sample")
    col_mq = 2 * MB_WIDTH // MEM_WIDTH
    mix_m = _mem_attn(zb, col_mq, mem_k[1][None], mem_v[1][None], 0, tp, None, m, name="mem_attn_b_prompt")
    mix_m = _mem_attn(zb, col_mq, cache_mem_k[1].reshape(bs_, mem_len, MEM_WIDTH),
                      cache_mem_v[1].reshape(bs_, mem_len, MEM_WIDTH), tp, ts, mix_m, m, name="mem_attn_b_sample")
    y = _out_proj(mix_o, mix_m, bf(w_out_b[0]), x1, g_final, name="out_proj_b")

    heads = lambda a, b_, t: a.reshape(b_, t, MB_HEADS, HEAD_DIM)
    memh = lambda parts: jnp.stack(parts).reshape(2, bp, mem_len, MEM_HEADS, HEAD_DIM)
    return (y[:tp].reshape(bp, tp, d), y[tp:].reshape(bs_, ts, d),
            heads(kv[:tp, :MB_WIDTH], bp, tp), heads(kv[:tp, MB_WIDTH:], bp, tp),
            heads(kv[tp:, :MB_WIDTH], bs_, ts), heads(kv[tp:, MB_WIDTH:], bs_, ts),
            s_prompt[None, None].astype(state_hgrn.dtype), s_sample[None].astype(state_hgrn.dtype),
            memh(mem_k), memh(mem_v))
```

```python
import functools
import math

import jax
import jax.numpy as jnp
from jax import lax
from jax.experimental import pallas as pl
from jax.experimental.pallas import tpu as pltpu

F32 = jnp.float32
BF16 = jnp.bfloat16

HEAD_DIM = 128
HG_HEADS = 12
MB_HEADS = 12
MEM_HEADS = 4
MB_BLOCK = 256
MB_TOPK = 3
PAGE_SIZE = 128
N_BUCKETS = 32
MAX_DISTANCE = 1024
EPS = 1e-6
HG_WIDTH = HG_HEADS * HEAD_DIM
MB_WIDTH = MB_HEADS * HEAD_DIM
MEM_WIDTH = MEM_HEADS * HEAD_DIM

HG_CHUNK = 128
HG_BLOCK_CHUNKS = 4
EXP_CLAMP = 80.0
NEAR_BLOCKS = 5
NEAR_PAGES = (NEAR_BLOCKS - 1) * MB_BLOCK // PAGE_SIZE
PAGES_PER_STEP = 4
LANES = 128
VMEM_LIMIT = 48 * 1024 * 1024

NT_DIMS = (((1,), (1,)), ((), ()))
TN_DIMS = (((0,), (0,)), ((), ()))


def _dot(a, b):
    return jnp.dot(a, b, preferred_element_type=F32)


def _dot_nt(a, b):
    return lax.dot_general(a, b, NT_DIMS, preferred_element_type=F32)


def _dot_tn(a, b):
    return lax.dot_general(a, b, TN_DIMS, preferred_element_type=F32)


def _sigmoid(x):
    return 1.0 / (1.0 + jnp.exp(-x))


def _silu(x):
    return x * _sigmoid(x)


def _split2(x):
    hi = x.astype(BF16)
    lo = (x - hi.astype(F32)).astype(BF16)
    return hi, lo


def _split3(x):
    hi = x.astype(BF16)
    r = x - hi.astype(F32)
    mid = r.astype(BF16)
    lo = (r - mid.astype(F32)).astype(BF16)
    return hi, mid, lo


def _row_tile(m, cap, mult):
    best = None
    for t in range(mult, min(m, cap) + 1, mult):
        if m % t == 0:
            best = t
    assert best is not None, (m, cap, mult)
    return best


def _params(*sem):
    return pltpu.CompilerParams(dimension_semantics=sem, vmem_limit_bytes=VMEM_LIMIT)


def _norm_matmul_kernel(x_ref, g_ref, w_ref, *rest):
    *o_refs, xn_ref = rest

    @pl.when(pl.program_id(1) == 0)
    def _():
        rows = x_ref.shape[0]
        step = 128 if rows % 128 == 0 else rows
        for r in range(0, rows, step):
            x = x_ref[r:r + step, :]
            ms = jnp.mean(x * x, axis=-1, keepdims=True)
            xn_ref[r:r + step, :] = (x * lax.rsqrt(ms + EPS) * g_ref[...]).astype(BF16)

    y = _dot(xn_ref[...], w_ref[...])
    for o_ref in o_refs:
        if len(o_ref.shape) == 3:
            for h in range(o_ref.shape[0]):
                o_ref[h] = y[:, h * HEAD_DIM:(h + 1) * HEAD_DIM].astype(o_ref.dtype)
        else:
            o_ref[...] = y.astype(o_ref.dtype)


def _norm_matmul(x, g, w, *, head_major=False, name):
    m, d = x.shape
    n = w.shape[1]
    tm = _row_tile(m, 768, 256)
    tn = _row_tile(n, 1024, 256)
    if head_major:
        hm = (n // HEAD_DIM, m, HEAD_DIM)
        spec = pl.BlockSpec((tn // HEAD_DIM, tm, HEAD_DIM), lambda i, j: (j, i, 0))
        out_shape = [jax.ShapeDtypeStruct(hm, F32), jax.ShapeDtypeStruct(hm, BF16)]
        out_specs = [spec, spec]
    else:
        out_shape = [jax.ShapeDtypeStruct((m, n), F32)]
        out_specs = [pl.BlockSpec((tm, tn), lambda i, j: (i, j))]
    res = pl.pallas_call(
        _norm_matmul_kernel,
        grid=(m // tm, n // tn),
        in_specs=[pl.BlockSpec((tm, d), lambda i, j: (i, 0)),
                  pl.BlockSpec((1, d), lambda i, j: (0, 0)),
                  pl.BlockSpec((d, tn), lambda i, j: (0, j))],
        out_specs=out_specs,
        out_shape=out_shape,
        scratch_shapes=[pltpu.VMEM((tm, d), BF16)],
        compiler_params=_params("parallel", "arbitrary"),
        name=name,
    )(x, g.reshape(1, d), w)
    return res if head_major else res[0]


def _out_proj_kernel(a_ref, b_ref, wa_ref, wb_ref, x_ref, *rest, final_norm):
    if final_norm:
        g_ref, o_ref = rest
    else:
        (o_ref,) = rest
    y = _dot(a_ref[...].astype(BF16), wa_ref[...]) + _dot(b_ref[...].astype(BF16), wb_ref[...])
    y = x_ref[...] + y
    if final_norm:
        ms = jnp.mean(y * y, axis=-1, keepdims=True)
        y = y * lax.rsqrt(ms + EPS) * g_ref[...]
    o_ref[...] = y


def _out_proj(mix_main, mix_mem, w, x, g_final=None, *, name):
    m, d = x.shape
    wm, wmem = mix_main.shape[1], mix_mem.shape[1]
    assert wm % wmem == 0
    tm = _row_tile(m, 256, 8)
    in_specs = [pl.BlockSpec((tm, wm), lambda i: (i, 0)),
                pl.BlockSpec((tm, wmem), lambda i: (i, 0)),
                pl.BlockSpec((wm, d), lambda i: (0, 0)),
                pl.BlockSpec((wmem, d), lambda i: (wm // wmem, 0)),
                pl.BlockSpec((tm, d), lambda i: (i, 0))]
    args = [mix_main, mix_mem, w, w, x]
    if g_final is not None:
        in_specs.append(pl.BlockSpec((1, d), lambda i: (0, 0)))
        args.append(g_final.reshape(1, d))
    return pl.pallas_call(
        functools.partial(_out_proj_kernel, final_norm=g_final is not None),
        grid=(m // tm,),
        in_specs=in_specs,
        out_specs=pl.BlockSpec((tm, d), lambda i: (i, 0)),
        out_shape=jax.ShapeDtypeStruct((m, d), F32),
        compiler_params=_params("parallel"),
        name=name,
    )(*args)


def _mem_attn_kernel(q_ref, gate_ref, mk_ref, mv_ref, *rest):
    o_ref = rest[-1]
    scale = HEAD_DIM ** -0.5
    for h in range(MEM_HEADS):
        cols = slice(h * HEAD_DIM, (h + 1) * HEAD_DIM)
        q = q_ref[:, cols].astype(BF16)
        k = mk_ref[0, :, cols].astype(BF16)
        v = mv_ref[0, :, cols].astype(BF16)
        s = _dot_nt(q, k) * scale
        e = jnp.exp(s - jnp.max(s, axis=-1, keepdims=True))
        o = _dot(e.astype(BF16), v) / jnp.sum(e, axis=-1, keepdims=True)
        o_ref[:, cols] = o * _silu(gate_ref[:, cols])


def _mem_attn(z, col_q, mk, mv, row0, rows_per_batch, out_prev, m_total, *, name):
    nb, mem_len, _ = mk.shape
    tm = _row_tile(rows_per_batch, 512, 8)
    tiles = rows_per_batch // tm
    base = row0 // tm
    assert row0 % tm == 0
    row_map = lambda b, t: base + b * tiles + t
    in_specs = [pl.BlockSpec((tm, MEM_WIDTH), lambda b, t: (row_map(b, t), col_q)),
                pl.BlockSpec((tm, MEM_WIDTH), lambda b, t: (row_map(b, t), col_q + 1)),
                pl.BlockSpec((1, mem_len, MEM_WIDTH), lambda b, t: (b, 0, 0)),
                pl.BlockSpec((1, mem_len, MEM_WIDTH), lambda b, t: (b, 0, 0))]
    args = [z, z, mk, mv]
    aliases = {}
    if out_prev is not None:
        in_specs.append(pl.BlockSpec(memory_space=pl.ANY))
        args.append(out_prev)
        aliases = {4: 0}
    return pl.pallas_call(
        _mem_attn_kernel,
        grid=(nb, tiles),
        in_specs=in_specs,
        out_specs=pl.BlockSpec((tm, MEM_WIDTH), lambda b, t: (row_map(b, t), 0)),
        out_shape=jax.ShapeDtypeStruct((m_total, MEM_WIDTH), F32),
        input_output_aliases=aliases,
        compiler_params=_params("parallel", "parallel"),
        name=name,
    )(*args)


def _cumsum_rows(x, seg):
    row = lax.broadcasted_iota(jnp.int32, x.shape, 0) & (seg - 1)
    s = 1
    while s < seg:
        x = x + jnp.where(row >= s, pltpu.roll(x, s, 0), 0.0)
        s *= 2
    return x


def _lower_bound(lbp, layer):
    e = jnp.exp(lbp - jnp.max(lbp, axis=0, keepdims=True))
    return jnp.sum(e[:layer + 1], axis=0, keepdims=True) / jnp.sum(e, axis=0, keepdims=True)


def _hgrn_gates(hq, hf, lb):
    q = _silu(hq)
    f = lb + (1.0 - lb) * _sigmoid(hf)
    return q, 1.0 - f, jnp.log(f)


def _hgrn_finish(o, hg, go):
    ms = jnp.mean(o * o, axis=-1, keepdims=True)
    return o * lax.rsqrt(ms + EPS) * go * _silu(hg)


def _exp_c(x):
    return jnp.exp(jnp.minimum(x, EXP_CLAMP))


def _hgrn_chunk_prompt(q, k, g, v, st):
    c = HG_CHUNK
    half = c // 2
    b = _cumsum_rows(g, c)
    row = lax.broadcasted_iota(jnp.int32, (c, c), 0)
    col = lax.broadcasted_iota(jnp.int32, (c, c), 1)
    rowv = lax.broadcasted_iota(jnp.int32, (c, HEAD_DIM), 0)
    ref_d = jnp.where(rowv < half, b[half // 2 - 1:half // 2, :], b[half + half // 2 - 1:half + half // 2, :])
    ref_o = b[half - 1:half, :]
    b_end = b[c - 1:c, :]
    a_d = _dot_nt((q * _exp_c(b - ref_d)).astype(BF16), (k * _exp_c(ref_d - b)).astype(BF16))
    a_o = _dot_nt((q * _exp_c(b - ref_o)).astype(BF16), (k * _exp_c(ref_o - b)).astype(BF16))
    same_half = (row < half) == (col < half)
    a = jnp.where(same_half, jnp.where(row >= col, a_d, 0.0), jnp.where(row >= half, a_o, 0.0))
    vb = v.astype(BF16)
    o = _dot(a.astype(BF16), vb) + _dot_nt((q * jnp.exp(b)).astype(BF16), st.astype(BF16))
    st_new = st * jnp.exp(b_end) + _dot_tn(vb, (k * jnp.exp(b_end - b)).astype(BF16))
    return o, st_new


def _hgrn_prompt_kernel(hq_ref, hf_ref, hi_ref, hg_ref, lbp_ref, go_ref, mix_ref, sout_ref, st_ref, *, layer):
    n = pl.program_id(1)

    @pl.when(n == 0)
    def _():
        st_ref[...] = jnp.zeros_like(st_ref)

    lb = _lower_bound(lbp_ref[...], layer)
    go = go_ref[...]
    for c in range(HG_BLOCK_CHUNKS):
        rows = slice(c * HG_CHUNK, (c + 1) * HG_CHUNK)
        q, k, g = _hgrn_gates(hq_ref[rows, :], hf_ref[rows, :], lb)
        o, st_new = _hgrn_chunk_prompt(q, k, g, hi_ref[rows, :], st_ref[...])
        st_ref[...] = st_new
        mix_ref[rows, :] = _hgrn_finish(o, hg_ref[rows, :], go)

    @pl.when(n == pl.num_programs(1) - 1)
    def _():
        sout_ref[0] = st_ref[...].T


def _hgrn_prompt(z, hg_lb, g_o, tp, m_total, layer, *, name):
    rb = HG_CHUNK * HG_BLOCK_CHUNKS
    assert tp % rb == 0
    h_ = HG_HEADS
    n_lb = hg_lb.shape[0]
    zspec = lambda off: pl.BlockSpec((rb, HEAD_DIM), lambda h, n: (n, off + h))
    return pl.pallas_call(
        functools.partial(_hgrn_prompt_kernel, layer=layer),
        grid=(h_, tp // rb),
        in_specs=[zspec(0), zspec(h_), zspec(2 * h_), zspec(3 * h_),
                  pl.BlockSpec((n_lb, HEAD_DIM), lambda h, n: (0, h)),
                  pl.BlockSpec((1, HEAD_DIM), lambda h, n: (0, h))],
        out_specs=[pl.BlockSpec((rb, HEAD_DIM), lambda h, n: (n, h)),
                   pl.BlockSpec((1, HEAD_DIM, HEAD_DIM), lambda h, n: (h, 0, 0))],
        out_shape=[jax.ShapeDtypeStruct((m_total, HG_WIDTH), F32),
                   jax.ShapeDtypeStruct((h_, HEAD_DIM, HEAD_DIM), F32)],
        scratch_shapes=[pltpu.VMEM((HEAD_DIM, HEAD_DIM), F32)],
        compiler_params=_params("parallel", "arbitrary"),
        name=name,
    )(z, z, z, z, hg_lb, g_o.reshape(1, HG_WIDTH))


def _pad_rows(x, rows):
    return jnp.concatenate([x, jnp.zeros((rows - x.shape[0], x.shape[1]), x.dtype)], axis=0)


def _hgrn_sample_kernel(hq_ref, hf_ref, hi_ref, hg_ref, lbp_ref, go_ref, s0_ref, prev_ref, mix_ref, sout_ref,
                        *, layer, ts, nb):
    del prev_ref
    lb = _lower_bound(lbp_ref[...], layer)
    go = go_ref[...]
    row = lax.broadcasted_iota(jnp.int32, (ts, LANES), 0)
    col = lax.broadcasted_iota(jnp.int32, (ts, LANES), 1)

    def body(bi, carry):
        rows = pl.ds(pl.multiple_of(bi * ts, ts), ts)
        q, k, g = _hgrn_gates(hq_ref[rows, :], hf_ref[rows, :], lb)
        v = hi_ref[rows, :]
        b = _cumsum_rows(g, ts)
        b_mid = b[ts // 2 - 1:ts // 2, :]
        b_end = b[ts - 1:ts, :]
        st = s0_ref[bi, 0].T
        kt = _pad_rows(k * _exp_c(b_mid - b), LANES).astype(BF16)
        vb = _pad_rows(v, LANES).astype(BF16)
        kh = _pad_rows(k * jnp.exp(b_end - b), LANES).astype(BF16)
        a = _dot_nt((q * _exp_c(b - b_mid)).astype(BF16), kt)
        a = jnp.where(row >= col, a, 0.0)
        o = _dot(a.astype(BF16), vb) + _dot_nt((q * jnp.exp(b)).astype(BF16), st.astype(BF16))
        st_new = st * jnp.exp(b_end) + _dot_tn(vb, kh)
        sout_ref[bi, 0] = st_new.T
        mix_ref[rows, :] = _hgrn_finish(o, hg_ref[rows, :], go)
        return carry

    lax.fori_loop(0, nb, body, 0)


def _hgrn_sample(z, hg_lb, g_o, s0, mix_prev, tp, ts, layer, *, name):
    nb = s0.shape[0]
    rows = nb * ts
    assert tp % rows == 0 and ts & (ts - 1) == 0 and ts <= 8
    h_ = HG_HEADS
    n_lb = hg_lb.shape[0]
    rblk = tp // rows
    zspec = lambda off: pl.BlockSpec((rows, HEAD_DIM), lambda h: (rblk, off + h))
    return pl.pallas_call(
        functools.partial(_hgrn_sample_kernel, layer=layer, ts=ts, nb=nb),
        grid=(h_,),
        in_specs=[zspec(0), zspec(h_), zspec(2 * h_), zspec(3 * h_),
                  pl.BlockSpec((n_lb, HEAD_DIM), lambda h: (0, h)),
                  pl.BlockSpec((1, HEAD_DIM), lambda h: (0, h)),
                  pl.BlockSpec((nb, 1, HEAD_DIM, HEAD_DIM), lambda h: (0, h, 0, 0)),
                  pl.BlockSpec(memory_space=pl.ANY)],
        out_specs=[pl.BlockSpec((rows, HEAD_DIM), lambda h: (rblk, h)),
                   pl.BlockSpec((nb, 1, HEAD_DIM, HEAD_DIM), lambda h: (0, h, 0, 0))],
        out_shape=[jax.ShapeDtypeStruct(mix_prev.shape, F32),
                   jax.ShapeDtypeStruct(s0.shape, F32)],
        input_output_aliases={7: 0},
        compiler_params=_params("parallel"),
        name=name,
    )(z, z, z, z, hg_lb, g_o.reshape(1, HG_WIDTH), s0, mix_prev)


def _t5_bucket(dist):
    exact = N_BUCKETS // 2
    d = jnp.maximum(dist, exact).astype(F32)
    large = exact + (jnp.log(d / exact) / math.log(MAX_DISTANCE / exact) * (N_BUCKETS - exact)).astype(jnp.int32)
    return jnp.where(dist < exact, dist, jnp.minimum(large, N_BUCKETS - 1))


def _one_hot_buckets(dist):
    bucket = _t5_bucket(jnp.maximum(dist, 0).astype(jnp.int32))
    return (bucket[None, :] == jnp.arange(LANES, dtype=jnp.int32)[:, None]).astype(BF16)


def _bias_table_kernel(rb_ref, oh_ref, o_ref):
    hi, mid, lo = _split3(rb_ref[...])
    oh = oh_ref[...]
    o_ref[...] = _dot(hi, oh) + _dot(mid, oh) + _dot(lo, oh)


def _bias_table(rel_bias, dist):
    n = dist.shape[0]
    rb = jnp.zeros((16, LANES), F32).at[:MB_HEADS, :N_BUCKETS].set(rel_bias.T)
    return pl.pallas_call(
        _bias_table_kernel,
        out_shape=jax.ShapeDtypeStruct((16, n), F32),
        name="bias_table",
    )(rb, _one_hot_buckets(dist))


def _block_mean_kernel(k_ref, o_ref):
    for n in range(o_ref.shape[0]):
        rows = slice(n * MB_BLOCK, (n + 1) * MB_BLOCK)
        o_ref[n:n + 1, :] = jnp.sum(k_ref[rows, :], axis=0, keepdims=True) * (1.0 / MB_BLOCK)


def _block_mean(kvh, n_blocks):
    return pl.pallas_call(
        _block_mean_kernel,
        grid=(MB_HEADS,),
        in_specs=[pl.BlockSpec((None, n_blocks * MB_BLOCK, HEAD_DIM), lambda h: (h, 0, 0))],
        out_specs=pl.BlockSpec((None, n_blocks, HEAD_DIM), lambda h: (h, 0, 0)),
        out_shape=jax.ShapeDtypeStruct((MB_HEADS, n_blocks, HEAD_DIM), F32),
        compiler_params=_params("parallel"),
        name="block_mean",
    )(kvh)


def _select_topk(gate, axis, n_blocks):
    idx = lax.broadcasted_iota(jnp.int32, gate.shape, axis)
    sel = jnp.zeros(gate.shape, F32)
    for _ in range(min(MB_TOPK, n_blocks)):
        mx = jnp.max(gate, axis=axis, keepdims=True)
        first = jnp.min(jnp.where(gate == mx, idx, n_blocks), axis=axis, keepdims=True)
        pick = idx == first
        sel = jnp.where(pick & (mx > -jnp.inf), 1.0, sel)
        gate = jnp.where(pick, -jnp.inf, gate)
    return sel


MASK_BIG = 2.0 ** 17
DUMMY_LANE = 125
FAR_LANES = (126, 127)
MOBA_GROUP_LOG2 = 2
MOBA_GROUP = 1 << MOBA_GROUP_LOG2


def _moba_prompt_kernel(q_ref, gate_ref, k_ref, v_ref, km_ref, brow_ref, o_ref, bias_ref, s_ref, mx_ref, acc_ref,
                        *, n_blocks):
    i = pl.program_id(1)
    blk = MB_BLOCK
    scale = HEAD_DIM ** -0.5
    inv_scale = HEAD_DIM ** 0.5
    near = bias_ref.shape[0]

    @pl.when(i == 0)
    def _build_bias():
        rowi = lax.broadcasted_iota(jnp.int32, (blk, 2 * blk), 0)
        for d in range(near):
            x = jnp.broadcast_to(brow_ref[0, d:d + 1, :], (blk, 2 * blk))
            for bit in range(blk.bit_length() - 1):
                x = jnp.where(((rowi >> bit) & 1) == 1, pltpu.roll(x, 1 << bit, 1), x)
            bias_ref[d] = x[:, blk:] * inv_scale

    q = q_ref[...]
    qh, ql = _split2(q)
    kmh, kml = _split2(_pad_rows(km_ref[...], LANES))
    gate = _dot_nt(qh, kmh) + _dot_nt(qh, kml) + _dot_nt(ql, kmh)
    lane = lax.broadcasted_iota(jnp.int32, gate.shape, 1)
    sel = _select_topk(jnp.where(lane < i, gate, -jnp.inf), 1, n_blocks)

    qa = jnp.concatenate([qh, jnp.where(lane >= FAR_LANES[0], 1.0, sel - 1.0).astype(BF16)], axis=1)
    c_far = brow_ref[0, near - 1:near, 0:1] * inv_scale
    c_hi = c_far.astype(BF16).astype(F32)
    lane_r = lax.broadcasted_iota(jnp.int32, (1, LANES), 1)
    far_row = jnp.where(lane_r == FAR_LANES[0], c_hi, jnp.where(lane_r == FAR_LANES[1], c_far - c_hi, 0.0))

    def scores(j, big_lane, row_vals):
        rows = pl.ds(pl.multiple_of(j * blk, blk), blk)
        right = jnp.broadcast_to(jnp.where(lane_r == big_lane, MASK_BIG, row_vals), (blk, LANES)).astype(BF16)
        return _dot_nt(qa, jnp.concatenate([k_ref[rows, :], right], axis=1))

    first_near = jnp.maximum(i - (near - 1), 0)
    mx_ref[...] = jnp.full((blk, blk), -MASK_BIG, F32)

    def far_group(p, carry):
        mx = mx_ref[...]
        for e in range(MOBA_GROUP):
            j = MOBA_GROUP * p + e
            raw = scores(j, jnp.where(j < first_near, j, DUMMY_LANE), far_row)
            s_ref[j] = raw
            mx = jnp.maximum(mx, raw)
        mx_ref[...] = mx
        return carry

    lax.fori_loop(0, (first_near + MOBA_GROUP - 1) >> MOBA_GROUP_LOG2, far_group, 0)

    tq = lax.broadcasted_iota(jnp.int32, (blk, blk), 0)
    tk = lax.broadcasted_iota(jnp.int32, (blk, blk), 1)
    mx = mx_ref[...]
    for dlt in range(near):
        j = i - dlt
        jc = jnp.maximum(j, 0)
        big_lane = -1 if dlt == 0 else jnp.where(j >= 0, jc, DUMMY_LANE)
        raw = scores(jc, big_lane, 0.0) + bias_ref[dlt]
        if dlt == 0:
            raw = jnp.where(tq >= tk, raw, -MASK_BIG)
        s_ref[jnp.where(j >= 0, jc, n_blocks + dlt)] = raw
        mx = jnp.maximum(mx, raw)
    mx_ref[...] = jnp.broadcast_to(jnp.max(mx, axis=1, keepdims=True), (blk, blk))
    for e in range(1, MOBA_GROUP):
        s_ref[i + e] = jnp.full((blk, blk), -MASK_BIG, F32)

    acc_ref[...] = jnp.zeros_like(acc_ref)
    ones = jnp.ones((blk, LANES), BF16)
    exp2_scale = scale * math.log2(math.e)

    def pv_group(p, carry):
        probs, vals = [], []
        for e in range(MOBA_GROUP):
            j = MOBA_GROUP * p + e
            probs.append(jnp.exp2((s_ref[j] - mx_ref[...]) * exp2_scale).astype(BF16))
            rows = pl.ds(pl.multiple_of(jnp.minimum(j, i) * blk, blk), blk)
            vals.append(jnp.concatenate([v_ref[rows, :], ones], axis=1))
        acc_ref[...] = acc_ref[...] + _dot(jnp.concatenate(probs, axis=1), jnp.concatenate(vals, axis=0))
        return carry

    lax.fori_loop(0, (i + MOBA_GROUP) >> MOBA_GROUP_LOG2, pv_group, 0)
    acc = acc_ref[...]
    o_ref[...] = acc[:, :HEAD_DIM] / acc[:, HEAD_DIM:] * _silu(gate_ref[...])


def _moba_prompt(z, kvb, kmean, brows, tp, m_total, *, name):
    nq = tp // MB_BLOCK
    h_ = MB_HEADS
    assert nq <= DUMMY_LANE
    return pl.pallas_call(
        functools.partial(_moba_prompt_kernel, n_blocks=nq),
        grid=(h_, nq),
        in_specs=[pl.BlockSpec((MB_BLOCK, HEAD_DIM), lambda h, i: (i, h)),
                  pl.BlockSpec((MB_BLOCK, HEAD_DIM), lambda h, i: (i, h_ + h)),
                  pl.BlockSpec((None, tp, HEAD_DIM), lambda h, i: (h, 0, 0)),
                  pl.BlockSpec((None, tp, HEAD_DIM), lambda h, i: (h_ + h, 0, 0)),
                  pl.BlockSpec((None, nq, HEAD_DIM), lambda h, i: (h, 0, 0)),
                  pl.BlockSpec((1, NEAR_BLOCKS, 2 * MB_BLOCK), lambda h, i: (h, 0, 0))],
        out_specs=pl.BlockSpec((MB_BLOCK, HEAD_DIM), lambda h, i: (i, h)),
        out_shape=jax.ShapeDtypeStruct((m_total, MB_WIDTH), F32),
        scratch_shapes=[pltpu.VMEM((NEAR_BLOCKS, MB_BLOCK, MB_BLOCK), F32),
                        pltpu.VMEM((nq + NEAR_BLOCKS, MB_BLOCK, MB_BLOCK), F32),
                        pltpu.VMEM((MB_BLOCK, MB_BLOCK), F32),
                        pltpu.VMEM((MB_BLOCK, MB_BLOCK), F32)],
        compiler_params=_params("parallel", "arbitrary"),
        name=name,
    )(z, z, kvb, kvb, kmean, brows)


def _cat_heads(ref, *lead):
    return jnp.concatenate([ref[lead + (h,)] for h in range(MB_HEADS)], axis=1)


def _moba_sample_kernel(pt_ref, q_ref, gate_ref, knew_ref, vnew_ref, *rest, n_pages, ts):
    del pt_ref
    pps = PAGES_PER_STEP
    kc = rest[:pps]
    vc = rest[pps:2 * pps]
    (bs_ref, cfar_ref, prev_ref, o_ref,
     wq_ref, wqh_ref, wql_ref, s_ref, km_ref, sel_ref, acc_ref, vpad_ref, lrow_ref) = rest[2 * pps:]
    del prev_ref
    ph = pl.program_id(1)
    g = pl.program_id(2)
    n_steps = n_pages // pps
    n_blocks = n_pages * PAGE_SIZE // MB_BLOCK
    near_pages = min(NEAR_PAGES, n_pages)
    first_near_step = (n_pages - near_pages) // pps
    scale = HEAD_DIM ** -0.5
    pg = PAGE_SIZE
    cur_rows = pl.ds(n_pages * pg, pg)

    @pl.when((ph == 0) & (g == 0))
    def _start_sequence():
        q = q_ref[...]
        rep = _pad_rows(jnp.concatenate([q] * MB_HEADS, axis=0), LANES)
        r_h = lax.broadcasted_iota(jnp.int32, rep.shape, 0) // ts
        c_h = lax.broadcasted_iota(jnp.int32, rep.shape, 1) // HEAD_DIM
        wq = jnp.where(r_h == c_h, rep, 0.0)
        hi, lo = _split2(wq)
        wq_ref[...] = hi
        wqh_ref[...] = hi
        wql_ref[...] = lo
        km_ref[...] = jnp.zeros_like(km_ref)

    @pl.when(ph == 0)
    def _scores():
        is_near = g >= first_near_step
        for u in range(pps):
            p = g * pps + u
            kp = _cat_heads(kc[u], 0)
            st = _dot_nt(kp.astype(BF16), wq_ref[...])
            near_idx = jnp.maximum(p - (n_pages - near_pages), 0)
            b_near = bs_ref[pl.ds(pl.multiple_of(near_idx * pg, pg), pg), :]
            bias = jnp.where(is_near, b_near, cfar_ref[...])
            s_ref[pl.ds(pl.multiple_of(p * pg, pg), pg), :] = st * scale + bias
            n = g * (pps * pg // MB_BLOCK) + (u * pg) // MB_BLOCK
            km_ref[pl.ds(n, 1), :] = km_ref[pl.ds(n, 1), :] + jnp.sum(kp, axis=0, keepdims=True) * (1.0 / MB_BLOCK)

    @pl.when((ph == 0) & (g == n_steps - 1))
    def _softmax():
        st = _dot_nt(_pad_rows(_cat_heads(knew_ref), pg).astype(BF16), wq_ref[...])
        krow = lax.broadcasted_iota(jnp.int32, (pg, LANES), 0)
        qcol = lax.broadcasted_iota(jnp.int32, (pg, LANES), 1)
        valid = (krow < ts) & (krow <= (qcol & (ts - 1)))
        s_cur = jnp.where(valid, st * scale + bs_ref[pl.ds(near_pages * pg, pg), :], -jnp.inf)
        vpad_ref[...] = _pad_rows(_cat_heads(vnew_ref), pg).astype(BF16)

        kmh, kml = _split2(km_ref[...])
        gate = _dot_nt(kmh, wqh_ref[...]) + _dot_nt(kmh, wql_ref[...]) + _dot_nt(kml, wqh_ref[...])
        sel_ref[...] = _select_topk(gate, 0, n_blocks)
        ppb = MB_BLOCK // pg

        def max_body(n, m):
            keep = sel_ref[pl.ds(n, 1), :] > 0.5
            for u in range(ppb):
                tile = s_ref[pl.ds(pl.multiple_of((n * ppb + u) * pg, pg), pg), :]
                m = jnp.maximum(m, jnp.where(keep, tile, -jnp.inf))
            return m

        m = lax.fori_loop(0, n_blocks, max_body, s_cur)
        mrow = jnp.max(m, axis=0, keepdims=True)

        def exp_body(n, l):
            keep = sel_ref[pl.ds(n, 1), :] > 0.5
            for u in range(ppb):
                rows = pl.ds(pl.multiple_of((n * ppb + u) * pg, pg), pg)
                e = jnp.exp(jnp.where(keep, s_ref[rows, :] - mrow, -jnp.inf))
                s_ref[rows, :] = e
                l = l + e
            return l

        e_cur = jnp.exp(s_cur - mrow)
        s_ref[cur_rows, :] = e_cur
        l = lax.fori_loop(0, n_blocks, exp_body, e_cur)
        lrow_ref[...] = jnp.sum(l, axis=0, keepdims=True)

    @pl.when((ph == 1) & (g == 0))
    def _own_block_values():
        acc_ref[...] = _dot_tn(s_ref[cur_rows, :].astype(BF16), vpad_ref[...])

    @pl.when(ph == 1)
    def _values():
        for u in range(pps):
            p = g * pps + u
            pr = s_ref[pl.ds(pl.multiple_of(p * pg, pg), pg), :].astype(BF16)
            acc_ref[...] = acc_ref[...] + _dot_tn(pr, _cat_heads(vc[u], 0).astype(BF16))

    @pl.when((ph == 1) & (g == n_steps - 1))
    def _finish():
        r = lax.broadcasted_iota(jnp.int32, (LANES, LANES), 0)
        c = lax.broadcasted_iota(jnp.int32, (LANES, LANES), 1)
        lcol = jnp.sum(jnp.where(r == c, jnp.broadcast_to(lrow_ref[...], (LANES, LANES)), 0.0), axis=1, keepdims=True)
        for h in range(MB_HEADS):
            cols = slice(h * HEAD_DIM, (h + 1) * HEAD_DIM)
            o = acc_ref[h * ts:(h + 1) * ts, cols] / lcol[h * ts:(h + 1) * ts, :]
            o_ref[:, cols] = o * _silu(gate_ref[:, cols])


def _moba_sample(z, kvh, cache_k, cache_v, page_table, bs, cfar, mix_prev, tp, ts, *, name):
    nb, n_pages = page_table.shape
    pps = PAGES_PER_STEP
    assert n_pages % pps == 0 and (pps * PAGE_SIZE) % MB_BLOCK == 0
    assert ts & (ts - 1) == 0 and MB_HEADS * ts <= LANES and tp % ts == 0
    n_steps = n_pages // pps
    n_blocks = n_pages * PAGE_SIZE // MB_BLOCK
    near_pages = min(NEAR_PAGES, n_pages)
    assert (n_pages - near_pages) % pps == 0
    kc = cache_k.transpose(0, 2, 1, 3)
    vc = cache_v.transpose(0, 2, 1, 3)
    page_block = (1, MB_HEADS, PAGE_SIZE, HEAD_DIM)
    rblk = tp // ts
    row_spec = lambda col: pl.BlockSpec((ts, MB_WIDTH), lambda b, ph, g, pt: (rblk + b, col))
    new_spec = lambda half: pl.BlockSpec((MB_HEADS, ts, HEAD_DIM), lambda b, ph, g, pt: (half, rblk + b, 0))

    def k_spec(u):
        def imap(b, ph, g, pt):
            p = jnp.where(ph == 0, g * pps + u, n_pages - pps + u)
            return (pt[b, p], 0, 0, 0)
        return pl.BlockSpec(page_block, imap)

    def v_spec(u):
        def imap(b, ph, g, pt):
            p = jnp.where(ph == 1, g * pps + u, u)
            return (pt[b, p], 0, 0, 0)
        return pl.BlockSpec(page_block, imap)

    const2 = lambda b, ph, g, pt: (0, 0)
    grid_spec = pltpu.PrefetchScalarGridSpec(
        num_scalar_prefetch=1,
        grid=(nb, 2, n_steps),
        in_specs=[row_spec(0), row_spec(1), new_spec(0), new_spec(1)]
                 + [k_spec(u) for u in range(pps)] + [v_spec(u) for u in range(pps)]
                 + [pl.BlockSpec(bs.shape, const2), pl.BlockSpec(cfar.shape, const2),
                    pl.BlockSpec(memory_space=pl.ANY)],
        out_specs=pl.BlockSpec((ts, MB_WIDTH), lambda b, ph, g, pt: (rblk + b, 0)),
        scratch_shapes=[pltpu.VMEM((LANES, MB_WIDTH), BF16),
                        pltpu.VMEM((LANES, MB_WIDTH), BF16),
                        pltpu.VMEM((LANES, MB_WIDTH), BF16),
                        pltpu.VMEM(((n_pages + 1) * PAGE_SIZE, LANES), F32),
                        pltpu.VMEM((n_blocks, MB_WIDTH), F32),
                        pltpu.VMEM((n_blocks, LANES), F32),
                        pltpu.VMEM((LANES, MB_WIDTH), F32),
                        pltpu.VMEM((PAGE_SIZE, MB_WIDTH), BF16),
                        pltpu.VMEM((1, LANES), F32)],
    )
    n_in = 1 + 4 + 2 * pps + 3
    return pl.pallas_call(
        functools.partial(_moba_sample_kernel, n_pages=n_pages, ts=ts),
        grid_spec=grid_spec,
        out_shape=jax.ShapeDtypeStruct(mix_prev.shape, F32),
        input_output_aliases={n_in - 1: 0},
        compiler_params=_params("arbitrary", "arbitrary", "arbitrary"),
        name=name,
    )(page_table, z, z, kvh, kvh, *([kc] * pps), *([vc] * pps), bs, cfar, mix_prev)


def kernel(x_prompt, x_sample, cache_k, cache_v, cache_mem_k, cache_mem_v, state_hgrn, page_table, mem_prompt,
           g_norm, w_in_a, hg_lb, g_hg_out, w_out_a, w_in_b, w_out_b, g_kv, w_kv, rel_bias, g_mem, w_mem_kv,
           g_final):
    bp, tp, d = x_prompt.shape
    bs_, ts, _ = x_sample.shape
    assert bp == 1 and w_in_a.shape[0] == 1 and w_in_b.shape[0] == 1
    n_pages = page_table.shape[1]
    past_len = n_pages * PAGE_SIZE
    assert past_len % MB_BLOCK == 0 and tp % MB_BLOCK == 0
    rows_s = bs_ * ts
    m = tp + rows_s

    x0 = jnp.concatenate([x_prompt.reshape(tp, d), x_sample.reshape(rows_s, d)], axis=0)
    bf = lambda w: w.astype(BF16)

    mem_k, mem_v = [], []
    for l in range(2):
        mkv = _norm_matmul(mem_prompt.reshape(-1, d), g_mem[l], bf(w_mem_kv[l]), name=f"mem_kv_{l}")
        mem_k.append(mkv[:, :MEM_WIDTH])
        mem_v.append(mkv[:, MEM_WIDTH:])
    mem_len = mem_k[0].shape[0]

    z = _norm_matmul(x0, g_norm[0], bf(w_in_a[0]), name="in_proj_a")
    mix_h, s_prompt = _hgrn_prompt(z, hg_lb, g_hg_out[0], tp, m, 0, name="hgrn_prompt")
    mix_h, s_sample = _hgrn_sample(z, hg_lb, g_hg_out[0], state_hgrn[0], mix_h, tp, ts, 0, name="hgrn_sample")
    col_mq = 4 * HG_WIDTH // MEM_WIDTH
    mix_m = _mem_attn(z, col_mq, mem_k[0][None], mem_v[0][None], 0, tp, None, m, name="mem_attn_a_prompt")
    mix_m = _mem_attn(z, col_mq, cache_mem_k[0].reshape(bs_, mem_len, MEM_WIDTH),
                      cache_mem_v[0].reshape(bs_, mem_len, MEM_WIDTH), tp, ts, mix_m, m, name="mem_attn_a_sample")
    x1 = _out_proj(mix_h, mix_m, bf(w_out_a[0]), x0, name="out_proj_a")

    kvh, kvb = _norm_matmul(x1, g_kv, bf(w_kv), head_major=True, name="shared_kv")
    kmean = _block_mean(kvh, tp // MB_BLOCK)

    blk = MB_BLOCK
    c = jnp.arange(2 * blk, dtype=jnp.int32)
    dist_p = (jnp.arange(NEAR_BLOCKS, dtype=jnp.int32)[:, None] * blk + blk - c[None, :]).reshape(-1)
    brows = _bias_table(rel_bias, dist_p)[:MB_HEADS].reshape(MB_HEADS, NEAR_BLOCKS, 2 * blk)
    near_pages = min(NEAR_PAGES, n_pages)
    n_keys = (near_pages + 1) * PAGE_SIZE
    key_x = jnp.arange(n_keys, dtype=jnp.int32)
    dist_s = (near_pages * PAGE_SIZE - key_x[None, :] + jnp.arange(ts, dtype=jnp.int32)[:, None]).reshape(-1)
    bs_tab = _bias_table(rel_bias, dist_s)[:MB_HEADS].reshape(MB_HEADS, ts, n_keys)
    bs_tab = jnp.pad(bs_tab.transpose(2, 0, 1).reshape(n_keys, MB_HEADS * ts), ((0, 0), (0, LANES - MB_HEADS * ts)))
    cfar = jnp.pad(jnp.repeat(brows[:, NEAR_BLOCKS - 1, 0], ts), (0, LANES - MB_HEADS * ts)).reshape(1, LANES)

    zb = _norm_matmul(x1, g_norm[1], bf(w_in_b[0]), name="in_proj_b")
    mix_o = _moba_prompt(zb, kvb, kmean, brows, tp, m, name="moba_prompt")
    mix_o = _moba_sample(zb, kvh, cache_k, cache_v, page_table, bs_tab, cfar, mix_o, tp, ts, name="moba_sample")
    col_mq = 2 * MB_WIDTH // MEM_WIDTH
    mix_m = _mem_attn(zb, col_mq, mem_k[1][None], mem_v[1][None], 0, tp, None, m, name="mem_attn_b_prompt")
    mix_m = _mem_attn(zb, col_mq, cache_mem_k[1].reshape(bs_, mem_len, MEM_WIDTH),
                      cache_mem_v[1].reshape(bs_, mem_len, MEM_WIDTH), tp, ts, mix_m, m, name="mem_attn_b_sample")
    y = _out_proj(mix_o, mix_m, bf(w_out_b[0]), x1, g_final, name="out_proj_b")

    heads = lambda a, b_, t: a.reshape(MB_HEADS, b_, t, HEAD_DIM).transpose(1, 2, 0, 3)
    memh = lambda parts: jnp.stack(parts).reshape(2, bp, mem_len, MEM_HEADS, HEAD_DIM)
    return (y[:tp].reshape(bp, tp, d), y[tp:].reshape(bs_, ts, d),
            heads(kvh[:MB_HEADS, :tp], bp, tp), heads(kvh[MB_HEADS:, :tp], bp, tp),
            heads(kvh[:MB_HEADS, tp:], bs_, ts), heads(kvh[MB_HEADS:, tp:], bs_, ts),
            s_prompt[None, None].astype(state_hgrn.dtype), s_sample[None].astype(state_hgrn.dtype),
            memh(mem_k), memh(mem_v))
```

```python
import functools
import math

import jax
import jax.numpy as jnp
from jax import lax
from jax.experimental import pallas as pl
from jax.experimental.pallas import tpu as pltpu

F32 = jnp.float32
BF16 = jnp.bfloat16

HEAD_DIM = 128
HG_HEADS = 12
MB_HEADS = 12
MEM_HEADS = 4
MB_BLOCK = 256
MB_TOPK = 3
PAGE_SIZE = 128
N_BUCKETS = 32
MAX_DISTANCE = 1024
EPS = 1e-6
HG_WIDTH = HG_HEADS * HEAD_DIM
MB_WIDTH = MB_HEADS * HEAD_DIM
MEM_WIDTH = MEM_HEADS * HEAD_DIM

SUBLANES = 8
LANES = 128
HG_CHUNK = 128
HG_BLOCK_CHUNKS = 4
NEAR_BLOCKS = 5
NEAR_PAGES = (NEAR_BLOCKS - 1) * MB_BLOCK // PAGE_SIZE
MAX_PAGES_PER_STEP = 16
VMEM_LIMIT = 48 * 1024 * 1024

NT_DIMS = (((1,), (1,)), ((), ()))
TN_DIMS = (((0,), (0,)), ((), ()))


def _dot(a, b):
    return jnp.dot(a, b, preferred_element_type=F32)


def _dot_nt(a, b):
    return lax.dot_general(a, b, NT_DIMS, preferred_element_type=F32)


def _dot_tn(a, b):
    return lax.dot_general(a, b, TN_DIMS, preferred_element_type=F32)


def _sigmoid(x):
    return 1.0 / (1.0 + jnp.exp(-x))


def _silu(x):
    return x * _sigmoid(x)


def _split2(x):
    hi = x.astype(BF16)
    lo = (x - hi.astype(F32)).astype(BF16)
    return hi, lo


def _split3(x):
    hi = x.astype(BF16)
    r = x - hi.astype(F32)
    mid = r.astype(BF16)
    lo = (r - mid.astype(F32)).astype(BF16)
    return hi, mid, lo


def _row_tile(m, cap, mult):
    best = None
    for t in range(mult, min(m, cap) + 1, mult):
        if m % t == 0:
            best = t
    assert best is not None, (m, cap, mult)
    return best


def _params(*sem):
    return pltpu.CompilerParams(dimension_semantics=sem, vmem_limit_bytes=VMEM_LIMIT)


def _pad_rows(x, rows):
    return jnp.concatenate([x, jnp.zeros((rows - x.shape[0], x.shape[1]), x.dtype)], axis=0)


def _normalize_rows(x_ref, g_ref, xn_ref):
    rows = x_ref.shape[0]
    step = 128 if rows % 128 == 0 else rows
    for r in range(0, rows, step):
        x = x_ref[r:r + step, :]
        ms = jnp.mean(x * x, axis=-1, keepdims=True)
        xn_ref[r:r + step, :] = (x * lax.rsqrt(ms + EPS) * g_ref[...]).astype(BF16)


def _norm_matmul_kernel(x_ref, g_ref, w_ref, o_ref, xn_ref):
    @pl.when(pl.program_id(1) == 0)
    def _():
        _normalize_rows(x_ref, g_ref, xn_ref)

    o_ref[...] = _dot(xn_ref[...], w_ref[...])


def _norm_matmul(x, g, w, *, name):
    m, d = x.shape
    n = w.shape[1]
    tm = _row_tile(m, 768, 256)
    tn = _row_tile(n, 1024, 256)
    return pl.pallas_call(
        _norm_matmul_kernel,
        grid=(m // tm, n // tn),
        in_specs=[pl.BlockSpec((tm, d), lambda i, j: (i, 0)),
                  pl.BlockSpec((1, d), lambda i, j: (0, 0)),
                  pl.BlockSpec((d, tn), lambda i, j: (0, j))],
        out_specs=pl.BlockSpec((tm, tn), lambda i, j: (i, j)),
        out_shape=jax.ShapeDtypeStruct((m, n), F32),
        scratch_shapes=[pltpu.VMEM((tm, d), BF16)],
        compiler_params=_params("parallel", "arbitrary"),
        name=name,
    )(x, g.reshape(1, d), w)


def _shared_kv_kernel(x_ref, g_ref, w_ref, *rest):
    *o_refs, xn_ref = rest
    j = pl.program_id(1)

    @pl.when(j == 0)
    def _():
        _normalize_rows(x_ref, g_ref, xn_ref)

    y = _dot(xn_ref[...], w_ref[...])

    def write(refs):
        for h in range(MB_HEADS):
            for ref in refs:
                ref[h] = y[:, h * HEAD_DIM:(h + 1) * HEAD_DIM].astype(ref.dtype)

    @pl.when(j == 0)
    def _():
        write(o_refs[0::2])

    @pl.when(j == 1)
    def _():
        write(o_refs[1::2])


def _shared_kv(x, g, w, row0, rows, with_bf16, *, name):
    d = x.shape[1]
    tm = _row_tile(rows, 512, 256)
    assert row0 % tm == 0 and w.shape[1] == 2 * MB_WIDTH
    base = row0 // tm
    shape = (MB_HEADS, rows, HEAD_DIM)
    spec = pl.BlockSpec((MB_HEADS, tm, HEAD_DIM), lambda i, j: (0, i, 0))
    dtypes = [F32, F32] + ([BF16, BF16] if with_bf16 else [])
    return pl.pallas_call(
        _shared_kv_kernel,
        grid=(rows // tm, 2),
        in_specs=[pl.BlockSpec((tm, d), lambda i, j: (base + i, 0)),
                  pl.BlockSpec((1, d), lambda i, j: (0, 0)),
                  pl.BlockSpec((d, MB_WIDTH), lambda i, j: (0, j))],
        out_specs=[spec] * len(dtypes),
        out_shape=[jax.ShapeDtypeStruct(shape, t) for t in dtypes],
        scratch_shapes=[pltpu.VMEM((tm, d), BF16)],
        compiler_params=_params("parallel", "arbitrary"),
        name=name,
    )(x, g.reshape(1, d), w)


def _out_proj_kernel(ap_ref, as_ref, bp_ref, bs_ref, wa_ref, wb_ref, x_ref, *rest, prompt_tiles, final_norm):
    if final_norm:
        g_ref, yp_ref, ys_ref = rest
    else:
        (yp_ref,) = rest
        ys_ref = yp_ref
    i = pl.program_id(0)

    def run(a_ref, b_ref, o_ref):
        y = _dot(a_ref[...].astype(BF16), wa_ref[...]) + _dot(b_ref[...].astype(BF16), wb_ref[...])
        y = x_ref[...] + y
        if final_norm:
            ms = jnp.mean(y * y, axis=-1, keepdims=True)
            y = y * lax.rsqrt(ms + EPS) * g_ref[...]
        o_ref[...] = y

    @pl.when(i < prompt_tiles)
    def _():
        run(ap_ref, bp_ref, yp_ref)

    @pl.when(i >= prompt_tiles)
    def _():
        run(as_ref, bs_ref, ys_ref)


def _out_proj(main_p, main_s, mem_p, mem_s, w, x, g_final=None, *, name):
    m, d = x.shape
    tp, rows_s = main_p.shape[0], main_s.shape[0]
    wm, wmem = main_p.shape[1], mem_p.shape[1]
    assert wm % wmem == 0 and tp + rows_s == m
    tm = _row_tile(math.gcd(tp, rows_s), 256, 8)
    pt = tp // tm
    p_map = lambda i: (jnp.minimum(i, pt - 1), 0)
    s_map = lambda i: (jnp.maximum(i - pt, 0), 0)
    in_specs = [pl.BlockSpec((tm, wm), p_map), pl.BlockSpec((tm, wm), s_map),
                pl.BlockSpec((tm, wmem), p_map), pl.BlockSpec((tm, wmem), s_map),
                pl.BlockSpec((wm, d), lambda i: (0, 0)),
                pl.BlockSpec((wmem, d), lambda i: (wm // wmem, 0)),
                pl.BlockSpec((tm, d), lambda i: (i, 0))]
    args = [main_p, main_s, mem_p, mem_s, w, w, x]
    if g_final is None:
        out_specs = pl.BlockSpec((tm, d), lambda i: (i, 0))
        out_shape = jax.ShapeDtypeStruct((m, d), F32)
    else:
        in_specs.append(pl.BlockSpec((1, d), lambda i: (0, 0)))
        args.append(g_final.reshape(1, d))
        out_specs = [pl.BlockSpec((tm, d), p_map), pl.BlockSpec((tm, d), s_map)]
        out_shape = [jax.ShapeDtypeStruct((tp, d), F32), jax.ShapeDtypeStruct((rows_s, d), F32)]
    return pl.pallas_call(
        functools.partial(_out_proj_kernel, prompt_tiles=pt, final_norm=g_final is not None),
        grid=(m // tm,),
        in_specs=in_specs,
        out_specs=out_specs,
        out_shape=out_shape,
        compiler_params=_params("arbitrary"),
        name=name,
    )(*args)


def _mem_attn_kernel(q_ref, gate_ref, mk_ref, mv_ref, o_ref):
    scale = HEAD_DIM ** -0.5
    for h in range(MEM_HEADS):
        cols = slice(h * HEAD_DIM, (h + 1) * HEAD_DIM)
        q = q_ref[:, cols].astype(BF16)
        k = mk_ref[0, :, cols].astype(BF16)
        v = mv_ref[0, :, cols].astype(BF16)
        s = _dot_nt(q, k) * scale
        e = jnp.exp(s - jnp.max(s, axis=-1, keepdims=True))
        o = _dot(e.astype(BF16), v) / jnp.sum(e, axis=-1, keepdims=True)
        o_ref[:, cols] = (o * _silu(gate_ref[:, cols])).astype(o_ref.dtype)


def _mem_attn(z, col_q, mk, mv, row0, rows_per_batch, out_dtype, *, name):
    nb, mem_len, _ = mk.shape
    tm = _row_tile(rows_per_batch, 512, 8)
    tiles = rows_per_batch // tm
    assert row0 % tm == 0
    base = row0 // tm
    return pl.pallas_call(
        _mem_attn_kernel,
        grid=(nb, tiles),
        in_specs=[pl.BlockSpec((tm, MEM_WIDTH), lambda b, t: (base + b * tiles + t, col_q)),
                  pl.BlockSpec((tm, MEM_WIDTH), lambda b, t: (base + b * tiles + t, col_q + 1)),
                  pl.BlockSpec((1, mem_len, MEM_WIDTH), lambda b, t: (b, 0, 0)),
                  pl.BlockSpec((1, mem_len, MEM_WIDTH), lambda b, t: (b, 0, 0))],
        out_specs=pl.BlockSpec((tm, MEM_WIDTH), lambda b, t: (b * tiles + t, 0)),
        out_shape=jax.ShapeDtypeStruct((nb * rows_per_batch, MEM_WIDTH), out_dtype),
        compiler_params=_params("parallel", "parallel"),
        name=name,
    )(z, z, mk, mv)


def _cumsum_rows(x):
    n = x.shape[0]
    row = lax.broadcasted_iota(jnp.int32, x.shape, 0)
    s = 1
    while s < n:
        x = x + jnp.where(row >= s, pltpu.roll(x, s, 0), 0.0)
        s *= 2
    return x


def _lower_bound(lbp, layer):
    e = jnp.exp(lbp - jnp.max(lbp, axis=0, keepdims=True))
    return jnp.sum(e[:layer + 1], axis=0, keepdims=True) / jnp.sum(e, axis=0, keepdims=True)


def _hgrn_gates(hq, hf, lb):
    q = _silu(hq)
    f = lb + (1.0 - lb) * _sigmoid(hf)
    return q, 1.0 - f, jnp.log(f)


def _hgrn_finish(o, hg, go):
    ms = jnp.mean(o * o, axis=-1, keepdims=True)
    return o * lax.rsqrt(ms + EPS) * go * _silu(hg)


def _diag_tiles(q, k, b, width):
    sub = SUBLANES
    lane = lax.broadcasted_iota(jnp.int32, (sub, width), 1)
    trow = lax.broadcasted_iota(jnp.int32, (sub, width), 0)
    tiles = []
    for r0 in range(0, q.shape[0], sub):
        q8, k8, b8 = q[r0:r0 + sub], k[r0:r0 + sub], b[r0:r0 + sub]
        tile = jnp.zeros((sub, width), F32)
        for s in range(sub):
            e = jnp.exp(jnp.minimum(b8 - b8[s:s + 1], 0.0))
            a_col = jnp.sum(q8 * e * k8[s:s + 1], axis=1, keepdims=True)
            tile = jnp.where((lane == r0 + s) & (trow >= s), a_col, tile)
        tiles.append(tile)
    return jnp.concatenate(tiles, axis=0)


def _level_ref(b, level):
    parts = []
    for start in range(0, b.shape[0], 2 * level):
        r = start + level - 1
        parts.append(jnp.broadcast_to(b[r:r + 1, :], (2 * level, b.shape[1])))
    return jnp.concatenate(parts, axis=0)


def _level_masks(c):
    row = lax.broadcasted_iota(jnp.int32, (c, c), 0)
    col = lax.broadcasted_iota(jnp.int32, (c, c), 1)
    masks = []
    level = SUBLANES
    while level < c:
        sh = level.bit_length() - 1
        same = (row >> (sh + 1)) == (col >> (sh + 1))
        masks.append((level, same & (((row >> sh) & 1) == 1) & (((col >> sh) & 1) == 0)))
        level *= 2
    return masks


def _hgrn_chunk(q, k, g, v, st, masks):
    c = HG_CHUNK
    b = _cumsum_rows(g)
    a = _diag_tiles(q, k, b, c)
    for level, mask in masks:
        e = jnp.exp(-jnp.abs(b - _level_ref(b, level)))
        a = jnp.where(mask, _dot_nt((q * e).astype(BF16), (k * e).astype(BF16)), a)
    b_end = b[c - 1:c, :]
    vb = v.astype(BF16)
    o = _dot(a.astype(BF16), vb) + _dot_nt((q * jnp.exp(b)).astype(BF16), st.astype(BF16))
    st_new = st * jnp.exp(b_end) + _dot_tn(vb, (k * jnp.exp(b_end - b)).astype(BF16))
    return o, st_new


def _hgrn_prompt_kernel(hq_ref, hf_ref, hi_ref, hg_ref, lbp_ref, go_ref, mix_ref, sout_ref, st_ref, *, layer):
    n = pl.program_id(1)

    @pl.when(n == 0)
    def _():
        st_ref[...] = jnp.zeros_like(st_ref)

    lb = _lower_bound(lbp_ref[...], layer)
    go = go_ref[...]
    masks = _level_masks(HG_CHUNK)
    for c in range(HG_BLOCK_CHUNKS):
        rows = slice(c * HG_CHUNK, (c + 1) * HG_CHUNK)
        q, k, g = _hgrn_gates(hq_ref[rows, :], hf_ref[rows, :], lb)
        o, st_new = _hgrn_chunk(q, k, g, hi_ref[rows, :], st_ref[...], masks)
        st_ref[...] = st_new
        mix_ref[rows, :] = _hgrn_finish(o, hg_ref[rows, :], go).astype(mix_ref.dtype)

    @pl.when(n == pl.num_programs(1) - 1)
    def _():
        sout_ref[0] = st_ref[...].T


def _hgrn_prompt(z, hg_lb, g_o, tp, layer, *, name):
    rb = HG_CHUNK * HG_BLOCK_CHUNKS
    assert tp % rb == 0
    h_ = HG_HEADS
    n_lb = hg_lb.shape[0]
    zspec = lambda off: pl.BlockSpec((rb, HEAD_DIM), lambda h, n: (n, off + h))
    return pl.pallas_call(
        functools.partial(_hgrn_prompt_kernel, layer=layer),
        grid=(h_, tp // rb),
        in_specs=[zspec(0), zspec(h_), zspec(2 * h_), zspec(3 * h_),
                  pl.BlockSpec((n_lb, HEAD_DIM), lambda h, n: (0, h)),
                  pl.BlockSpec((1, HEAD_DIM), lambda h, n: (0, h))],
        out_specs=[pl.BlockSpec((rb, HEAD_DIM), lambda h, n: (n, h)),
                   pl.BlockSpec((1, HEAD_DIM, HEAD_DIM), lambda h, n: (h, 0, 0))],
        out_shape=[jax.ShapeDtypeStruct((tp, HG_WIDTH), BF16),
                   jax.ShapeDtypeStruct((h_, HEAD_DIM, HEAD_DIM), F32)],
        scratch_shapes=[pltpu.VMEM((HEAD_DIM, HEAD_DIM), F32)],
        compiler_params=_params("parallel", "arbitrary"),
        name=name,
    )(z, z, z, z, hg_lb, g_o.reshape(1, HG_WIDTH))


def _hgrn_sample_kernel(hq_ref, hf_ref, hi_ref, hg_ref, lbp_ref, go_ref, s0_ref, mix_ref, sout_ref, *, layer, ts, nb):
    lb = _lower_bound(lbp_ref[...], layer)
    go = go_ref[...]

    def body(bi, carry):
        rows = pl.ds(pl.multiple_of(bi * ts, ts), ts)
        q, k, g = _hgrn_gates(hq_ref[rows, :], hf_ref[rows, :], lb)
        b = _cumsum_rows(g)
        b_end = b[ts - 1:ts, :]
        st = s0_ref[bi, 0].T
        a = _diag_tiles(q, k, b, LANES)
        vb = _pad_rows(hi_ref[rows, :], LANES).astype(BF16)
        kh = _pad_rows(k * jnp.exp(b_end - b), LANES).astype(BF16)
        o = _dot(a.astype(BF16), vb) + _dot_nt((q * jnp.exp(b)).astype(BF16), st.astype(BF16))
        st_new = st * jnp.exp(b_end) + _dot_tn(vb, kh)
        sout_ref[bi, 0] = st_new.T
        mix_ref[rows, :] = _hgrn_finish(o, hg_ref[rows, :], go)
        return carry

    lax.fori_loop(0, nb, body, 0)


def _hgrn_sample(z, hg_lb, g_o, s0, tp, ts, layer, *, name):
    nb = s0.shape[0]
    rows = nb * ts
    assert tp % rows == 0 and ts == SUBLANES
    h_ = HG_HEADS
    n_lb = hg_lb.shape[0]
    rblk = tp // rows
    zspec = lambda off: pl.BlockSpec((rows, HEAD_DIM), lambda h: (rblk, off + h))
    return pl.pallas_call(
        functools.partial(_hgrn_sample_kernel, layer=layer, ts=ts, nb=nb),
        grid=(h_,),
        in_specs=[zspec(0), zspec(h_), zspec(2 * h_), zspec(3 * h_),
                  pl.BlockSpec((n_lb, HEAD_DIM), lambda h: (0, h)),
                  pl.BlockSpec((1, HEAD_DIM), lambda h: (0, h)),
                  pl.BlockSpec((nb, 1, HEAD_DIM, HEAD_DIM), lambda h: (0, h, 0, 0))],
        out_specs=[pl.BlockSpec((rows, HEAD_DIM), lambda h: (0, h)),
                   pl.BlockSpec((nb, 1, HEAD_DIM, HEAD_DIM), lambda h: (0, h, 0, 0))],
        out_shape=[jax.ShapeDtypeStruct((rows, HG_WIDTH), F32),
                   jax.ShapeDtypeStruct(s0.shape, F32)],
        compiler_params=_params("parallel"),
        name=name,
    )(z, z, z, z, hg_lb, g_o.reshape(1, HG_WIDTH), s0)


def _t5_bucket(dist):
    exact = N_BUCKETS // 2
    d = jnp.maximum(dist, exact).astype(F32)
    large = exact + (jnp.log(d / exact) / math.log(MAX_DISTANCE / exact) * (N_BUCKETS - exact)).astype(jnp.int32)
    return jnp.where(dist < exact, dist, jnp.minimum(large, N_BUCKETS - 1))


def _one_hot_buckets(dist):
    bucket = _t5_bucket(jnp.maximum(dist, 0).astype(jnp.int32))
    return (bucket[None, :] == jnp.arange(LANES, dtype=jnp.int32)[:, None]).astype(BF16)


def _bias_table_kernel(rb_ref, oh_ref, o_ref):
    hi, mid, lo = _split3(rb_ref[...])
    oh = oh_ref[...]
    o_ref[...] = _dot(hi, oh) + _dot(mid, oh) + _dot(lo, oh)


def _bias_table(rel_bias, dist):
    n = dist.shape[0]
    rb = jnp.zeros((16, LANES), F32).at[:MB_HEADS, :N_BUCKETS].set(rel_bias.T)
    return pl.pallas_call(
        _bias_table_kernel,
        out_shape=jax.ShapeDtypeStruct((16, n), F32),
        name="bias_table",
    )(rb, _one_hot_buckets(dist))


def _block_mean_kernel(k_ref, o_ref):
    for n in range(o_ref.shape[0]):
        rows = slice(n * MB_BLOCK, (n + 1) * MB_BLOCK)
        o_ref[n:n + 1, :] = jnp.sum(k_ref[rows, :], axis=0, keepdims=True) * (1.0 / MB_BLOCK)


def _block_mean(k, n_blocks):
    return pl.pallas_call(
        _block_mean_kernel,
        grid=(MB_HEADS,),
        in_specs=[pl.BlockSpec((None, n_blocks * MB_BLOCK, HEAD_DIM), lambda h: (h, 0, 0))],
        out_specs=pl.BlockSpec((None, n_blocks, HEAD_DIM), lambda h: (h, 0, 0)),
        out_shape=jax.ShapeDtypeStruct((MB_HEADS, n_blocks, HEAD_DIM), F32),
        compiler_params=_params("parallel"),
        name="block_mean",
    )(k)


def _select_topk(gate, axis, n_blocks):
    idx = lax.broadcasted_iota(jnp.int32, gate.shape, axis)
    sel = jnp.zeros(gate.shape, F32)
    for _ in range(min(MB_TOPK, n_blocks)):
        mx = jnp.max(gate, axis=axis, keepdims=True)
        first = jnp.min(jnp.where(gate == mx, idx, n_blocks), axis=axis, keepdims=True)
        pick = idx == first
        sel = jnp.where(pick & (mx > -jnp.inf), 1.0, sel)
        gate = jnp.where(pick, -jnp.inf, gate)
    return sel


MASK_BIG = 2.0 ** 17
DUMMY_LANE = 125
FAR_LANES = (126, 127)
MOBA_GROUP_LOG2 = 2
MOBA_GROUP = 1 << MOBA_GROUP_LOG2


def _moba_prompt_kernel(q_ref, gate_ref, k_ref, v_ref, km_ref, brow_ref, o_ref, bias_ref, s_ref, mx_ref, acc_ref,
                        *, n_blocks):
    i = pl.program_id(1)
    blk = MB_BLOCK
    scale = HEAD_DIM ** -0.5
    inv_scale = HEAD_DIM ** 0.5
    near = bias_ref.shape[0]

    @pl.when(i == 0)
    def _build_bias():
        rowi = lax.broadcasted_iota(jnp.int32, (blk, 2 * blk), 0)
        for d in range(near):
            x = jnp.broadcast_to(brow_ref[0, d:d + 1, :], (blk, 2 * blk))
            for bit in range(blk.bit_length() - 1):
                x = jnp.where(((rowi >> bit) & 1) == 1, pltpu.roll(x, 1 << bit, 1), x)
            bias_ref[d] = x[:, blk:] * inv_scale

    q = q_ref[...]
    qh, ql = _split2(q)
    kmh, kml = _split2(_pad_rows(km_ref[...], LANES))
    gate_t = _dot_nt(kmh, qh) + _dot_nt(kml, qh) + _dot_nt(kmh, ql)
    blk_id = lax.broadcasted_iota(jnp.int32, gate_t.shape, 0)
    sel = _select_topk(jnp.where(blk_id < i, gate_t, -jnp.inf), 0, n_blocks).T

    lane = lax.broadcasted_iota(jnp.int32, sel.shape, 1)
    qa = jnp.concatenate([qh, jnp.where(lane >= FAR_LANES[0], 1.0, sel - 1.0).astype(BF16)], axis=1)
    c_far = brow_ref[0, near - 1:near, 0:1] * inv_scale
    c_hi = c_far.astype(BF16).astype(F32)
    lane_r = lax.broadcasted_iota(jnp.int32, (1, LANES), 1)
    far_row = jnp.where(lane_r == FAR_LANES[0], c_hi, jnp.where(lane_r == FAR_LANES[1], c_far - c_hi, 0.0))

    def scores(j, big_lane, row_vals):
        rows = pl.ds(pl.multiple_of(j * blk, blk), blk)
        right = jnp.broadcast_to(jnp.where(lane_r == big_lane, MASK_BIG, row_vals), (blk, LANES)).astype(BF16)
        return _dot_nt(qa, jnp.concatenate([k_ref[rows, :], right], axis=1))

    first_near = jnp.maximum(i - (near - 1), 0)
    mx_ref[...] = jnp.full((blk, blk), -MASK_BIG, F32)

    def far_group(p, carry):
        mx = mx_ref[...]
        for e in range(MOBA_GROUP):
            j = MOBA_GROUP * p + e
            raw = scores(j, jnp.where(j < first_near, j, DUMMY_LANE), far_row)
            s_ref[j] = raw
            mx = jnp.maximum(mx, raw)
        mx_ref[...] = mx
        return carry

    lax.fori_loop(0, (first_near + MOBA_GROUP - 1) >> MOBA_GROUP_LOG2, far_group, 0)

    tq = lax.broadcasted_iota(jnp.int32, (blk, blk), 0)
    tk = lax.broadcasted_iota(jnp.int32, (blk, blk), 1)
    mx = mx_ref[...]
    for dlt in range(near):
        j = i - dlt
        jc = jnp.maximum(j, 0)
        big_lane = -1 if dlt == 0 else jnp.where(j >= 0, jc, DUMMY_LANE)
        raw = scores(jc, big_lane, 0.0) + bias_ref[dlt]
        if dlt == 0:
            raw = jnp.where(tq >= tk, raw, -MASK_BIG)
        s_ref[jnp.where(j >= 0, jc, n_blocks + dlt)] = raw
        mx = jnp.maximum(mx, raw)
    mx_ref[...] = jnp.broadcast_to(jnp.max(mx, axis=1, keepdims=True), (blk, blk))
    for e in range(1, MOBA_GROUP):
        s_ref[i + e] = jnp.full((blk, blk), -MASK_BIG, F32)

    acc_ref[...] = jnp.zeros_like(acc_ref)
    ones = jnp.ones((blk, LANES), BF16)
    exp2_scale = scale * math.log2(math.e)

    def pv_group(p, carry):
        probs, vals = [], []
        for e in range(MOBA_GROUP):
            j = MOBA_GROUP * p + e
            probs.append(jnp.exp2((s_ref[j] - mx_ref[...]) * exp2_scale).astype(BF16))
            rows = pl.ds(pl.multiple_of(jnp.minimum(j, i) * blk, blk), blk)
            vals.append(jnp.concatenate([v_ref[rows, :], ones], axis=1))
        acc_ref[...] = acc_ref[...] + _dot(jnp.concatenate(probs, axis=1), jnp.concatenate(vals, axis=0))
        return carry

    lax.fori_loop(0, (i + MOBA_GROUP) >> MOBA_GROUP_LOG2, pv_group, 0)
    acc = acc_ref[...]
    o_ref[...] = (acc[:, :HEAD_DIM] / acc[:, HEAD_DIM:] * _silu(gate_ref[...])).astype(o_ref.dtype)


def _moba_prompt(z, kb, vb, kmean, brows, tp, *, name):
    nq = tp // MB_BLOCK
    h_ = MB_HEADS
    assert nq <= DUMMY_LANE
    return pl.pallas_call(
        functools.partial(_moba_prompt_kernel, n_blocks=nq),
        grid=(h_, nq),
        in_specs=[pl.BlockSpec((MB_BLOCK, HEAD_DIM), lambda h, i: (i, h)),
                  pl.BlockSpec((MB_BLOCK, HEAD_DIM), lambda h, i: (i, h_ + h)),
                  pl.BlockSpec((None, tp, HEAD_DIM), lambda h, i: (h, 0, 0)),
                  pl.BlockSpec((None, tp, HEAD_DIM), lambda h, i: (h, 0, 0)),
                  pl.BlockSpec((None, nq, HEAD_DIM), lambda h, i: (h, 0, 0)),
                  pl.BlockSpec((1, NEAR_BLOCKS, 2 * MB_BLOCK), lambda h, i: (h, 0, 0))],
        out_specs=pl.BlockSpec((MB_BLOCK, HEAD_DIM), lambda h, i: (i, h)),
        out_shape=jax.ShapeDtypeStruct((tp, MB_WIDTH), BF16),
        scratch_shapes=[pltpu.VMEM((NEAR_BLOCKS, MB_BLOCK, MB_BLOCK), F32),
                        pltpu.VMEM((nq + NEAR_BLOCKS, MB_BLOCK, MB_BLOCK), F32),
                        pltpu.VMEM((MB_BLOCK, MB_BLOCK), F32),
                        pltpu.VMEM((MB_BLOCK, MB_BLOCK), F32)],
        compiler_params=_params("parallel", "arbitrary"),
        name=name,
    )(z, z, kb, vb, kmean, brows)


def _cat_heads(ref, *lead):
    return jnp.concatenate([ref[lead + (h,)] for h in range(MB_HEADS)], axis=1)


def _sample_scores_kernel(pt_ref, q_ref, knew_ref, *rest, n_pages, pps, ts):
    del pt_ref
    kc = rest[:pps]
    bs_ref, cfar_ref, p_ref, l_ref, wq_ref, wql_ref, s_ref, km_ref, sel_ref = rest[pps:]
    g = pl.program_id(1)
    n_steps = n_pages // pps
    n_blocks = n_pages * PAGE_SIZE // MB_BLOCK
    near_pages = min(NEAR_PAGES, n_pages)
    scale = HEAD_DIM ** -0.5
    pg = PAGE_SIZE
    ppb = MB_BLOCK // pg

    @pl.when(g == 0)
    def _start_sequence():
        rep = _pad_rows(jnp.concatenate([q_ref[...]] * MB_HEADS, axis=0), LANES)
        r_h = lax.broadcasted_iota(jnp.int32, rep.shape, 0) // ts
        c_h = lax.broadcasted_iota(jnp.int32, rep.shape, 1) // HEAD_DIM
        hi, lo = _split2(jnp.where(r_h == c_h, rep, 0.0))
        wq_ref[...] = hi
        wql_ref[...] = lo

    pages = [_cat_heads(kc[u], 0) for u in range(pps)]
    st = _dot_nt(jnp.concatenate([kp.astype(BF16) for kp in pages], axis=0), wq_ref[...])
    for u in range(pps):
        p = g * pps + u
        near_idx = jnp.maximum(p - (n_pages - near_pages), 0)
        b_near = bs_ref[pl.ds(pl.multiple_of(near_idx * pg, pg), pg), :]
        bias = jnp.where(p >= n_pages - near_pages, b_near, cfar_ref[...])
        s_ref[pl.ds(pl.multiple_of(p * pg, pg), pg), :] = st[u * pg:(u + 1) * pg] * scale + bias
    for n in range(pps // ppb):
        ksum = sum(jnp.sum(pages[n * ppb + u], axis=0, keepdims=True) for u in range(ppb))
        km_ref[pl.ds(g * (pps // ppb) + n, 1), :] = ksum * (1.0 / MB_BLOCK)

    @pl.when(g == n_steps - 1)
    def _softmax():
        st_new = _dot_nt(_pad_rows(_cat_heads(knew_ref), pg).astype(BF16), wq_ref[...])
        krow = lax.broadcasted_iota(jnp.int32, (pg, LANES), 0)
        qcol = lax.broadcasted_iota(jnp.int32, (pg, LANES), 1)
        valid = (krow < ts) & (krow <= (qcol & (ts - 1)))
        s_cur = jnp.where(valid, st_new * scale + bs_ref[pl.ds(near_pages * pg, pg), :], -jnp.inf)

        kmh, kml = _split2(km_ref[...])
        gate = _dot_nt(kmh, wq_ref[...]) + _dot_nt(kmh, wql_ref[...]) + _dot_nt(kml, wq_ref[...])
        sel_ref[...] = _select_topk(gate, 0, n_blocks)

        def max_body(n, m):
            keep = sel_ref[pl.ds(n, 1), :] > 0.5
            for u in range(ppb):
                tile = s_ref[pl.ds(pl.multiple_of((n * ppb + u) * pg, pg), pg), :]
                m = jnp.maximum(m, jnp.where(keep, tile, -jnp.inf))
            return m

        m = lax.fori_loop(0, n_blocks, max_body, s_cur)
        mrow = jnp.max(m, axis=0, keepdims=True)

        def exp_body(n, l):
            keep = sel_ref[pl.ds(n, 1), :] > 0.5
            for u in range(ppb):
                rows = pl.ds(pl.multiple_of((n * ppb + u) * pg, pg), pg)
                e = jnp.exp(jnp.where(keep, s_ref[rows, :] - mrow, -jnp.inf))
                p_ref[0, rows, :] = e.astype(BF16)
                l = l + e
            return l

        e_cur = jnp.exp(s_cur - mrow)
        p_ref[0, pl.ds(n_pages * pg, pg), :] = e_cur.astype(BF16)
        l = lax.fori_loop(0, n_blocks, exp_body, e_cur)
        l_ref[0] = jnp.sum(l, axis=0, keepdims=True)


def _sample_values_kernel(pt_ref, gate_ref, vnew_ref, p_ref, pcur_ref, l_ref, *rest, n_pages, pps, ts):
    del pt_ref
    vc = rest[:pps]
    o_ref, acc_ref = rest[pps:]
    g = pl.program_id(1)
    pg = PAGE_SIZE

    @pl.when(g == 0)
    def _own_block():
        acc_ref[...] = _dot_tn(pcur_ref[0], _pad_rows(_cat_heads(vnew_ref), pg).astype(BF16))

    vals = jnp.concatenate([_cat_heads(vc[u], 0).astype(BF16) for u in range(pps)], axis=0)
    acc_ref[...] = acc_ref[...] + _dot_tn(p_ref[0], vals)

    @pl.when(g == n_pages // pps - 1)
    def _finish():
        r = lax.broadcasted_iota(jnp.int32, (LANES, LANES), 0)
        c = lax.broadcasted_iota(jnp.int32, (LANES, LANES), 1)
        lcol = jnp.sum(jnp.where(r == c, jnp.broadcast_to(l_ref[0], (LANES, LANES)), 0.0), axis=1, keepdims=True)
        for h in range(MB_HEADS):
            cols = slice(h * HEAD_DIM, (h + 1) * HEAD_DIM)
            o = acc_ref[h * ts:(h + 1) * ts, cols] / lcol[h * ts:(h + 1) * ts, :]
            o_ref[:, cols] = o * _silu(gate_ref[:, cols])


def _moba_sample(z, k_s, v_s, cache_k, cache_v, page_table, bs, cfar, tp, ts, *, name):
    nb, n_pages = page_table.shape
    pps = math.gcd(n_pages, MAX_PAGES_PER_STEP)
    assert (pps * PAGE_SIZE) % MB_BLOCK == 0
    assert ts & (ts - 1) == 0 and MB_HEADS * ts <= LANES and tp % ts == 0
    n_steps = n_pages // pps
    n_blocks = n_pages * PAGE_SIZE // MB_BLOCK
    n_keys = (n_pages + 1) * PAGE_SIZE
    kc = cache_k.transpose(0, 2, 1, 3)
    vc = cache_v.transpose(0, 2, 1, 3)
    rblk = tp // ts
    row_spec = lambda col: pl.BlockSpec((ts, MB_WIDTH), lambda b, g, pt: (rblk + b, col))
    new_spec = pl.BlockSpec((MB_HEADS, ts, HEAD_DIM), lambda b, g, pt: (0, b, 0))
    page_spec = lambda u: pl.BlockSpec((1, MB_HEADS, PAGE_SIZE, HEAD_DIM), lambda b, g, pt: (pt[b, g * pps + u], 0, 0, 0))
    const2 = lambda b, g, pt: (0, 0)

    probs, denom = pl.pallas_call(
        functools.partial(_sample_scores_kernel, n_pages=n_pages, pps=pps, ts=ts),
        grid_spec=pltpu.PrefetchScalarGridSpec(
            num_scalar_prefetch=1,
            grid=(nb, n_steps),
            in_specs=[row_spec(0), new_spec] + [page_spec(u) for u in range(pps)]
                     + [pl.BlockSpec(bs.shape, const2), pl.BlockSpec(cfar.shape, const2)],
            out_specs=[pl.BlockSpec((1, n_keys, LANES), lambda b, g, pt: (b, 0, 0)),
                       pl.BlockSpec((1, 1, LANES), lambda b, g, pt: (b, 0, 0))],
            scratch_shapes=[pltpu.VMEM((LANES, MB_WIDTH), BF16),
                            pltpu.VMEM((LANES, MB_WIDTH), BF16),
                            pltpu.VMEM((n_pages * PAGE_SIZE, LANES), F32),
                            pltpu.VMEM((n_blocks, MB_WIDTH), F32),
                            pltpu.VMEM((n_blocks, LANES), F32)],
        ),
        out_shape=[jax.ShapeDtypeStruct((nb, n_keys, LANES), BF16),
                   jax.ShapeDtypeStruct((nb, 1, LANES), F32)],
        compiler_params=_params("arbitrary", "arbitrary"),
        name=name + "_scores",
    )(page_table, z, k_s, *([kc] * pps), bs, cfar)

    return pl.pallas_call(
        functools.partial(_sample_values_kernel, n_pages=n_pages, pps=pps, ts=ts),
        grid_spec=pltpu.PrefetchScalarGridSpec(
            num_scalar_prefetch=1,
            grid=(nb, n_steps),
            in_specs=[row_spec(1), new_spec,
                      pl.BlockSpec((1, pps * PAGE_SIZE, LANES), lambda b, g, pt: (b, g, 0)),
                      pl.BlockSpec((1, PAGE_SIZE, LANES), lambda b, g, pt: (b, n_pages, 0)),
                      pl.BlockSpec((1, 1, LANES), lambda b, g, pt: (b, 0, 0))]
                     + [page_spec(u) for u in range(pps)],
            out_specs=pl.BlockSpec((ts, MB_WIDTH), lambda b, g, pt: (b, 0)),
            scratch_shapes=[pltpu.VMEM((LANES, MB_WIDTH), F32)],
        ),
        out_shape=jax.ShapeDtypeStruct((nb * ts, MB_WIDTH), F32),
        compiler_params=_params("arbitrary", "arbitrary"),
        name=name + "_values",
    )(page_table, z, v_s, probs, probs, denom, *([vc] * pps))


def kernel(x_prompt, x_sample, cache_k, cache_v, cache_mem_k, cache_mem_v, state_hgrn, page_table, mem_prompt,
           g_norm, w_in_a, hg_lb, g_hg_out, w_out_a, w_in_b, w_out_b, g_kv, w_kv, rel_bias, g_mem, w_mem_kv,
           g_final):
    bp, tp, d = x_prompt.shape
    bs_, ts, _ = x_sample.shape
    assert bp == 1 and w_in_a.shape[0] == 1 and w_in_b.shape[0] == 1
    n_pages = page_table.shape[1]
    assert (n_pages * PAGE_SIZE) % MB_BLOCK == 0 and tp % MB_BLOCK == 0
    rows_s = bs_ * ts

    x0 = jnp.concatenate([x_prompt.reshape(tp, d), x_sample.reshape(rows_s, d)], axis=0)
    bf = lambda w: w.astype(BF16)

    mem_k, mem_v = [], []
    for l in range(2):
        mkv = _norm_matmul(mem_prompt.reshape(-1, d), g_mem[l], bf(w_mem_kv[l]), name=f"mem_kv_{l}")
        mem_k.append(mkv[:, :MEM_WIDTH])
        mem_v.append(mkv[:, MEM_WIDTH:])
    mem_len = mem_k[0].shape[0]

    def mem_attn(z, col_q, l, tag):
        mix_p = _mem_attn(z, col_q, mem_k[l][None], mem_v[l][None], 0, tp, BF16, name=f"mem_attn_{tag}_prompt")
        mix_s = _mem_attn(z, col_q, cache_mem_k[l].reshape(bs_, mem_len, MEM_WIDTH),
                          cache_mem_v[l].reshape(bs_, mem_len, MEM_WIDTH), tp, ts, F32, name=f"mem_attn_{tag}_sample")
        return mix_p, mix_s

    z = _norm_matmul(x0, g_norm[0], bf(w_in_a[0]), name="in_proj_a")
    mix_p, s_prompt = _hgrn_prompt(z, hg_lb, g_hg_out[0], tp, 0, name="hgrn_prompt")
    mix_s, s_sample = _hgrn_sample(z, hg_lb, g_hg_out[0], state_hgrn[0], tp, ts, 0, name="hgrn_sample")
    mem_p, mem_s = mem_attn(z, 4 * HG_WIDTH // MEM_WIDTH, 0, "a")
    x1 = _out_proj(mix_p, mix_s, mem_p, mem_s, bf(w_out_a[0]), x0, name="out_proj_a")

    k_p, v_p, kb_p, vb_p = _shared_kv(x1, g_kv, bf(w_kv), 0, tp, True, name="shared_kv_prompt")
    k_s, v_s = _shared_kv(x1, g_kv, bf(w_kv), tp, rows_s, False, name="shared_kv_sample")
    kmean = _block_mean(k_p, tp // MB_BLOCK)

    blk = MB_BLOCK
    c = jnp.arange(2 * blk, dtype=jnp.int32)
    dist_p = (jnp.arange(NEAR_BLOCKS, dtype=jnp.int32)[:, None] * blk + blk - c[None, :]).reshape(-1)
    brows = _bias_table(rel_bias, dist_p)[:MB_HEADS].reshape(MB_HEADS, NEAR_BLOCKS, 2 * blk)
    near_pages = min(NEAR_PAGES, n_pages)
    n_keys = (near_pages + 1) * PAGE_SIZE
    key_x = jnp.arange(n_keys, dtype=jnp.int32)
    dist_s = (near_pages * PAGE_SIZE - key_x[None, :] + jnp.arange(ts, dtype=jnp.int32)[:, None]).reshape(-1)
    bs_tab = _bias_table(rel_bias, dist_s)[:MB_HEADS].reshape(MB_HEADS, ts, n_keys)
    bs_tab = jnp.pad(bs_tab.transpose(2, 0, 1).reshape(n_keys, MB_HEADS * ts), ((0, 0), (0, LANES - MB_HEADS * ts)))
    cfar = jnp.pad(jnp.repeat(brows[:, NEAR_BLOCKS - 1, 0], ts), (0, LANES - MB_HEADS * ts)).reshape(1, LANES)

    zb = _norm_matmul(x1, g_norm[1], bf(w_in_b[0]), name="in_proj_b")
    mix_p = _moba_prompt(zb, kb_p, vb_p, kmean, brows, tp, name="moba_prompt")
    mix_s = _moba_sample(zb, k_s, v_s, cache_k, cache_v, page_table, bs_tab, cfar, tp, ts, name="moba_sample")
    mem_p, mem_s = mem_attn(zb, 2 * MB_WIDTH // MEM_WIDTH, 1, "b")
    y_p, y_s = _out_proj(mix_p, mix_s, mem_p, mem_s, bf(w_out_b[0]), x1, g_final, name="out_proj_b")

    heads = lambda a, b_, t: a.reshape(MB_HEADS, b_, t, HEAD_DIM).transpose(1, 2, 0, 3)
    memh = lambda parts: jnp.stack(parts).reshape(2, bp, mem_len, MEM_HEADS, HEAD_DIM)
    return (y_p.reshape(bp, tp, d), y_s.reshape(bs_, ts, d),
            heads(k_p, bp, tp), heads(v_p, bp, tp), heads(k_s, bs_, ts), heads(v_s, bs_, ts),
            s_prompt[None, None].astype(state_hgrn.dtype), s_sample[None].astype(state_hgrn.dtype),
            memh(mem_k), memh(mem_v))
```

```python
import functools
import math

import jax
import jax.numpy as jnp
from jax import lax
from jax.experimental import pallas as pl
from jax.experimental.pallas import tpu as pltpu

F32 = jnp.float32
BF16 = jnp.bfloat16

HEAD_DIM = 128
HG_HEADS = 12
MB_HEADS = 12
MEM_HEADS = 4
MB_BLOCK = 256
MB_TOPK = 3
PAGE_SIZE = 128
N_BUCKETS = 32
MAX_DISTANCE = 1024
EPS = 1e-6
HG_WIDTH = HG_HEADS * HEAD_DIM
MB_WIDTH = MB_HEADS * HEAD_DIM
MEM_WIDTH = MEM_HEADS * HEAD_DIM

SUBLANES = 8
LANES = 128
HG_CHUNK = 128
HG_BLOCK_CHUNKS = 4
HG_SAMPLE_UNROLL = 2
NEAR_BLOCKS = 5
NEAR_PAGES = (NEAR_BLOCKS - 1) * MB_BLOCK // PAGE_SIZE
MAX_PAGES_PER_STEP = 16
VMEM_LIMIT = 48 * 1024 * 1024

NT_DIMS = (((1,), (1,)), ((), ()))
TN_DIMS = (((0,), (0,)), ((), ()))


def _dot(a, b):
    return jnp.dot(a, b, preferred_element_type=F32)


def _dot_nt(a, b):
    return lax.dot_general(a, b, NT_DIMS, preferred_element_type=F32)


def _dot_tn(a, b):
    return lax.dot_general(a, b, TN_DIMS, preferred_element_type=F32)


def _sigmoid(x):
    return 1.0 / (1.0 + jnp.exp(-x))


def _silu(x):
    return x * _sigmoid(x)


def _split2(x):
    hi = x.astype(BF16)
    lo = (x - hi.astype(F32)).astype(BF16)
    return hi, lo


def _split3(x):
    hi = x.astype(BF16)
    r = x - hi.astype(F32)
    mid = r.astype(BF16)
    lo = (r - mid.astype(F32)).astype(BF16)
    return hi, mid, lo


def _row_tile(m, cap, mult):
    best = None
    for t in range(mult, min(m, cap) + 1, mult):
        if m % t == 0:
            best = t
    assert best is not None, (m, cap, mult)
    return best


def _params(*sem):
    return pltpu.CompilerParams(dimension_semantics=sem, vmem_limit_bytes=VMEM_LIMIT)


def _pad_rows(x, rows):
    return jnp.concatenate([x, jnp.zeros((rows - x.shape[0], x.shape[1]), x.dtype)], axis=0)


def _normalize_rows(x_ref, g_ref, xn_ref):
    rows = x_ref.shape[0]
    step = 128 if rows % 128 == 0 else rows
    for r in range(0, rows, step):
        x = x_ref[r:r + step, :]
        ms = jnp.mean(x * x, axis=-1, keepdims=True)
        xn_ref[r:r + step, :] = (x * lax.rsqrt(ms + EPS) * g_ref[...]).astype(BF16)


def _norm_matmul_kernel(x_ref, g_ref, w_ref, o_ref, xn_ref):
    @pl.when(pl.program_id(1) == 0)
    def _():
        _normalize_rows(x_ref, g_ref, xn_ref)

    o_ref[...] = _dot(xn_ref[...], w_ref[...])


def _norm_matmul(x, g, w, *, name):
    m, d = x.shape
    n = w.shape[1]
    tm = _row_tile(m, 768, 256)
    tn = _row_tile(n, 1024, 256)
    return pl.pallas_call(
        _norm_matmul_kernel,
        grid=(m // tm, n // tn),
        in_specs=[pl.BlockSpec((tm, d), lambda i, j: (i, 0)),
                  pl.BlockSpec((1, d), lambda i, j: (0, 0)),
                  pl.BlockSpec((d, tn), lambda i, j: (0, j))],
        out_specs=pl.BlockSpec((tm, tn), lambda i, j: (i, j)),
        out_shape=jax.ShapeDtypeStruct((m, n), F32),
        scratch_shapes=[pltpu.VMEM((tm, d), BF16)],
        compiler_params=_params("parallel", "arbitrary"),
        name=name,
    )(x, g.reshape(1, d), w)


def _shared_kv_kernel(x_ref, g_ref, w_ref, *rest):
    *o_refs, xn_ref = rest
    j = pl.program_id(1)

    @pl.when(j == 0)
    def _():
        _normalize_rows(x_ref, g_ref, xn_ref)

    y = _dot(xn_ref[...], w_ref[...])

    def write(refs):
        for h in range(MB_HEADS):
            for ref in refs:
                ref[h] = y[:, h * HEAD_DIM:(h + 1) * HEAD_DIM].astype(ref.dtype)

    @pl.when(j == 0)
    def _():
        write(o_refs[0::2])

    @pl.when(j == 1)
    def _():
        write(o_refs[1::2])


def _shared_kv(x, g, w, row0, rows, with_bf16, *, name):
    d = x.shape[1]
    tm = _row_tile(rows, 512, 256)
    assert row0 % tm == 0 and w.shape[1] == 2 * MB_WIDTH
    base = row0 // tm
    shape = (MB_HEADS, rows, HEAD_DIM)
    spec = pl.BlockSpec((MB_HEADS, tm, HEAD_DIM), lambda i, j: (0, i, 0))
    dtypes = [F32, F32] + ([BF16, BF16] if with_bf16 else [])
    return pl.pallas_call(
        _shared_kv_kernel,
        grid=(rows // tm, 2),
        in_specs=[pl.BlockSpec((tm, d), lambda i, j: (base + i, 0)),
                  pl.BlockSpec((1, d), lambda i, j: (0, 0)),
                  pl.BlockSpec((d, MB_WIDTH), lambda i, j: (0, j))],
        out_specs=[spec] * len(dtypes),
        out_shape=[jax.ShapeDtypeStruct(shape, t) for t in dtypes],
        scratch_shapes=[pltpu.VMEM((tm, d), BF16)],
        compiler_params=_params("parallel", "arbitrary"),
        name=name,
    )(x, g.reshape(1, d), w)


def _out_proj_kernel(ap_ref, as_ref, bp_ref, bs_ref, wa_ref, wb_ref, x_ref, *rest, prompt_tiles, final_norm):
    if final_norm:
        g_ref, yp_ref, ys_ref = rest
    else:
        (yp_ref,) = rest
        ys_ref = yp_ref
    i = pl.program_id(0)

    def run(a_ref, b_ref, o_ref):
        y = _dot(a_ref[...].astype(BF16), wa_ref[...]) + _dot(b_ref[...].astype(BF16), wb_ref[...])
        y = x_ref[...] + y
        if final_norm:
            ms = jnp.mean(y * y, axis=-1, keepdims=True)
            y = y * lax.rsqrt(ms + EPS) * g_ref[...]
        o_ref[...] = y

    @pl.when(i < prompt_tiles)
    def _():
        run(ap_ref, bp_ref, yp_ref)

    @pl.when(i >= prompt_tiles)
    def _():
        run(as_ref, bs_ref, ys_ref)


def _out_proj(main_p, main_s, mem_p, mem_s, w, x, g_final=None, *, name):
    m, d = x.shape
    tp, rows_s = main_p.shape[0], main_s.shape[0]
    wm, wmem = main_p.shape[1], mem_p.shape[1]
    assert wm % wmem == 0 and tp + rows_s == m
    tm = _row_tile(math.gcd(tp, rows_s), 256, 8)
    pt = tp // tm
    p_map = lambda i: (jnp.minimum(i, pt - 1), 0)
    s_map = lambda i: (jnp.maximum(i - pt, 0), 0)
    in_specs = [pl.BlockSpec((tm, wm), p_map), pl.BlockSpec((tm, wm), s_map),
                pl.BlockSpec((tm, wmem), p_map), pl.BlockSpec((tm, wmem), s_map),
                pl.BlockSpec((wm, d), lambda i: (0, 0)),
                pl.BlockSpec((wmem, d), lambda i: (wm // wmem, 0)),
                pl.BlockSpec((tm, d), lambda i: (i, 0))]
    args = [main_p, main_s, mem_p, mem_s, w, w, x]
    if g_final is None:
        out_specs = pl.BlockSpec((tm, d), lambda i: (i, 0))
        out_shape = jax.ShapeDtypeStruct((m, d), F32)
    else:
        in_specs.append(pl.BlockSpec((1, d), lambda i: (0, 0)))
        args.append(g_final.reshape(1, d))
        out_specs = [pl.BlockSpec((tm, d), p_map), pl.BlockSpec((tm, d), s_map)]
        out_shape = [jax.ShapeDtypeStruct((tp, d), F32), jax.ShapeDtypeStruct((rows_s, d), F32)]
    return pl.pallas_call(
        functools.partial(_out_proj_kernel, prompt_tiles=pt, final_norm=g_final is not None),
        grid=(m // tm,),
        in_specs=in_specs,
        out_specs=out_specs,
        out_shape=out_shape,
        compiler_params=_params("arbitrary"),
        name=name,
    )(*args)


def _mem_attn_kernel(q_ref, gate_ref, mk_ref, mv_ref, o_ref):
    scale = HEAD_DIM ** -0.5
    for h in range(MEM_HEADS):
        cols = slice(h * HEAD_DIM, (h + 1) * HEAD_DIM)
        q = q_ref[:, cols].astype(BF16)
        k = mk_ref[0, :, cols].astype(BF16)
        v = mv_ref[0, :, cols].astype(BF16)
        s = _dot_nt(q, k) * scale
        e = jnp.exp(s - jnp.max(s, axis=-1, keepdims=True))
        o = _dot(e.astype(BF16), v) / jnp.sum(e, axis=-1, keepdims=True)
        o_ref[:, cols] = (o * _silu(gate_ref[:, cols])).astype(o_ref.dtype)


def _mem_attn(z, col_q, mk, mv, row0, rows_per_batch, out_dtype, *, name):
    nb, mem_len, _ = mk.shape
    tm = _row_tile(rows_per_batch, 512, 8)
    tiles = rows_per_batch // tm
    assert row0 % tm == 0
    base = row0 // tm
    return pl.pallas_call(
        _mem_attn_kernel,
        grid=(nb, tiles),
        in_specs=[pl.BlockSpec((tm, MEM_WIDTH), lambda b, t: (base + b * tiles + t, col_q)),
                  pl.BlockSpec((tm, MEM_WIDTH), lambda b, t: (base + b * tiles + t, col_q + 1)),
                  pl.BlockSpec((1, mem_len, MEM_WIDTH), lambda b, t: (b, 0, 0)),
                  pl.BlockSpec((1, mem_len, MEM_WIDTH), lambda b, t: (b, 0, 0))],
        out_specs=pl.BlockSpec((tm, MEM_WIDTH), lambda b, t: (b * tiles + t, 0)),
        out_shape=jax.ShapeDtypeStruct((nb * rows_per_batch, MEM_WIDTH), out_dtype),
        compiler_params=_params("parallel", "parallel"),
        name=name,
    )(z, z, mk, mv)


def _cumsum_rows(x):
    n = x.shape[0]
    row = lax.broadcasted_iota(jnp.int32, x.shape, 0)
    s = 1
    while s < n:
        x = x + jnp.where(row >= s, pltpu.roll(x, s, 0), 0.0)
        s *= 2
    return x


def _lower_bound(lbp, layer):
    e = jnp.exp(lbp - jnp.max(lbp, axis=0, keepdims=True))
    return jnp.sum(e[:layer + 1], axis=0, keepdims=True) / jnp.sum(e, axis=0, keepdims=True)


def _hgrn_gates(hq, hf, lb):
    q = _silu(hq)
    f = lb + (1.0 - lb) * _sigmoid(hf)
    return q, 1.0 - f, jnp.log(f)


def _hgrn_finish(o, hg, go):
    ms = jnp.mean(o * o, axis=-1, keepdims=True)
    return o * lax.rsqrt(ms + EPS) * go * _silu(hg)


def _diag_tiles(q, k, b, width):
    sub = SUBLANES
    lane = lax.broadcasted_iota(jnp.int32, (sub, width), 1)
    trow = lax.broadcasted_iota(jnp.int32, (sub, width), 0)
    tiles = []
    for r0 in range(0, q.shape[0], sub):
        q8, k8, b8 = q[r0:r0 + sub], k[r0:r0 + sub], b[r0:r0 + sub]
        tile = jnp.zeros((sub, width), F32)
        for s in range(sub):
            e = jnp.exp(b8 - b8[s:s + 1])
            a_col = jnp.sum(q8 * e * k8[s:s + 1], axis=1, keepdims=True)
            tile = jnp.where(lane == r0 + s, a_col, tile)
        tiles.append(jnp.where(trow >= lane - r0, tile, 0.0))
    return jnp.concatenate(tiles, axis=0)


def _level_ref(b, level):
    parts = []
    for start in range(0, b.shape[0], 2 * level):
        r = start + level - 1
        parts.append(jnp.broadcast_to(b[r:r + 1, :], (2 * level, b.shape[1])))
    return jnp.concatenate(parts, axis=0)


def _level_masks(c):
    row = lax.broadcasted_iota(jnp.int32, (c, c), 0)
    col = lax.broadcasted_iota(jnp.int32, (c, c), 1)
    masks = []
    level = SUBLANES
    while level < c:
        sh = level.bit_length() - 1
        same = (row >> (sh + 1)) == (col >> (sh + 1))
        masks.append((level, same & (((row >> sh) & 1) == 1) & (((col >> sh) & 1) == 0)))
        level *= 2
    return masks


def _hgrn_chunk(q, k, g, v, st, masks):
    c = HG_CHUNK
    b = _cumsum_rows(g)
    a = _diag_tiles(q, k, b, c)
    for level, mask in masks:
        e = jnp.exp(-jnp.abs(b - _level_ref(b, level)))
        a = jnp.where(mask, _dot_nt((q * e).astype(BF16), (k * e).astype(BF16)), a)
    b_end = b[c - 1:c, :]
    vb = v.astype(BF16)
    o = _dot(a.astype(BF16), vb) + _dot_nt((q * jnp.exp(b)).astype(BF16), st.astype(BF16))
    st_new = st * jnp.exp(b_end) + _dot_tn(vb, (k * jnp.exp(b_end - b)).astype(BF16))
    return o, st_new


def _hgrn_prompt_kernel(hq_ref, hf_ref, hi_ref, hg_ref, lbp_ref, go_ref, mix_ref, sout_ref, st_ref, *, layer):
    n = pl.program_id(1)

    @pl.when(n == 0)
    def _():
        st_ref[...] = jnp.zeros_like(st_ref)

    lb = _lower_bound(lbp_ref[...], layer)
    go = go_ref[...]
    masks = _level_masks(HG_CHUNK)
    for c in range(HG_BLOCK_CHUNKS):
        rows = slice(c * HG_CHUNK, (c + 1) * HG_CHUNK)
        q, k, g = _hgrn_gates(hq_ref[rows, :], hf_ref[rows, :], lb)
        o, st_new = _hgrn_chunk(q, k, g, hi_ref[rows, :], st_ref[...], masks)
        st_ref[...] = st_new
        mix_ref[rows, :] = _hgrn_finish(o, hg_ref[rows, :], go).astype(mix_ref.dtype)

    @pl.when(n == pl.num_programs(1) - 1)
    def _():
        sout_ref[0] = st_ref[...].T


def _hgrn_prompt(z, hg_lb, g_o, tp, layer, *, name):
    rb = HG_CHUNK * HG_BLOCK_CHUNKS
    assert tp % rb == 0
    h_ = HG_HEADS
    n_lb = hg_lb.shape[0]
    zspec = lambda off: pl.BlockSpec((rb, HEAD_DIM), lambda h, n: (n, off + h))
    return pl.pallas_call(
        functools.partial(_hgrn_prompt_kernel, layer=layer),
        grid=(h_, tp // rb),
        in_specs=[zspec(0), zspec(h_), zspec(2 * h_), zspec(3 * h_),
                  pl.BlockSpec((n_lb, HEAD_DIM), lambda h, n: (0, h)),
                  pl.BlockSpec((1, HEAD_DIM), lambda h, n: (0, h))],
        out_specs=[pl.BlockSpec((rb, HEAD_DIM), lambda h, n: (n, h)),
                   pl.BlockSpec((1, HEAD_DIM, HEAD_DIM), lambda h, n: (h, 0, 0))],
        out_shape=[jax.ShapeDtypeStruct((tp, HG_WIDTH), BF16),
                   jax.ShapeDtypeStruct((h_, HEAD_DIM, HEAD_DIM), F32)],
        scratch_shapes=[pltpu.VMEM((HEAD_DIM, HEAD_DIM), F32)],
        compiler_params=_params("parallel", "arbitrary"),
        name=name,
    )(z, z, z, z, hg_lb, g_o.reshape(1, HG_WIDTH))


def _hgrn_sample_kernel(hq_ref, hf_ref, hi_ref, hg_ref, lbp_ref, go_ref, s0_ref, mix_ref, sout_ref, *, layer, ts, nb):
    lb = _lower_bound(lbp_ref[...], layer)
    go = go_ref[...]

    def one_sequence(bi):
        rows = pl.ds(pl.multiple_of(bi * ts, ts), ts)
        q, k, g = _hgrn_gates(hq_ref[rows, :], hf_ref[rows, :], lb)
        b = _cumsum_rows(g)
        b_end = b[ts - 1:ts, :]
        st = s0_ref[bi, 0].T
        a = _diag_tiles(q, k, b, LANES)
        vb = _pad_rows(hi_ref[rows, :], LANES).astype(BF16)
        kh = _pad_rows(k * jnp.exp(b_end - b), LANES).astype(BF16)
        o = _dot(a.astype(BF16), vb) + _dot_nt((q * jnp.exp(b)).astype(BF16), st.astype(BF16))
        st_new = st * jnp.exp(b_end) + _dot_tn(vb, kh)
        sout_ref[bi, 0] = st_new.T
        mix_ref[rows, :] = _hgrn_finish(o, hg_ref[rows, :], go)

    def body(p, carry):
        for e in range(HG_SAMPLE_UNROLL):
            one_sequence(HG_SAMPLE_UNROLL * p + e)
        return carry

    lax.fori_loop(0, nb // HG_SAMPLE_UNROLL, body, 0)


def _hgrn_sample(z, hg_lb, g_o, s0, tp, ts, layer, *, name):
    nb = s0.shape[0]
    rows = nb * ts
    assert tp % rows == 0 and ts == SUBLANES and nb % HG_SAMPLE_UNROLL == 0
    h_ = HG_HEADS
    n_lb = hg_lb.shape[0]
    rblk = tp // rows
    zspec = lambda off: pl.BlockSpec((rows, HEAD_DIM), lambda h: (rblk, off + h))
    return pl.pallas_call(
        functools.partial(_hgrn_sample_kernel, layer=layer, ts=ts, nb=nb),
        grid=(h_,),
        in_specs=[zspec(0), zspec(h_), zspec(2 * h_), zspec(3 * h_),
                  pl.BlockSpec((n_lb, HEAD_DIM), lambda h: (0, h)),
                  pl.BlockSpec((1, HEAD_DIM), lambda h: (0, h)),
                  pl.BlockSpec((nb, 1, HEAD_DIM, HEAD_DIM), lambda h: (0, h, 0, 0))],
        out_specs=[pl.BlockSpec((rows, HEAD_DIM), lambda h: (0, h)),
                   pl.BlockSpec((nb, 1, HEAD_DIM, HEAD_DIM), lambda h: (0, h, 0, 0))],
        out_shape=[jax.ShapeDtypeStruct((rows, HG_WIDTH), F32),
                   jax.ShapeDtypeStruct(s0.shape, F32)],
        compiler_params=_params("parallel"),
        name=name,
    )(z, z, z, z, hg_lb, g_o.reshape(1, HG_WIDTH), s0)


def _t5_bucket(dist):
    exact = N_BUCKETS // 2
    d = jnp.maximum(dist, exact).astype(F32)
    large = exact + (jnp.log(d / exact) / math.log(MAX_DISTANCE / exact) * (N_BUCKETS - exact)).astype(jnp.int32)
    return jnp.where(dist < exact, dist, jnp.minimum(large, N_BUCKETS - 1))


def _bias_table_kernel(rb_ref, bucket_ref, o_ref):
    hi, mid, lo = _split3(rb_ref[...])
    row = lax.broadcasted_iota(jnp.int32, (LANES, bucket_ref.shape[1]), 0)
    oh = jnp.where(row == bucket_ref[...], 1.0, 0.0).astype(BF16)
    o_ref[...] = _dot(hi, oh) + _dot(mid, oh) + _dot(lo, oh)


def _bias_table(rel_bias, dist):
    n = dist.shape[0]
    rb = jnp.zeros((16, LANES), F32).at[:MB_HEADS, :N_BUCKETS].set(rel_bias.T)
    bucket = _t5_bucket(jnp.maximum(dist, 0).astype(jnp.int32)).reshape(1, n)
    return pl.pallas_call(
        _bias_table_kernel,
        out_shape=jax.ShapeDtypeStruct((16, n), F32),
        name="bias_table",
    )(rb, bucket)


def _block_mean_kernel(k_ref, o_ref):
    for n in range(o_ref.shape[0]):
        rows = slice(n * MB_BLOCK, (n + 1) * MB_BLOCK)
        o_ref[n:n + 1, :] = jnp.sum(k_ref[rows, :], axis=0, keepdims=True) * (1.0 / MB_BLOCK)


def _block_mean(k, n_blocks):
    return pl.pallas_call(
        _block_mean_kernel,
        grid=(MB_HEADS,),
        in_specs=[pl.BlockSpec((None, n_blocks * MB_BLOCK, HEAD_DIM), lambda h: (h, 0, 0))],
        out_specs=pl.BlockSpec((None, n_blocks, HEAD_DIM), lambda h: (h, 0, 0)),
        out_shape=jax.ShapeDtypeStruct((MB_HEADS, n_blocks, HEAD_DIM), F32),
        compiler_params=_params("parallel"),
        name="block_mean",
    )(k)


def _select_topk(gate, axis, n_blocks):
    idx = lax.broadcasted_iota(jnp.int32, gate.shape, axis)
    sel = jnp.zeros(gate.shape, F32)
    for _ in range(min(MB_TOPK, n_blocks)):
        mx = jnp.max(gate, axis=axis, keepdims=True)
        first = jnp.min(jnp.where(gate == mx, idx, n_blocks), axis=axis, keepdims=True)
        pick = idx == first
        sel = jnp.where(pick & (mx > -jnp.inf), 1.0, sel)
        gate = jnp.where(pick, -jnp.inf, gate)
    return sel


MASK_BIG = 2.0 ** 17
DUMMY_LANE = 125
FAR_LANES = (126, 127)
MOBA_GROUP_LOG2 = 3
MOBA_GROUP = 1 << MOBA_GROUP_LOG2


def _moba_prompt_kernel(q_ref, gate_ref, k_ref, v_ref, km_ref, brow_ref, o_ref, bias_ref, s_ref, mx_ref, acc_ref,
                        *, n_blocks):
    i = pl.program_id(1)
    blk = MB_BLOCK
    scale = HEAD_DIM ** -0.5
    inv_scale = HEAD_DIM ** 0.5
    near = bias_ref.shape[0]
    filler = n_blocks + near

    @pl.when(i == 0)
    def _build_bias():
        rowi = lax.broadcasted_iota(jnp.int32, (blk, 2 * blk), 0)
        for d in range(near):
            x = jnp.broadcast_to(brow_ref[0, d:d + 1, :], (blk, 2 * blk))
            for bit in range(blk.bit_length() - 1):
                x = jnp.where(((rowi >> bit) & 1) == 1, pltpu.roll(x, 1 << bit, 1), x)
            bias_ref[d] = x[:, blk:] * inv_scale
        s_ref[filler] = jnp.full((blk, blk), -MASK_BIG, F32)

    q = q_ref[...]
    qh, ql = _split2(q)
    kmh, kml = _split2(_pad_rows(km_ref[...], LANES))
    gate_t = _dot_nt(kmh, qh) + _dot_nt(kml, qh) + _dot_nt(kmh, ql)
    blk_id = lax.broadcasted_iota(jnp.int32, gate_t.shape, 0)
    sel = _select_topk(jnp.where(blk_id < i, gate_t, -jnp.inf), 0, n_blocks).T

    lane = lax.broadcasted_iota(jnp.int32, sel.shape, 1)
    qa = jnp.concatenate([qh, jnp.where(lane >= FAR_LANES[0], 1.0, sel - 1.0).astype(BF16)], axis=1)
    c_far = brow_ref[0, near - 1:near, 0:1] * inv_scale
    c_hi = c_far.astype(BF16).astype(F32)
    lane_r = lax.broadcasted_iota(jnp.int32, (1, LANES), 1)
    far_row = jnp.where(lane_r == FAR_LANES[0], c_hi, jnp.where(lane_r == FAR_LANES[1], c_far - c_hi, 0.0))

    def scores(j, big_lane, row_vals):
        rows = pl.ds(pl.multiple_of(jnp.minimum(j, i) * blk, blk), blk)
        right = jnp.broadcast_to(jnp.where(lane_r == big_lane, MASK_BIG, row_vals), (blk, LANES)).astype(BF16)
        return _dot_nt(qa, jnp.concatenate([k_ref[rows, :], right], axis=1))

    first_near = jnp.maximum(i - (near - 1), 0)
    mx_ref[...] = jnp.full((blk, blk), -MASK_BIG, F32)

    def far_group(p, carry):
        mx = mx_ref[...]
        for e in range(MOBA_GROUP):
            j = MOBA_GROUP * p + e
            raw = scores(j, jnp.where(j < first_near, j, DUMMY_LANE), far_row)
            s_ref[j] = raw
            mx = jnp.maximum(mx, raw)
        mx_ref[...] = mx
        return carry

    lax.fori_loop(0, (first_near + MOBA_GROUP - 1) >> MOBA_GROUP_LOG2, far_group, 0)

    tq = lax.broadcasted_iota(jnp.int32, (blk, blk), 0)
    tk = lax.broadcasted_iota(jnp.int32, (blk, blk), 1)
    mx = mx_ref[...]
    for dlt in range(near):
        j = i - dlt
        jc = jnp.maximum(j, 0)
        big_lane = -1 if dlt == 0 else jnp.where(j >= 0, jc, DUMMY_LANE)
        raw = scores(jc, big_lane, 0.0) + bias_ref[dlt]
        if dlt == 0:
            raw = jnp.where(tq >= tk, raw, -MASK_BIG)
        s_ref[jnp.where(j >= 0, jc, n_blocks + dlt)] = raw
        mx = jnp.maximum(mx, raw)
    mx_ref[...] = jnp.broadcast_to(jnp.max(mx, axis=1, keepdims=True), (blk, blk))

    ones = jnp.ones((blk, LANES), BF16)
    exp2_scale = scale * math.log2(math.e)

    def pv_part(first):
        probs, vals = [], []
        for e in range(MOBA_GROUP // 2):
            j = first + e
            tile = s_ref[jnp.where(j <= i, j, filler)]
            probs.append(jnp.exp2((tile - mx_ref[...]) * exp2_scale).astype(BF16))
            rows = pl.ds(pl.multiple_of(jnp.minimum(j, i) * blk, blk), blk)
            vals.append(jnp.concatenate([v_ref[rows, :], ones], axis=1))
        return _dot(jnp.concatenate(probs, axis=1), jnp.concatenate(vals, axis=0))

    def pv_group(p, carry):
        first = MOBA_GROUP * p
        acc_ref[...] = acc_ref[...] + (pv_part(first) + pv_part(first + MOBA_GROUP // 2))
        return carry

    acc_ref[...] = jnp.zeros_like(acc_ref)
    lax.fori_loop(0, (i + MOBA_GROUP) >> MOBA_GROUP_LOG2, pv_group, 0)
    acc = acc_ref[...]
    o_ref[...] = (acc[:, :HEAD_DIM] / acc[:, HEAD_DIM:] * _silu(gate_ref[...])).astype(o_ref.dtype)


def _moba_prompt(z, kb, vb, kmean, brows, tp, *, name):
    nq = tp // MB_BLOCK
    h_ = MB_HEADS
    assert nq <= DUMMY_LANE
    return pl.pallas_call(
        functools.partial(_moba_prompt_kernel, n_blocks=nq),
        grid=(h_, nq),
        in_specs=[pl.BlockSpec((MB_BLOCK, HEAD_DIM), lambda h, i: (i, h)),
                  pl.BlockSpec((MB_BLOCK, HEAD_DIM), lambda h, i: (i, h_ + h)),
                  pl.BlockSpec((None, tp, HEAD_DIM), lambda h, i: (h, 0, 0)),
                  pl.BlockSpec((None, tp, HEAD_DIM), lambda h, i: (h, 0, 0)),
                  pl.BlockSpec((None, nq, HEAD_DIM), lambda h, i: (h, 0, 0)),
                  pl.BlockSpec((1, NEAR_BLOCKS, 2 * MB_BLOCK), lambda h, i: (h, 0, 0))],
        out_specs=pl.BlockSpec((MB_BLOCK, HEAD_DIM), lambda h, i: (i, h)),
        out_shape=jax.ShapeDtypeStruct((tp, MB_WIDTH), BF16),
        scratch_shapes=[pltpu.VMEM((NEAR_BLOCKS, MB_BLOCK, MB_BLOCK), F32),
                        pltpu.VMEM((nq + NEAR_BLOCKS + 1, MB_BLOCK, MB_BLOCK), F32),
                        pltpu.VMEM((MB_BLOCK, MB_BLOCK), F32),
                        pltpu.VMEM((MB_BLOCK, MB_BLOCK), F32)],
        compiler_params=_params("parallel", "arbitrary"),
        name=name,
    )(z, z, kb, vb, kmean, brows)


def _cat_heads(ref, *lead):
    return jnp.concatenate([ref[lead + (h,)] for h in range(MB_HEADS)], axis=1)


def _sample_scores_kernel(pt_ref, q_ref, knew_ref, *rest, n_pages, pps, ts):
    del pt_ref
    kc = rest[:pps]
    bs_ref, cfar_ref, p_ref, l_ref, wq_ref, wql_ref, s_ref, km_ref, sel_ref = rest[pps:]
    g = pl.program_id(1)
    n_steps = n_pages // pps
    n_blocks = n_pages * PAGE_SIZE // MB_BLOCK
    near_pages = min(NEAR_PAGES, n_pages)
    scale = HEAD_DIM ** -0.5
    pg = PAGE_SIZE
    ppb = MB_BLOCK // pg

    @pl.when(g == 0)
    def _start_sequence():
        rep = _pad_rows(jnp.concatenate([q_ref[...]] * MB_HEADS, axis=0), LANES)
        r_h = lax.broadcasted_iota(jnp.int32, rep.shape, 0) // ts
        c_h = lax.broadcasted_iota(jnp.int32, rep.shape, 1) // HEAD_DIM
        hi, lo = _split2(jnp.where(r_h == c_h, rep, 0.0))
        wq_ref[...] = hi
        wql_ref[...] = lo

    pages = [_cat_heads(kc[u], 0) for u in range(pps)]
    st = _dot_nt(jnp.concatenate([kp.astype(BF16) for kp in pages], axis=0), wq_ref[...])
    for u in range(pps):
        p = g * pps + u
        near_idx = jnp.maximum(p - (n_pages - near_pages), 0)
        b_near = bs_ref[pl.ds(pl.multiple_of(near_idx * pg, pg), pg), :]
        bias = jnp.where(p >= n_pages - near_pages, b_near, cfar_ref[...])
        s_ref[pl.ds(pl.multiple_of(p * pg, pg), pg), :] = st[u * pg:(u + 1) * pg] * scale + bias
    for n in range(pps // ppb):
        ksum = sum(jnp.sum(pages[n * ppb + u], axis=0, keepdims=True) for u in range(ppb))
        km_ref[pl.ds(g * (pps // ppb) + n, 1), :] = ksum * (1.0 / MB_BLOCK)

    @pl.when(g == n_steps - 1)
    def _softmax():
        st_new = _dot_nt(_pad_rows(_cat_heads(knew_ref), pg).astype(BF16), wq_ref[...])
        krow = lax.broadcasted_iota(jnp.int32, (pg, LANES), 0)
        qcol = lax.broadcasted_iota(jnp.int32, (pg, LANES), 1)
        valid = (krow < ts) & (krow <= (qcol & (ts - 1)))
        s_cur = jnp.where(valid, st_new * scale + bs_ref[pl.ds(near_pages * pg, pg), :], -jnp.inf)

        kmh, kml = _split2(km_ref[...])
        gate = _dot_nt(kmh, wq_ref[...]) + _dot_nt(kmh, wql_ref[...]) + _dot_nt(kml, wq_ref[...])
        sel_ref[...] = _select_topk(gate, 0, n_blocks)

        def max_body(n, m):
            keep = sel_ref[pl.ds(n, 1), :] > 0.5
            for u in range(ppb):
                tile = s_ref[pl.ds(pl.multiple_of((n * ppb + u) * pg, pg), pg), :]
                m = jnp.maximum(m, jnp.where(keep, tile, -jnp.inf))
            return m

        m = lax.fori_loop(0, n_blocks, max_body, s_cur)
        mrow = jnp.max(m, axis=0, keepdims=True)

        def exp_body(n, l):
            keep = sel_ref[pl.ds(n, 1), :] > 0.5
            for u in range(ppb):
                rows = pl.ds(pl.multiple_of((n * ppb + u) * pg, pg), pg)
                e = jnp.exp(jnp.where(keep, s_ref[rows, :] - mrow, -jnp.inf))
                p_ref[0, rows, :] = e.astype(BF16)
                l = l + e
            return l

        e_cur = jnp.exp(s_cur - mrow)
        p_ref[0, pl.ds(n_pages * pg, pg), :] = e_cur.astype(BF16)
        l = lax.fori_loop(0, n_blocks, exp_body, e_cur)
        l_ref[0] = jnp.sum(l, axis=0, keepdims=True)


def _sample_values_kernel(pt_ref, gate_ref, vnew_ref, p_ref, pcur_ref, l_ref, *rest, n_pages, pps, ts):
    del pt_ref
    vc = rest[:pps]
    o_ref, acc_ref = rest[pps:]
    g = pl.program_id(1)
    pg = PAGE_SIZE

    @pl.when(g == 0)
    def _own_block():
        acc_ref[...] = _dot_tn(pcur_ref[0], _pad_rows(_cat_heads(vnew_ref), pg).astype(BF16))

    vals = jnp.concatenate([_cat_heads(vc[u], 0).astype(BF16) for u in range(pps)], axis=0)
    acc_ref[...] = acc_ref[...] + _dot_tn(p_ref[0], vals)

    @pl.when(g == n_pages // pps - 1)
    def _finish():
        r = lax.broadcasted_iota(jnp.int32, (LANES, LANES), 0)
        c = lax.broadcasted_iota(jnp.int32, (LANES, LANES), 1)
        lcol = jnp.sum(jnp.where(r == c, jnp.broadcast_to(l_ref[0], (LANES, LANES)), 0.0), axis=1, keepdims=True)
        for h in range(MB_HEADS):
            cols = slice(h * HEAD_DIM, (h + 1) * HEAD_DIM)
            o = acc_ref[h * ts:(h + 1) * ts, cols] / lcol[h * ts:(h + 1) * ts, :]
            o_ref[:, cols] = o * _silu(gate_ref[:, cols])


def _moba_sample(z, k_s, v_s, cache_k, cache_v, page_table, bs, cfar, tp, ts, *, name):
    nb, n_pages = page_table.shape
    pps = math.gcd(n_pages, MAX_PAGES_PER_STEP)
    assert (pps * PAGE_SIZE) % MB_BLOCK == 0
    assert ts & (ts - 1) == 0 and MB_HEADS * ts <= LANES and tp % ts == 0
    n_steps = n_pages // pps
    n_blocks = n_pages * PAGE_SIZE // MB_BLOCK
    n_keys = (n_pages + 1) * PAGE_SIZE
    kc = cache_k.transpose(0, 2, 1, 3)
    vc = cache_v.transpose(0, 2, 1, 3)
    rblk = tp // ts
    row_spec = lambda col: pl.BlockSpec((ts, MB_WIDTH), lambda b, g, pt: (rblk + b, col))
    new_spec = pl.BlockSpec((MB_HEADS, ts, HEAD_DIM), lambda b, g, pt: (0, b, 0))
    page_spec = lambda u: pl.BlockSpec((1, MB_HEADS, PAGE_SIZE, HEAD_DIM), lambda b, g, pt: (pt[b, g * pps + u], 0, 0, 0))
    const2 = lambda b, g, pt: (0, 0)

    probs, denom = pl.pallas_call(
        functools.partial(_sample_scores_kernel, n_pages=n_pages, pps=pps, ts=ts),
        grid_spec=pltpu.PrefetchScalarGridSpec(
            num_scalar_prefetch=1,
            grid=(nb, n_steps),
            in_specs=[row_spec(0), new_spec] + [page_spec(u) for u in range(pps)]
                     + [pl.BlockSpec(bs.shape, const2), pl.BlockSpec(cfar.shape, const2)],
            out_specs=[pl.BlockSpec((1, n_keys, LANES), lambda b, g, pt: (b, 0, 0)),
                       pl.BlockSpec((1, 1, LANES), lambda b, g, pt: (b, 0, 0))],
            scratch_shapes=[pltpu.VMEM((LANES, MB_WIDTH), BF16),
                            pltpu.VMEM((LANES, MB_WIDTH), BF16),
                            pltpu.VMEM((n_pages * PAGE_SIZE, LANES), F32),
                            pltpu.VMEM((n_blocks, MB_WIDTH), F32),
                            pltpu.VMEM((n_blocks, LANES), F32)],
        ),
        out_shape=[jax.ShapeDtypeStruct((nb, n_keys, LANES), BF16),
                   jax.ShapeDtypeStruct((nb, 1, LANES), F32)],
        compiler_params=_params("arbitrary", "arbitrary"),
        name=name + "_scores",
    )(page_table, z, k_s, *([kc] * pps), bs, cfar)

    return pl.pallas_call(
        functools.partial(_sample_values_kernel, n_pages=n_pages, pps=pps, ts=ts),
        grid_spec=pltpu.PrefetchScalarGridSpec(
            num_scalar_prefetch=1,
            grid=(nb, n_steps),
            in_specs=[row_spec(1), new_spec,
                      pl.BlockSpec((1, pps * PAGE_SIZE, LANES), lambda b, g, pt: (b, g, 0)),
                      pl.BlockSpec((1, PAGE_SIZE, LANES), lambda b, g, pt: (b, n_pages, 0)),
                      pl.BlockSpec((1, 1, LANES), lambda b, g, pt: (b, 0, 0))]
                     + [page_spec(u) for u in range(pps)],
            out_specs=pl.BlockSpec((ts, MB_WIDTH), lambda b, g, pt: (b, 0)),
            scratch_shapes=[pltpu.VMEM((LANES, MB_WIDTH), F32)],
        ),
        out_shape=jax.ShapeDtypeStruct((nb * ts, MB_WIDTH), F32),
        compiler_params=_params("arbitrary", "arbitrary"),
        name=name + "_values",
    )(page_table, z, v_s, probs, probs, denom, *([vc] * pps))


def kernel(x_prompt, x_sample, cache_k, cache_v, cache_mem_k, cache_mem_v, state_hgrn, page_table, mem_prompt,
           g_norm, w_in_a, hg_lb, g_hg_out, w_out_a, w_in_b, w_out_b, g_kv, w_kv, rel_bias, g_mem, w_mem_kv,
           g_final):
    bp, tp, d = x_prompt.shape
    bs_, ts, _ = x_sample.shape
    assert bp == 1 and w_in_a.shape[0] == 1 and w_in_b.shape[0] == 1
    n_pages = page_table.shape[1]
    assert (n_pages * PAGE_SIZE) % MB_BLOCK == 0 and tp % MB_BLOCK == 0
    rows_s = bs_ * ts

    x0 = jnp.concatenate([x_prompt.reshape(tp, d), x_sample.reshape(rows_s, d)], axis=0)
    bf = lambda w: w.astype(BF16)

    mem_k, mem_v = [], []
    for l in range(2):
        mkv = _norm_matmul(mem_prompt.reshape(-1, d), g_mem[l], bf(w_mem_kv[l]), name=f"mem_kv_{l}")
        mem_k.append(mkv[:, :MEM_WIDTH])
        mem_v.append(mkv[:, MEM_WIDTH:])
    mem_len = mem_k[0].shape[0]

    def mem_attn(z, col_q, l, tag):
        mix_p = _mem_attn(z, col_q, mem_k[l][None], mem_v[l][None], 0, tp, BF16, name=f"mem_attn_{tag}_prompt")
        mix_s = _mem_attn(z, col_q, cache_mem_k[l].reshape(bs_, mem_len, MEM_WIDTH),
                          cache_mem_v[l].reshape(bs_, mem_len, MEM_WIDTH), tp, ts, F32, name=f"mem_attn_{tag}_sample")
        return mix_p, mix_s

    z = _norm_matmul(x0, g_norm[0], bf(w_in_a[0]), name="in_proj_a")
    mix_p, s_prompt = _hgrn_prompt(z, hg_lb, g_hg_out[0], tp, 0, name="hgrn_prompt")
    mix_s, s_sample = _hgrn_sample(z, hg_lb, g_hg_out[0], state_hgrn[0], tp, ts, 0, name="hgrn_sample")
    mem_p, mem_s = mem_attn(z, 4 * HG_WIDTH // MEM_WIDTH, 0, "a")
    x1 = _out_proj(mix_p, mix_s, mem_p, mem_s, bf(w_out_a[0]), x0, name="out_proj_a")

    k_p, v_p, kb_p, vb_p = _shared_kv(x1, g_kv, bf(w_kv), 0, tp, True, name="shared_kv_prompt")
    k_s, v_s = _shared_kv(x1, g_kv, bf(w_kv), tp, rows_s, False, name="shared_kv_sample")
    kmean = _block_mean(k_p, tp // MB_BLOCK)

    blk = MB_BLOCK
    c = jnp.arange(2 * blk, dtype=jnp.int32)
    dist_p = (jnp.arange(NEAR_BLOCKS, dtype=jnp.int32)[:, None] * blk + blk - c[None, :]).reshape(-1)
    brows = _bias_table(rel_bias, dist_p)[:MB_HEADS].reshape(MB_HEADS, NEAR_BLOCKS, 2 * blk)
    near_pages = min(NEAR_PAGES, n_pages)
    n_keys = (near_pages + 1) * PAGE_SIZE
    key_x = jnp.arange(n_keys, dtype=jnp.int32)
    dist_s = (near_pages * PAGE_SIZE - key_x[None, :] + jnp.arange(ts, dtype=jnp.int32)[:, None]).reshape(-1)
    bs_tab = _bias_table(rel_bias, dist_s)[:MB_HEADS].reshape(MB_HEADS, ts, n_keys)
    bs_tab = jnp.pad(bs_tab.transpose(2, 0, 1).reshape(n_keys, MB_HEADS * ts), ((0, 0), (0, LANES - MB_HEADS * ts)))
    cfar = jnp.pad(jnp.repeat(brows[:, NEAR_BLOCKS - 1, 0], ts), (0, LANES - MB_HEADS * ts)).reshape(1, LANES)

    zb = _norm_matmul(x1, g_norm[1], bf(w_in_b[0]), name="in_proj_b")
    mix_p = _moba_prompt(zb, kb_p, vb_p, kmean, brows, tp, name="moba_prompt")
    mix_s = _moba_sample(zb, k_s, v_s, cache_k, cache_v, page_table, bs_tab, cfar, tp, ts, name="moba_sample")
    mem_p, mem_s = mem_attn(zb, 2 * MB_WIDTH // MEM_WIDTH, 1, "b")
    y_p, y_s = _out_proj(mix_p, mix_s, mem_p, mem_s, bf(w_out_b[0]), x1, g_final, name="out_proj_b")

    heads = lambda a, b_, t: a.reshape(MB_HEADS, b_, t, HEAD_DIM).transpose(1, 2, 0, 3)
    memh = lambda parts: jnp.stack(parts).reshape(2, bp, mem_len, MEM_HEADS, HEAD_DIM)
    return (y_p.reshape(bp, tp, d), y_s.reshape(bs_, ts, d),
            heads(k_p, bp, tp), heads(v_p, bp, tp), heads(k_s, bs_, ts), heads(v_s, bs_, ts),
            s_prompt[None, None].astype(state_hgrn.dtype), s_sample[None].astype(state_hgrn.dtype),
            memh(mem_k), memh(mem_v))
```

```python
import functools
import math

import jax
import jax.numpy as jnp
from jax import lax
from jax.experimental import pallas as pl
from jax.experimental.pallas import tpu as pltpu

F32 = jnp.float32
BF16 = jnp.bfloat16

HEAD_DIM = 128
HG_HEADS = 12
MB_HEADS = 12
MEM_HEADS = 4
MB_BLOCK = 256
MB_TOPK = 3
PAGE_SIZE = 128
N_BUCKETS = 32
MAX_DISTANCE = 1024
EPS = 1e-6
HG_WIDTH = HG_HEADS * HEAD_DIM
MB_WIDTH = MB_HEADS * HEAD_DIM
MEM_WIDTH = MEM_HEADS * HEAD_DIM

SUBLANES = 8
LANES = 128
HG_CHUNK = 128
HG_BLOCK_CHUNKS = 4
HG_SAMPLE_UNROLL = 2
NEAR_BLOCKS = 5
NEAR_PAGES = (NEAR_BLOCKS - 1) * MB_BLOCK // PAGE_SIZE
MAX_PAGES_PER_STEP = 16
VMEM_LIMIT = 48 * 1024 * 1024

NT_DIMS = (((1,), (1,)), ((), ()))
TN_DIMS = (((0,), (0,)), ((), ()))


def _dot(a, b):
    return jnp.dot(a, b, preferred_element_type=F32)


def _dot_nt(a, b):
    return lax.dot_general(a, b, NT_DIMS, preferred_element_type=F32)


def _dot_tn(a, b):
    return lax.dot_general(a, b, TN_DIMS, preferred_element_type=F32)


def _sigmoid(x):
    return 1.0 / (1.0 + jnp.exp(-x))


def _silu(x):
    return x * _sigmoid(x)


def _split2(x):
    hi = x.astype(BF16)
    lo = (x - hi.astype(F32)).astype(BF16)
    return hi, lo


def _split3(x):
    hi = x.astype(BF16)
    r = x - hi.astype(F32)
    mid = r.astype(BF16)
    lo = (r - mid.astype(F32)).astype(BF16)
    return hi, mid, lo


def _row_tile(m, cap, mult):
    best = None
    for t in range(mult, min(m, cap) + 1, mult):
        if m % t == 0:
            best = t
    assert best is not None, (m, cap, mult)
    return best


def _params(*sem):
    return pltpu.CompilerParams(dimension_semantics=sem, vmem_limit_bytes=VMEM_LIMIT)


def _pad_rows(x, rows):
    if rows == x.shape[0]:
        return x
    return jnp.concatenate([x, jnp.zeros((rows - x.shape[0], x.shape[1]), x.dtype)], axis=0)


def _normalize_rows(x_ref, g_ref, xn_ref):
    rows = x_ref.shape[0]
    step = 128 if rows % 128 == 0 else rows
    for r in range(0, rows, step):
        x = x_ref[r:r + step, :]
        ms = jnp.mean(x * x, axis=-1, keepdims=True)
        xn_ref[r:r + step, :] = (x * lax.rsqrt(ms + EPS) * g_ref[...]).astype(BF16)


def _norm_matmul_kernel(x_ref, g_ref, w_ref, o_ref, xn_ref):
    @pl.when(pl.program_id(1) == 0)
    def _():
        _normalize_rows(x_ref, g_ref, xn_ref)

    o_ref[...] = _dot(xn_ref[...], w_ref[...])


def _norm_matmul(x, g, w, *, name):
    m, d = x.shape
    n = w.shape[1]
    tm = _row_tile(m, 1024, 256)
    tn = _row_tile(n, 1024, 256)
    return pl.pallas_call(
        _norm_matmul_kernel,
        grid=(m // tm, n // tn),
        in_specs=[pl.BlockSpec((tm, d), lambda i, j: (i, 0)),
                  pl.BlockSpec((1, d), lambda i, j: (0, 0)),
                  pl.BlockSpec((d, tn), lambda i, j: (0, j))],
        out_specs=pl.BlockSpec((tm, tn), lambda i, j: (i, j)),
        out_shape=jax.ShapeDtypeStruct((m, n), F32),
        scratch_shapes=[pltpu.VMEM((tm, d), BF16)],
        compiler_params=_params("parallel", "arbitrary"),
        name=name,
    )(x, g.reshape(1, d), w)


def _shared_kv_kernel(x_ref, g_ref, w_ref, *rest):
    *o_refs, xn_ref = rest
    j = pl.program_id(1)

    @pl.when(j == 0)
    def _():
        _normalize_rows(x_ref, g_ref, xn_ref)

    y = _dot(xn_ref[...], w_ref[...])

    def write(refs):
        for h in range(MB_HEADS):
            for ref in refs:
                ref[h] = y[:, h * HEAD_DIM:(h + 1) * HEAD_DIM].astype(ref.dtype)

    @pl.when(j == 0)
    def _():
        write(o_refs[0::2])

    @pl.when(j == 1)
    def _():
        write(o_refs[1::2])


def _shared_kv(x, g, w, with_bf16, *, name):
    rows, d = x.shape
    tm = _row_tile(rows, 512, 256)
    assert w.shape[1] == 2 * MB_WIDTH
    shape = (MB_HEADS, rows, HEAD_DIM)
    spec = pl.BlockSpec((MB_HEADS, tm, HEAD_DIM), lambda i, j: (0, i, 0))
    dtypes = [F32, F32] + ([BF16, BF16] if with_bf16 else [])
    return pl.pallas_call(
        _shared_kv_kernel,
        grid=(rows // tm, 2),
        in_specs=[pl.BlockSpec((tm, d), lambda i, j: (i, 0)),
                  pl.BlockSpec((1, d), lambda i, j: (0, 0)),
                  pl.BlockSpec((d, MB_WIDTH), lambda i, j: (0, j))],
        out_specs=[spec] * len(dtypes),
        out_shape=[jax.ShapeDtypeStruct(shape, t) for t in dtypes],
        scratch_shapes=[pltpu.VMEM((tm, d), BF16)],
        compiler_params=_params("parallel", "arbitrary"),
        name=name,
    )(x, g.reshape(1, d), w)


def _out_proj_kernel(ap_ref, as_ref, bp_ref, bs_ref, wa_ref, wb_ref, xp_ref, xs_ref, *rest, prompt_tiles,
                     final_norm):
    if final_norm:
        g_ref, yp_ref, ys_ref = rest
    else:
        yp_ref, ys_ref = rest
    i = pl.program_id(0)

    def run(a_ref, b_ref, x_ref, o_ref):
        y = _dot(a_ref[...].astype(BF16), wa_ref[...]) + _dot(b_ref[...].astype(BF16), wb_ref[...])
        y = x_ref[...] + y
        if final_norm:
            ms = jnp.mean(y * y, axis=-1, keepdims=True)
            y = y * lax.rsqrt(ms + EPS) * g_ref[...]
        o_ref[...] = y

    @pl.when(i < prompt_tiles)
    def _():
        run(ap_ref, bp_ref, xp_ref, yp_ref)

    @pl.when(i >= prompt_tiles)
    def _():
        run(as_ref, bs_ref, xs_ref, ys_ref)


def _out_proj(main_p, main_s, mem_p, mem_s, w, x_p, x_s, g_final=None, *, name):
    tp, d = x_p.shape
    rows_s = x_s.shape[0]
    wm, wmem = main_p.shape[1], mem_p.shape[1]
    assert wm % wmem == 0
    tm = _row_tile(math.gcd(tp, rows_s), 256, 8)
    pt = tp // tm
    p_map = lambda i: (jnp.minimum(i, pt - 1), 0)
    s_map = lambda i: (jnp.maximum(i - pt, 0), 0)
    in_specs = [pl.BlockSpec((tm, wm), p_map), pl.BlockSpec((tm, wm), s_map),
                pl.BlockSpec((tm, wmem), p_map), pl.BlockSpec((tm, wmem), s_map),
                pl.BlockSpec((wm, d), lambda i: (0, 0)),
                pl.BlockSpec((wmem, d), lambda i: (wm // wmem, 0)),
                pl.BlockSpec((tm, d), p_map), pl.BlockSpec((tm, d), s_map)]
    args = [main_p, main_s, mem_p, mem_s, w, w, x_p, x_s]
    if g_final is not None:
        in_specs.append(pl.BlockSpec((1, d), lambda i: (0, 0)))
        args.append(g_final.reshape(1, d))
    return pl.pallas_call(
        functools.partial(_out_proj_kernel, prompt_tiles=pt, final_norm=g_final is not None),
        grid=((tp + rows_s) // tm,),
        in_specs=in_specs,
        out_specs=[pl.BlockSpec((tm, d), p_map), pl.BlockSpec((tm, d), s_map)],
        out_shape=[jax.ShapeDtypeStruct((tp, d), F32), jax.ShapeDtypeStruct((rows_s, d), F32)],
        compiler_params=_params("arbitrary"),
        name=name,
    )(*args)


def _mem_attn_heads(q_ref, gate_ref, o_ref, kv_of_head):
    scale = HEAD_DIM ** -0.5
    for h in range(MEM_HEADS):
        cols = slice(h * HEAD_DIM, (h + 1) * HEAD_DIM)
        k, v = kv_of_head(h)
        s = _dot_nt(q_ref[:, cols].astype(BF16), k.astype(BF16)) * scale
        e = jnp.exp(s - jnp.max(s, axis=-1, keepdims=True))
        o = _dot(e.astype(BF16), v.astype(BF16)) / jnp.sum(e, axis=-1, keepdims=True)
        o_ref[:, cols] = (o * _silu(gate_ref[:, cols])).astype(o_ref.dtype)


def _mem_attn_prompt_kernel(q_ref, gate_ref, mkv_ref, o_ref):
    def kv_of_head(h):
        return (mkv_ref[:, h * HEAD_DIM:(h + 1) * HEAD_DIM],
                mkv_ref[:, MEM_WIDTH + h * HEAD_DIM:MEM_WIDTH + (h + 1) * HEAD_DIM])

    _mem_attn_heads(q_ref, gate_ref, o_ref, kv_of_head)


def _mem_attn_sample_kernel(q_ref, gate_ref, mk_ref, mv_ref, o_ref):
    mem_len = mk_ref.shape[0] // MEM_HEADS

    def kv_of_head(h):
        rows = pl.ds(h, mem_len, stride=MEM_HEADS)
        return mk_ref[rows, :], mv_ref[rows, :]

    _mem_attn_heads(q_ref, gate_ref, o_ref, kv_of_head)


def _mem_attn_prompt(z, col_q, mkv, *, name):
    rows = z.shape[0]
    mem_len = mkv.shape[0]
    tm = _row_tile(rows, 512, 8)
    return pl.pallas_call(
        _mem_attn_prompt_kernel,
        grid=(rows // tm,),
        in_specs=[pl.BlockSpec((tm, MEM_WIDTH), lambda t: (t, col_q)),
                  pl.BlockSpec((tm, MEM_WIDTH), lambda t: (t, col_q + 1)),
                  pl.BlockSpec((mem_len, 2 * MEM_WIDTH), lambda t: (0, 0))],
        out_specs=pl.BlockSpec((tm, MEM_WIDTH), lambda t: (t, 0)),
        out_shape=jax.ShapeDtypeStruct((rows, MEM_WIDTH), BF16),
        compiler_params=_params("parallel"),
        name=name,
    )(z, z, mkv)


def _mem_attn_sample(z, col_q, cache_mk, cache_mv, layer, ts, *, name):
    _, nb, mem_len, heads, hd = cache_mk.shape
    assert heads == MEM_HEADS and hd == HEAD_DIM
    view = lambda c: c.reshape(c.shape[0], nb, mem_len * heads, hd)
    cache_spec = pl.BlockSpec((None, None, mem_len * heads, hd), lambda b: (layer, b, 0, 0))
    return pl.pallas_call(
        _mem_attn_sample_kernel,
        grid=(nb,),
        in_specs=[pl.BlockSpec((ts, MEM_WIDTH), lambda b: (b, col_q)),
                  pl.BlockSpec((ts, MEM_WIDTH), lambda b: (b, col_q + 1)),
                  cache_spec, cache_spec],
        out_specs=pl.BlockSpec((ts, MEM_WIDTH), lambda b: (b, 0)),
        out_shape=jax.ShapeDtypeStruct((nb * ts, MEM_WIDTH), F32),
        compiler_params=_params("parallel"),
        name=name,
    )(z, z, view(cache_mk), view(cache_mv))


def _cumsum_rows(x):
    n = x.shape[0]
    row = lax.broadcasted_iota(jnp.int32, x.shape, 0)
    s = 1
    while s < n:
        x = x + jnp.where(row >= s, pltpu.roll(x, s, 0), 0.0)
        s *= 2
    return x


def _lower_bound(lbp, layer):
    e = jnp.exp(lbp - jnp.max(lbp, axis=0, keepdims=True))
    return jnp.sum(e[:layer + 1], axis=0, keepdims=True) / jnp.sum(e, axis=0, keepdims=True)


def _hgrn_gates(hq, hf, lb):
    q = _silu(hq)
    f = lb + (1.0 - lb) * _sigmoid(hf)
    return q, 1.0 - f, jnp.log(f)


def _hgrn_finish(o, hg, go):
    ms = jnp.mean(o * o, axis=-1, keepdims=True)
    return o * lax.rsqrt(ms + EPS) * go * _silu(hg)


def _diag_tiles(q, k, b, width):
    sub = SUBLANES
    lane = lax.broadcasted_iota(jnp.int32, (sub, width), 1)
    trow = lax.broadcasted_iota(jnp.int32, (sub, width), 0)
    tiles = []
    for r0 in range(0, q.shape[0], sub):
        q8, k8, b8 = q[r0:r0 + sub], k[r0:r0 + sub], b[r0:r0 + sub]
        tile = jnp.zeros((sub, width), F32)
        for s in range(sub):
            e = jnp.exp(b8 - b8[s:s + 1])
            a_col = jnp.sum(q8 * e * k8[s:s + 1], axis=1, keepdims=True)
            tile = jnp.where(lane == r0 + s, a_col, tile)
        tiles.append(jnp.where(trow >= lane - r0, tile, 0.0))
    return jnp.concatenate(tiles, axis=0)


def _level_ref(b, level):
    parts = []
    for start in range(0, b.shape[0], 2 * level):
        r = start + level - 1
        parts.append(jnp.broadcast_to(b[r:r + 1, :], (2 * level, b.shape[1])))
    return jnp.concatenate(parts, axis=0)


def _level_masks(c):
    row = lax.broadcasted_iota(jnp.int32, (c, c), 0)
    col = lax.broadcasted_iota(jnp.int32, (c, c), 1)
    masks = []
    level = SUBLANES
    while level < c:
        sh = level.bit_length() - 1
        same = (row >> (sh + 1)) == (col >> (sh + 1))
        masks.append((level, same & (((row >> sh) & 1) == 1) & (((col >> sh) & 1) == 0)))
        level *= 2
    return masks


def _hgrn_chunk(q, k, g, v, st, masks):
    c = HG_CHUNK
    b = _cumsum_rows(g)
    a = _diag_tiles(q, k, b, c)
    for level, mask in masks:
        e = jnp.exp(-jnp.abs(b - _level_ref(b, level)))
        a = jnp.where(mask, _dot_nt((q * e).astype(BF16), (k * e).astype(BF16)), a)
    b_end = b[c - 1:c, :]
    vb = v.astype(BF16)
    o = _dot(a.astype(BF16), vb) + _dot_nt((q * jnp.exp(b)).astype(BF16), st.astype(BF16))
    st_new = st * jnp.exp(b_end) + _dot_tn(vb, (k * jnp.exp(b_end - b)).astype(BF16))
    return o, st_new


def _hgrn_prompt_kernel(hq_ref, hf_ref, hi_ref, hg_ref, lbp_ref, go_ref, mix_ref, sout_ref, st_ref, *, layer):
    n = pl.program_id(1)

    @pl.when(n == 0)
    def _():
        st_ref[...] = jnp.zeros_like(st_ref)

    lb = _lower_bound(lbp_ref[...], layer)
    go = go_ref[...]
    masks = _level_masks(HG_CHUNK)
    for c in range(HG_BLOCK_CHUNKS):
        rows = slice(c * HG_CHUNK, (c + 1) * HG_CHUNK)
        q, k, g = _hgrn_gates(hq_ref[rows, :], hf_ref[rows, :], lb)
        o, st_new = _hgrn_chunk(q, k, g, hi_ref[rows, :], st_ref[...], masks)
        st_ref[...] = st_new
        mix_ref[rows, :] = _hgrn_finish(o, hg_ref[rows, :], go).astype(mix_ref.dtype)

    @pl.when(n == pl.num_programs(1) - 1)
    def _():
        sout_ref[0] = st_ref[...].T


def _hgrn_prompt(z, hg_lb, g_o, tp, layer, *, name):
    rb = HG_CHUNK * HG_BLOCK_CHUNKS
    assert tp % rb == 0
    h_ = HG_HEADS
    n_lb = hg_lb.shape[0]
    zspec = lambda off: pl.BlockSpec((rb, HEAD_DIM), lambda h, n: (n, off + h))
    return pl.pallas_call(
        functools.partial(_hgrn_prompt_kernel, layer=layer),
        grid=(h_, tp // rb),
        in_specs=[zspec(0), zspec(h_), zspec(2 * h_), zspec(3 * h_),
                  pl.BlockSpec((n_lb, HEAD_DIM), lambda h, n: (0, h)),
                  pl.BlockSpec((1, HEAD_DIM), lambda h, n: (0, h))],
        out_specs=[pl.BlockSpec((rb, HEAD_DIM), lambda h, n: (n, h)),
                   pl.BlockSpec((1, HEAD_DIM, HEAD_DIM), lambda h, n: (h, 0, 0))],
        out_shape=[jax.ShapeDtypeStruct((tp, HG_WIDTH), BF16),
                   jax.ShapeDtypeStruct((h_, HEAD_DIM, HEAD_DIM), F32)],
        scratch_shapes=[pltpu.VMEM((HEAD_DIM, HEAD_DIM), F32)],
        compiler_params=_params("parallel", "arbitrary"),
        name=name,
    )(z, z, z, z, hg_lb, g_o.reshape(1, HG_WIDTH))


def _hgrn_sample_kernel(hq_ref, hf_ref, hi_ref, hg_ref, lbp_ref, go_ref, s0_ref, mix_ref, sout_ref, *, layer, ts, nb):
    lb = _lower_bound(lbp_ref[...], layer)
    go = go_ref[...]

    def one_sequence(bi):
        rows = pl.ds(pl.multiple_of(bi * ts, ts), ts)
        q, k, g = _hgrn_gates(hq_ref[rows, :], hf_ref[rows, :], lb)
        b = _cumsum_rows(g)
        b_end = b[ts - 1:ts, :]
        st = s0_ref[bi, 0].T
        a = _diag_tiles(q, k, b, LANES)
        vb = _pad_rows(hi_ref[rows, :], LANES).astype(BF16)
        kh = _pad_rows(k * jnp.exp(b_end - b), LANES).astype(BF16)
        o = _dot(a.astype(BF16), vb) + _dot_nt((q * jnp.exp(b)).astype(BF16), st.astype(BF16))
        st_new = st * jnp.exp(b_end) + _dot_tn(vb, kh)
        sout_ref[bi, 0] = st_new.T
        mix_ref[rows, :] = _hgrn_finish(o, hg_ref[rows, :], go)

    def body(p, carry):
        for e in range(HG_SAMPLE_UNROLL):
            one_sequence(HG_SAMPLE_UNROLL * p + e)
        return carry

    lax.fori_loop(0, nb // HG_SAMPLE_UNROLL, body, 0)


def _hgrn_sample(z, hg_lb, g_o, s0, ts, layer, *, name):
    nb = s0.shape[0]
    rows = nb * ts
    assert rows == z.shape[0] and ts == SUBLANES and nb % HG_SAMPLE_UNROLL == 0
    h_ = HG_HEADS
    n_lb = hg_lb.shape[0]
    zspec = lambda off: pl.BlockSpec((rows, HEAD_DIM), lambda h: (0, off + h))
    return pl.pallas_call(
        functools.partial(_hgrn_sample_kernel, layer=layer, ts=ts, nb=nb),
        grid=(h_,),
        in_specs=[zspec(0), zspec(h_), zspec(2 * h_), zspec(3 * h_),
                  pl.BlockSpec((n_lb, HEAD_DIM), lambda h: (0, h)),
                  pl.BlockSpec((1, HEAD_DIM), lambda h: (0, h)),
                  pl.BlockSpec((nb, 1, HEAD_DIM, HEAD_DIM), lambda h: (0, h, 0, 0))],
        out_specs=[pl.BlockSpec((rows, HEAD_DIM), lambda h: (0, h)),
                   pl.BlockSpec((nb, 1, HEAD_DIM, HEAD_DIM), lambda h: (0, h, 0, 0))],
        out_shape=[jax.ShapeDtypeStruct((rows, HG_WIDTH), F32),
                   jax.ShapeDtypeStruct(s0.shape, F32)],
        compiler_params=_params("parallel"),
        name=name,
    )(z, z, z, z, hg_lb, g_o.reshape(1, HG_WIDTH), s0)


def _t5_bucket(dist):
    exact = N_BUCKETS // 2
    d = jnp.maximum(dist, exact).astype(F32)
    large = exact + (jnp.log(d / exact) / math.log(MAX_DISTANCE / exact) * (N_BUCKETS - exact)).astype(jnp.int32)
    return jnp.where(dist < exact, dist, jnp.minimum(large, N_BUCKETS - 1))


def _bias_table_kernel(rb_ref, bucket_ref, o_ref):
    hi, mid, lo = _split3(rb_ref[...])
    row = lax.broadcasted_iota(jnp.int32, (LANES, bucket_ref.shape[1]), 0)
    oh = jnp.where(row == bucket_ref[...], 1.0, 0.0).astype(BF16)
    o_ref[...] = _dot(hi, oh) + _dot(mid, oh) + _dot(lo, oh)


def _bias_table(rel_bias, dist):
    n = dist.shape[0]
    rb = jnp.zeros((16, LANES), F32).at[:MB_HEADS, :N_BUCKETS].set(rel_bias.T)
    bucket = _t5_bucket(jnp.maximum(dist, 0).astype(jnp.int32)).reshape(1, n)
    return pl.pallas_call(
        _bias_table_kernel,
        out_shape=jax.ShapeDtypeStruct((16, n), F32),
        name="bias_table",
    )(rb, bucket)


def _block_mean_kernel(k_ref, o_ref):
    for n in range(o_ref.shape[0]):
        rows = slice(n * MB_BLOCK, (n + 1) * MB_BLOCK)
        o_ref[n:n + 1, :] = jnp.sum(k_ref[rows, :], axis=0, keepdims=True) * (1.0 / MB_BLOCK)


def _block_mean(k, n_blocks):
    return pl.pallas_call(
        _block_mean_kernel,
        grid=(MB_HEADS,),
        in_specs=[pl.BlockSpec((None, n_blocks * MB_BLOCK, HEAD_DIM), lambda h: (h, 0, 0))],
        out_specs=pl.BlockSpec((None, n_blocks, HEAD_DIM), lambda h: (h, 0, 0)),
        out_shape=jax.ShapeDtypeStruct((MB_HEADS, n_blocks, HEAD_DIM), F32),
        compiler_params=_params("parallel"),
        name="block_mean",
    )(k)


def _select_topk(gate, axis, n_blocks):
    idx = lax.broadcasted_iota(jnp.int32, gate.shape, axis)
    sel = jnp.zeros(gate.shape, F32)
    for _ in range(min(MB_TOPK, n_blocks)):
        mx = jnp.max(gate, axis=axis, keepdims=True)
        first = jnp.min(jnp.where(gate == mx, idx, n_blocks), axis=axis, keepdims=True)
        pick = idx == first
        sel = jnp.where(pick & (mx > -jnp.inf), 1.0, sel)
        gate = jnp.where(pick, -jnp.inf, gate)
    return sel


MASK_BIG = 2.0 ** 17
DUMMY_LANE = 125
FAR_LANES = (126, 127)
MOBA_GROUP_LOG2 = 3
MOBA_GROUP = 1 << MOBA_GROUP_LOG2


def _moba_prompt_kernel(q_ref, gate_ref, k_ref, v_ref, km_ref, brow_ref, o_ref, bias_ref, s_ref, mx_ref, acc_ref,
                        *, n_blocks):
    i = pl.program_id(1)
    blk = MB_BLOCK
    scale = HEAD_DIM ** -0.5
    inv_scale = HEAD_DIM ** 0.5
    near = bias_ref.shape[0]
    filler = n_blocks + near

    @pl.when(i == 0)
    def _build_bias():
        rowi = lax.broadcasted_iota(jnp.int32, (blk, 2 * blk), 0)
        for d in range(near):
            x = jnp.broadcast_to(brow_ref[0, d:d + 1, :], (blk, 2 * blk))
            for bit in range(blk.bit_length() - 1):
                x = jnp.where(((rowi >> bit) & 1) == 1, pltpu.roll(x, 1 << bit, 1), x)
            bias_ref[d] = x[:, blk:] * inv_scale
        s_ref[filler] = jnp.full((blk, blk), -MASK_BIG, F32)

    q = q_ref[...]
    qh, ql = _split2(q)

    tq = lax.broadcasted_iota(jnp.int32, (blk, blk), 0)
    tk = lax.broadcasted_iota(jnp.int32, (blk, blk), 1)
    own = _dot_nt(qh, k_ref[pl.ds(pl.multiple_of(i * blk, blk), blk), :]) + bias_ref[0]
    own = jnp.where(tq >= tk, own, -MASK_BIG)
    s_ref[i] = own
    mx_ref[...] = own

    kmh, kml = _split2(_pad_rows(km_ref[...], -(-n_blocks // 16) * 16))
    gate_t = _dot_nt(kmh, qh) + _dot_nt(kml, qh) + _dot_nt(kmh, ql)
    blk_id = lax.broadcasted_iota(jnp.int32, gate_t.shape, 0)
    sel_t = _select_topk(jnp.where(blk_id < i, gate_t, -jnp.inf), 0, n_blocks)
    sel = _pad_rows(sel_t, LANES).T

    lane = lax.broadcasted_iota(jnp.int32, sel.shape, 1)
    qa = jnp.concatenate([qh, jnp.where(lane >= FAR_LANES[0], 1.0, sel - 1.0).astype(BF16)], axis=1)
    c_far = brow_ref[0, near - 1:near, 0:1] * inv_scale
    c_hi = c_far.astype(BF16).astype(F32)
    lane_r = lax.broadcasted_iota(jnp.int32, (1, LANES), 1)
    far_row = jnp.where(lane_r == FAR_LANES[0], c_hi, jnp.where(lane_r == FAR_LANES[1], c_far - c_hi, 0.0))

    def scores(j, big_lane, row_vals):
        rows = pl.ds(pl.multiple_of(jnp.minimum(j, i) * blk, blk), blk)
        right = jnp.broadcast_to(jnp.where(lane_r == big_lane, MASK_BIG, row_vals), (blk, LANES)).astype(BF16)
        return _dot_nt(qa, jnp.concatenate([k_ref[rows, :], right], axis=1))

    first_near = jnp.maximum(i - (near - 1), 0)

    def far_group(p, carry):
        mx = mx_ref[...]
        for e in range(MOBA_GROUP):
            j = MOBA_GROUP * p + e
            is_far = j < first_near
            raw = scores(j, jnp.where(is_far, j, DUMMY_LANE), far_row)
            s_ref[jnp.where(is_far, j, n_blocks)] = raw
            mx = jnp.maximum(mx, raw)
        mx_ref[...] = mx
        return carry

    lax.fori_loop(0, (first_near + MOBA_GROUP - 1) >> MOBA_GROUP_LOG2, far_group, 0)

    mx = mx_ref[...]
    for dlt in range(1, near):
        j = i - dlt
        jc = jnp.maximum(j, 0)
        raw = scores(jc, jnp.where(j >= 0, jc, DUMMY_LANE), 0.0) + bias_ref[dlt]
        s_ref[jnp.where(j >= 0, jc, n_blocks + dlt)] = raw
        mx = jnp.maximum(mx, raw)
    mx_ref[...] = jnp.broadcast_to(jnp.max(mx, axis=1, keepdims=True), (blk, blk))

    ones = jnp.ones((blk, LANES), BF16)
    exp2_scale = scale * math.log2(math.e)

    def pv_part(first):
        probs, vals = [], []
        for e in range(MOBA_GROUP // 2):
            j = first + e
            tile = s_ref[jnp.where(j <= i, j, filler)]
            probs.append(jnp.exp2((tile - mx_ref[...]) * exp2_scale).astype(BF16))
            rows = pl.ds(pl.multiple_of(jnp.minimum(j, i) * blk, blk), blk)
            vals.append(jnp.concatenate([v_ref[rows, :], ones], axis=1))
        return _dot(jnp.concatenate(probs, axis=1), jnp.concatenate(vals, axis=0))

    def pv_group(p, carry):
        first = MOBA_GROUP * p
        acc_ref[...] = acc_ref[...] + (pv_part(first) + pv_part(first + MOBA_GROUP // 2))
        return carry

    acc_ref[...] = jnp.zeros_like(acc_ref)
    lax.fori_loop(0, (i + MOBA_GROUP) >> MOBA_GROUP_LOG2, pv_group, 0)
    acc = acc_ref[...]
    o_ref[...] = (acc[:, :HEAD_DIM] / acc[:, HEAD_DIM:] * _silu(gate_ref[...])).astype(o_ref.dtype)


def _moba_prompt(z, kb, vb, kmean, brows, tp, *, name):
    nq = tp // MB_BLOCK
    h_ = MB_HEADS
    assert nq <= DUMMY_LANE
    return pl.pallas_call(
        functools.partial(_moba_prompt_kernel, n_blocks=nq),
        grid=(h_, nq),
        in_specs=[pl.BlockSpec((MB_BLOCK, HEAD_DIM), lambda h, i: (i, h)),
                  pl.BlockSpec((MB_BLOCK, HEAD_DIM), lambda h, i: (i, h_ + h)),
                  pl.BlockSpec((None, tp, HEAD_DIM), lambda h, i: (h, 0, 0)),
                  pl.BlockSpec((None, tp, HEAD_DIM), lambda h, i: (h, 0, 0)),
                  pl.BlockSpec((None, nq, HEAD_DIM), lambda h, i: (h, 0, 0)),
                  pl.BlockSpec((1, NEAR_BLOCKS, 2 * MB_BLOCK), lambda h, i: (h, 0, 0))],
        out_specs=pl.BlockSpec((MB_BLOCK, HEAD_DIM), lambda h, i: (i, h)),
        out_shape=jax.ShapeDtypeStruct((tp, MB_WIDTH), BF16),
        scratch_shapes=[pltpu.VMEM((NEAR_BLOCKS, MB_BLOCK, MB_BLOCK), F32),
                        pltpu.VMEM((nq + NEAR_BLOCKS + 1, MB_BLOCK, MB_BLOCK), F32),
                        pltpu.VMEM((MB_BLOCK, MB_BLOCK), F32),
                        pltpu.VMEM((MB_BLOCK, MB_BLOCK), F32)],
        compiler_params=_params("parallel", "arbitrary"),
        name=name,
    )(z, z, kb, vb, kmean, brows)


def _cat_heads(ref, *lead):
    return jnp.concatenate([ref[lead + (h,)] for h in range(MB_HEADS)], axis=1)


def _sample_scores_kernel(pt_ref, q_ref, knew_ref, *rest, n_pages, pps, ts):
    del pt_ref
    kc = rest[:pps]
    bs_ref, cfar_ref, p_ref, l_ref, wq_ref, wql_ref, s_ref, km_ref, sel_ref = rest[pps:]
    g = pl.program_id(1)
    n_steps = n_pages // pps
    n_blocks = n_pages * PAGE_SIZE // MB_BLOCK
    near_pages = min(NEAR_PAGES, n_pages)
    scale = HEAD_DIM ** -0.5
    pg = PAGE_SIZE
    ppb = MB_BLOCK // pg

    @pl.when(g == 0)
    def _start_sequence():
        rep = _pad_rows(jnp.concatenate([q_ref[...]] * MB_HEADS, axis=0), LANES)
        r_h = lax.broadcasted_iota(jnp.int32, rep.shape, 0) // ts
        c_h = lax.broadcasted_iota(jnp.int32, rep.shape, 1) // HEAD_DIM
        hi, lo = _split2(jnp.where(r_h == c_h, rep, 0.0))
        wq_ref[...] = hi
        wql_ref[...] = lo

    pages = [_cat_heads(kc[u], 0) for u in range(pps)]
    st = _dot_nt(jnp.concatenate([kp.astype(BF16) for kp in pages], axis=0), wq_ref[...])
    for u in range(pps):
        p = g * pps + u
        near_idx = jnp.maximum(p - (n_pages - near_pages), 0)
        b_near = bs_ref[pl.ds(pl.multiple_of(near_idx * pg, pg), pg), :]
        bias = jnp.where(p >= n_pages - near_pages, b_near, cfar_ref[...])
        s_ref[pl.ds(pl.multiple_of(p * pg, pg), pg), :] = st[u * pg:(u + 1) * pg] * scale + bias
    for n in range(pps // ppb):
        ksum = sum(jnp.sum(pages[n * ppb + u], axis=0, keepdims=True) for u in range(ppb))
        km_ref[pl.ds(g * (pps // ppb) + n, 1), :] = ksum * (1.0 / MB_BLOCK)

    @pl.when(g == n_steps - 1)
    def _softmax():
        st_new = _dot_nt(_pad_rows(_cat_heads(knew_ref), pg).astype(BF16), wq_ref[...])
        krow = lax.broadcasted_iota(jnp.int32, (pg, LANES), 0)
        qcol = lax.broadcasted_iota(jnp.int32, (pg, LANES), 1)
        valid = (krow < ts) & (krow <= (qcol & (ts - 1)))
        s_cur = jnp.where(valid, st_new * scale + bs_ref[pl.ds(near_pages * pg, pg), :], -jnp.inf)

        kmh, kml = _split2(km_ref[...])
        gate = _dot_nt(kmh, wq_ref[...]) + _dot_nt(kmh, wql_ref[...]) + _dot_nt(kml, wq_ref[...])
        sel_ref[...] = _select_topk(gate, 0, n_blocks)

        def max_body(n, m):
            keep = sel_ref[pl.ds(n, 1), :] > 0.5
            for u in range(ppb):
                tile = s_ref[pl.ds(pl.multiple_of((n * ppb + u) * pg, pg), pg), :]
                m = jnp.maximum(m, jnp.where(keep, tile, -jnp.inf))
            return m

        m = lax.fori_loop(0, n_blocks, max_body, s_cur)
        mrow = jnp.max(m, axis=0, keepdims=True)

        def exp_body(n, l):
            keep = sel_ref[pl.ds(n, 1), :] > 0.5
            for u in range(ppb):
                rows = pl.ds(pl.multiple_of((n * ppb + u) * pg, pg), pg)
                e = jnp.exp(jnp.where(keep, s_ref[rows, :] - mrow, -jnp.inf))
                p_ref[0, rows, :] = e.astype(BF16)
                l = l + e
            return l

        e_cur = jnp.exp(s_cur - mrow)
        p_ref[0, pl.ds(n_pages * pg, pg), :] = e_cur.astype(BF16)
        l = lax.fori_loop(0, n_blocks, exp_body, e_cur)
        l_ref[0] = jnp.sum(l, axis=0, keepdims=True)


def _sample_values_kernel(pt_ref, gate_ref, vnew_ref, p_ref, pcur_ref, l_ref, *rest, n_pages, pps, ts):
    del pt_ref
    vc = rest[:pps]
    o_ref, acc_ref = rest[pps:]
    g = pl.program_id(1)
    pg = PAGE_SIZE

    @pl.when(g == 0)
    def _own_block():
        acc_ref[...] = _dot_tn(pcur_ref[0], _pad_rows(_cat_heads(vnew_ref), pg).astype(BF16))

    vals = jnp.concatenate([_cat_heads(vc[u], 0).astype(BF16) for u in range(pps)], axis=0)
    acc_ref[...] = acc_ref[...] + _dot_tn(p_ref[0], vals)

    @pl.when(g == n_pages // pps - 1)
    def _finish():
        r = lax.broadcasted_iota(jnp.int32, (LANES, LANES), 0)
        c = lax.broadcasted_iota(jnp.int32, (LANES, LANES), 1)
        lcol = jnp.sum(jnp.where(r == c, jnp.broadcast_to(l_ref[0], (LANES, LANES)), 0.0), axis=1, keepdims=True)
        for h in range(MB_HEADS):
            cols = slice(h * HEAD_DIM, (h + 1) * HEAD_DIM)
            o = acc_ref[h * ts:(h + 1) * ts, cols] / lcol[h * ts:(h + 1) * ts, :]
            o_ref[:, cols] = o * _silu(gate_ref[:, cols])


def _moba_sample(z, k_s, v_s, cache_k, cache_v, page_table, bs, cfar, ts, *, name):
    nb, n_pages = page_table.shape
    pps = math.gcd(n_pages, MAX_PAGES_PER_STEP)
    assert (pps * PAGE_SIZE) % MB_BLOCK == 0
    assert ts & (ts - 1) == 0 and MB_HEADS * ts <= LANES and z.shape[0] == nb * ts
    n_steps = n_pages // pps
    n_blocks = n_pages * PAGE_SIZE // MB_BLOCK
    n_keys = (n_pages + 1) * PAGE_SIZE
    kc = cache_k.transpose(0, 2, 1, 3)
    vc = cache_v.transpose(0, 2, 1, 3)
    row_spec = lambda col: pl.BlockSpec((ts, MB_WIDTH), lambda b, g, pt: (b, col))
    new_spec = pl.BlockSpec((MB_HEADS, ts, HEAD_DIM), lambda b, g, pt: (0, b, 0))
    page_spec = lambda u: pl.BlockSpec((1, MB_HEADS, PAGE_SIZE, HEAD_DIM), lambda b, g, pt: (pt[b, g * pps + u], 0, 0, 0))
    const2 = lambda b, g, pt: (0, 0)

    probs, denom = pl.pallas_call(
        functools.partial(_sample_scores_kernel, n_pages=n_pages, pps=pps, ts=ts),
        grid_spec=pltpu.PrefetchScalarGridSpec(
            num_scalar_prefetch=1,
            grid=(nb, n_steps),
            in_specs=[row_spec(0), new_spec] + [page_spec(u) for u in range(pps)]
                     + [pl.BlockSpec(bs.shape, const2), pl.BlockSpec(cfar.shape, const2)],
            out_specs=[pl.BlockSpec((1, n_keys, LANES), lambda b, g, pt: (b, 0, 0)),
                       pl.BlockSpec((1, 1, LANES), lambda b, g, pt: (b, 0, 0))],
            scratch_shapes=[pltpu.VMEM((LANES, MB_WIDTH), BF16),
                            pltpu.VMEM((LANES, MB_WIDTH), BF16),
                            pltpu.VMEM((n_pages * PAGE_SIZE, LANES), F32),
                            pltpu.VMEM((n_blocks, MB_WIDTH), F32),
                            pltpu.VMEM((n_blocks, LANES), F32)],
        ),
        out_shape=[jax.ShapeDtypeStruct((nb, n_keys, LANES), BF16),
                   jax.ShapeDtypeStruct((nb, 1, LANES), F32)],
        compiler_params=_params("arbitrary", "arbitrary"),
        name=name + "_scores",
    )(page_table, z, k_s, *([kc] * pps), bs, cfar)

    return pl.pallas_call(
        functools.partial(_sample_values_kernel, n_pages=n_pages, pps=pps, ts=ts),
        grid_spec=pltpu.PrefetchScalarGridSpec(
            num_scalar_prefetch=1,
            grid=(nb, n_steps),
            in_specs=[row_spec(1), new_spec,
                      pl.BlockSpec((1, pps * PAGE_SIZE, LANES), lambda b, g, pt: (b, g, 0)),
                      pl.BlockSpec((1, PAGE_SIZE, LANES), lambda b, g, pt: (b, n_pages, 0)),
                      pl.BlockSpec((1, 1, LANES), lambda b, g, pt: (b, 0, 0))]
                     + [page_spec(u) for u in range(pps)],
            out_specs=pl.BlockSpec((ts, MB_WIDTH), lambda b, g, pt: (b, 0)),
            scratch_shapes=[pltpu.VMEM((LANES, MB_WIDTH), F32)],
        ),
        out_shape=jax.ShapeDtypeStruct((nb * ts, MB_WIDTH), F32),
        compiler_params=_params("arbitrary", "arbitrary"),
        name=name + "_values",
    )(page_table, z, v_s, probs, probs, denom, *([vc] * pps))


def kernel(x_prompt, x_sample, cache_k, cache_v, cache_mem_k, cache_mem_v, state_hgrn, page_table, mem_prompt,
           g_norm, w_in_a, hg_lb, g_hg_out, w_out_a, w_in_b, w_out_b, g_kv, w_kv, rel_bias, g_mem, w_mem_kv,
           g_final):
    bp, tp, d = x_prompt.shape
    bs_, ts, _ = x_sample.shape
    assert bp == 1 and w_in_a.shape[0] == 1 and w_in_b.shape[0] == 1
    n_pages = page_table.shape[1]
    assert (n_pages * PAGE_SIZE) % MB_BLOCK == 0 and tp % MB_BLOCK == 0
    rows_s = bs_ * ts

    x0_p = x_prompt.reshape(tp, d)
    x0_s = x_sample.reshape(rows_s, d)
    bf = lambda w: w.astype(BF16)

    mem_kv = [_norm_matmul(mem_prompt.reshape(-1, d), g_mem[l], bf(w_mem_kv[l]), name=f"mem_kv_{l}")
              for l in range(2)]
    mem_len = mem_kv[0].shape[0]

    def in_proj(xp, xs, g, w, tag):
        w = bf(w)
        return (_norm_matmul(xp, g, w, name=f"in_proj_{tag}_prompt"), _norm_matmul(xs, g, w, name=f"in_proj_{tag}_sample"))

    def mem_attn(zp, zs, col_q, l, tag):
        return (_mem_attn_prompt(zp, col_q, mem_kv[l], name=f"mem_attn_{tag}_prompt"),
                _mem_attn_sample(zs, col_q, cache_mem_k, cache_mem_v, l, ts, name=f"mem_attn_{tag}_sample"))

    z_p, z_s = in_proj(x0_p, x0_s, g_norm[0], w_in_a[0], "a")
    mix_p, s_prompt = _hgrn_prompt(z_p, hg_lb, g_hg_out[0], tp, 0, name="hgrn_prompt")
    mix_s, s_sample = _hgrn_sample(z_s, hg_lb, g_hg_out[0], state_hgrn[0], ts, 0, name="hgrn_sample")
    mem_p, mem_s = mem_attn(z_p, z_s, 4 * HG_WIDTH // MEM_WIDTH, 0, "a")
    x1_p, x1_s = _out_proj(mix_p, mix_s, mem_p, mem_s, bf(w_out_a[0]), x0_p, x0_s, name="out_proj_a")

    k_p, v_p, kb_p, vb_p = _shared_kv(x1_p, g_kv, bf(w_kv), True, name="shared_kv_prompt")
    k_s, v_s = _shared_kv(x1_s, g_kv, bf(w_kv), False, name="shared_kv_sample")
    kmean = _block_mean(k_p, tp // MB_BLOCK)

    blk = MB_BLOCK
    c = jnp.arange(2 * blk, dtype=jnp.int32)
    dist_p = (jnp.arange(NEAR_BLOCKS, dtype=jnp.int32)[:, None] * blk + blk - c[None, :]).reshape(-1)
    brows = _bias_table(rel_bias, dist_p)[:MB_HEADS].reshape(MB_HEADS, NEAR_BLOCKS, 2 * blk)
    near_pages = min(NEAR_PAGES, n_pages)
    n_keys = (near_pages + 1) * PAGE_SIZE
    key_x = jnp.arange(n_keys, dtype=jnp.int32)
    dist_s = (near_pages * PAGE_SIZE - key_x[None, :] + jnp.arange(ts, dtype=jnp.int32)[:, None]).reshape(-1)
    bs_tab = _bias_table(rel_bias, dist_s)[:MB_HEADS].reshape(MB_HEADS, ts, n_keys)
    bs_tab = jnp.pad(bs_tab.transpose(2, 0, 1).reshape(n_keys, MB_HEADS * ts), ((0, 0), (0, LANES - MB_HEADS * ts)))
    cfar = jnp.pad(jnp.repeat(brows[:, NEAR_BLOCKS - 1, 0], ts), (0, LANES - MB_HEADS * ts)).reshape(1, LANES)

    zb_p, zb_s = in_proj(x1_p, x1_s, g_norm[1], w_in_b[0], "b")
    mix_p = _moba_prompt(zb_p, kb_p, vb_p, kmean, brows, tp, name="moba_prompt")
    mix_s = _moba_sample(zb_s, k_s, v_s, cache_k, cache_v, page_table, bs_tab, cfar, ts, name="moba_sample")
    mem_p, mem_s = mem_attn(zb_p, zb_s, 2 * MB_WIDTH // MEM_WIDTH, 1, "b")
    y_p, y_s = _out_proj(mix_p, mix_s, mem_p, mem_s, bf(w_out_b[0]), x1_p, x1_s, g_final, name="out_proj_b")

    heads = lambda a, b_, t: a.reshape(MB_HEADS, b_, t, HEAD_DIM).transpose(1, 2, 0, 3)
    memh = lambda lo: jnp.stack([kv[:, lo:lo + MEM_WIDTH] for kv in mem_kv]).reshape(2, bp, mem_len, MEM_HEADS, HEAD_DIM)
    return (y_p.reshape(bp, tp, d), y_s.reshape(bs_, ts, d),
            heads(k_p, bp, tp), heads(v_p, bp, tp), heads(k_s, bs_, ts), heads(v_s, bs_, ts),
            s_prompt[None, None].astype(state_hgrn.dtype), s_sample[None].astype(state_hgrn.dtype),
            memh(0), memh(MEM_WIDTH))
```

```python
import functools
import math

import jax
import jax.numpy as jnp
from jax import lax
from jax.experimental import pallas as pl
from jax.experimental.pallas import tpu as pltpu

F32 = jnp.float32
BF16 = jnp.bfloat16

HEAD_DIM = 128
HG_HEADS = 12
MB_HEADS = 12
MEM_HEADS = 4
MB_BLOCK = 256
MB_TOPK = 3
PAGE_SIZE = 128
N_BUCKETS = 32
MAX_DISTANCE = 1024
EPS = 1e-6
HG_WIDTH = HG_HEADS * HEAD_DIM
MB_WIDTH = MB_HEADS * HEAD_DIM
MEM_WIDTH = MEM_HEADS * HEAD_DIM

SUBLANES = 8
LANES = 128
HG_CHUNK = 128
HG_BLOCK_CHUNKS = 4
HG_SAMPLE_UNROLL = 2
NEAR_BLOCKS = 5
NEAR_PAGES = (NEAR_BLOCKS - 1) * MB_BLOCK // PAGE_SIZE
MAX_PAGES_PER_STEP = 16
VMEM_LIMIT = 48 * 1024 * 1024

NT_DIMS = (((1,), (1,)), ((), ()))
TN_DIMS = (((0,), (0,)), ((), ()))


def _dot(a, b):
    return jnp.dot(a, b, preferred_element_type=F32)


def _dot_nt(a, b):
    return lax.dot_general(a, b, NT_DIMS, preferred_element_type=F32)


def _dot_tn(a, b):
    return lax.dot_general(a, b, TN_DIMS, preferred_element_type=F32)


def _sigmoid(x):
    return 1.0 / (1.0 + jnp.exp(-x))


def _silu(x):
    return x * _sigmoid(x)


def _split2(x):
    hi = x.astype(BF16)
    lo = (x - hi.astype(F32)).astype(BF16)
    return hi, lo


def _split3(x):
    hi = x.astype(BF16)
    r = x - hi.astype(F32)
    mid = r.astype(BF16)
    lo = (r - mid.astype(F32)).astype(BF16)
    return hi, mid, lo


def _row_tile(m, cap, mult):
    best = None
    for t in range(mult, min(m, cap) + 1, mult):
        if m % t == 0:
            best = t
    assert best is not None, (m, cap, mult)
    return best


def _params(*sem):
    return pltpu.CompilerParams(dimension_semantics=sem, vmem_limit_bytes=VMEM_LIMIT)


def _pad_rows(x, rows):
    if rows == x.shape[0]:
        return x
    return jnp.concatenate([x, jnp.zeros((rows - x.shape[0], x.shape[1]), x.dtype)], axis=0)


def _normalize_rows(x_ref, g_ref, xn_ref):
    rows = x_ref.shape[0]
    step = 128 if rows % 128 == 0 else rows
    for r in range(0, rows, step):
        x = x_ref[r:r + step, :]
        ms = jnp.mean(x * x, axis=-1, keepdims=True)
        xn_ref[r:r + step, :] = (x * lax.rsqrt(ms + EPS) * g_ref[...]).astype(BF16)


def _norm_matmul_kernel(x_ref, g_ref, w_ref, o_ref, xn_ref):
    @pl.when(pl.program_id(1) == 0)
    def _():
        _normalize_rows(x_ref, g_ref, xn_ref)

    o_ref[...] = _dot(xn_ref[...], w_ref[...])


def _norm_matmul(x, g, w, *, name):
    m, d = x.shape
    n = w.shape[1]
    tm = _row_tile(m, 1024, 256)
    tn = _row_tile(n, 1024, 256)
    return pl.pallas_call(
        _norm_matmul_kernel,
        grid=(m // tm, n // tn),
        in_specs=[pl.BlockSpec((tm, d), lambda i, j: (i, 0)),
                  pl.BlockSpec((1, d), lambda i, j: (0, 0)),
                  pl.BlockSpec((d, tn), lambda i, j: (0, j))],
        out_specs=pl.BlockSpec((tm, tn), lambda i, j: (i, j)),
        out_shape=jax.ShapeDtypeStruct((m, n), F32),
        scratch_shapes=[pltpu.VMEM((tm, d), BF16)],
        compiler_params=_params("parallel", "arbitrary"),
        name=name,
    )(x, g.reshape(1, d), w)


def _shared_kv_kernel(x_ref, g_ref, w_ref, *rest):
    *o_refs, xn_ref = rest
    j = pl.program_id(1)

    @pl.when(j == 0)
    def _():
        _normalize_rows(x_ref, g_ref, xn_ref)

    y = _dot(xn_ref[...], w_ref[...])

    def write(refs):
        for h in range(MB_HEADS):
            for ref in refs:
                ref[h] = y[:, h * HEAD_DIM:(h + 1) * HEAD_DIM].astype(ref.dtype)

    @pl.when(j == 0)
    def _():
        write(o_refs[0::2])

    @pl.when(j == 1)
    def _():
        write(o_refs[1::2])


def _shared_kv(x, g, w, with_bf16, *, name):
    rows, d = x.shape
    tm = _row_tile(rows, 512, 256)
    assert w.shape[1] == 2 * MB_WIDTH
    shape = (MB_HEADS, rows, HEAD_DIM)
    spec = pl.BlockSpec((MB_HEADS, tm, HEAD_DIM), lambda i, j: (0, i, 0))
    dtypes = [F32, F32] + ([BF16, BF16] if with_bf16 else [])
    return pl.pallas_call(
        _shared_kv_kernel,
        grid=(rows // tm, 2),
        in_specs=[pl.BlockSpec((tm, d), lambda i, j: (i, 0)),
                  pl.BlockSpec((1, d), lambda i, j: (0, 0)),
                  pl.BlockSpec((d, MB_WIDTH), lambda i, j: (0, j))],
        out_specs=[spec] * len(dtypes),
        out_shape=[jax.ShapeDtypeStruct(shape, t) for t in dtypes],
        scratch_shapes=[pltpu.VMEM((tm, d), BF16)],
        compiler_params=_params("parallel", "arbitrary"),
        name=name,
    )(x, g.reshape(1, d), w)


def _out_proj_kernel(ap_ref, as_ref, bp_ref, bs_ref, wa_ref, wb_ref, xp_ref, xs_ref, *rest, prompt_tiles,
                     final_norm):
    if final_norm:
        g_ref, yp_ref, ys_ref = rest
    else:
        yp_ref, ys_ref = rest
    i = pl.program_id(0)

    def run(a_ref, b_ref, x_ref, o_ref):
        y = _dot(a_ref[...].astype(BF16), wa_ref[...]) + _dot(b_ref[...].astype(BF16), wb_ref[...])
        y = x_ref[...] + y
        if final_norm:
            ms = jnp.mean(y * y, axis=-1, keepdims=True)
            y = y * lax.rsqrt(ms + EPS) * g_ref[...]
        o_ref[...] = y

    @pl.when(i < prompt_tiles)
    def _():
        run(ap_ref, bp_ref, xp_ref, yp_ref)

    @pl.when(i >= prompt_tiles)
    def _():
        run(as_ref, bs_ref, xs_ref, ys_ref)


def _out_proj(main_p, main_s, mem_p, mem_s, w, x_p, x_s, g_final=None, *, name):
    tp, d = x_p.shape
    rows_s = x_s.shape[0]
    wm, wmem = main_p.shape[1], mem_p.shape[1]
    assert wm % wmem == 0
    tm = _row_tile(math.gcd(tp, rows_s), 256, 8)
    pt = tp // tm
    p_map = lambda i: (jnp.minimum(i, pt - 1), 0)
    s_map = lambda i: (jnp.maximum(i - pt, 0), 0)
    in_specs = [pl.BlockSpec((tm, wm), p_map), pl.BlockSpec((tm, wm), s_map),
                pl.BlockSpec((tm, wmem), p_map), pl.BlockSpec((tm, wmem), s_map),
                pl.BlockSpec((wm, d), lambda i: (0, 0)),
                pl.BlockSpec((wmem, d), lambda i: (wm // wmem, 0)),
                pl.BlockSpec((tm, d), p_map), pl.BlockSpec((tm, d), s_map)]
    args = [main_p, main_s, mem_p, mem_s, w, w, x_p, x_s]
    if g_final is not None:
        in_specs.append(pl.BlockSpec((1, d), lambda i: (0, 0)))
        args.append(g_final.reshape(1, d))
    return pl.pallas_call(
        functools.partial(_out_proj_kernel, prompt_tiles=pt, final_norm=g_final is not None),
        grid=((tp + rows_s) // tm,),
        in_specs=in_specs,
        out_specs=[pl.BlockSpec((tm, d), p_map), pl.BlockSpec((tm, d), s_map)],
        out_shape=[jax.ShapeDtypeStruct((tp, d), F32), jax.ShapeDtypeStruct((rows_s, d), F32)],
        compiler_params=_params("arbitrary"),
        name=name,
    )(*args)


def _mem_attn_heads(q_ref, gate_ref, o_ref, kv_of_head):
    scale = HEAD_DIM ** -0.5
    for h in range(MEM_HEADS):
        cols = slice(h * HEAD_DIM, (h + 1) * HEAD_DIM)
        k, v = kv_of_head(h)
        s = _dot_nt(q_ref[:, cols].astype(BF16), k.astype(BF16)) * scale
        e = jnp.exp(s - jnp.max(s, axis=-1, keepdims=True))
        o = _dot(e.astype(BF16), v.astype(BF16)) / jnp.sum(e, axis=-1, keepdims=True)
        o_ref[:, cols] = (o * _silu(gate_ref[:, cols])).astype(o_ref.dtype)


def _mem_attn_prompt_kernel(q_ref, gate_ref, mkv_ref, o_ref):
    def kv_of_head(h):
        return (mkv_ref[:, h * HEAD_DIM:(h + 1) * HEAD_DIM],
                mkv_ref[:, MEM_WIDTH + h * HEAD_DIM:MEM_WIDTH + (h + 1) * HEAD_DIM])

    _mem_attn_heads(q_ref, gate_ref, o_ref, kv_of_head)


def _mem_attn_sample_kernel(q_ref, gate_ref, mk_ref, mv_ref, o_ref):
    mem_len = mk_ref.shape[0] // MEM_HEADS

    def kv_of_head(h):
        rows = pl.ds(h, mem_len, stride=MEM_HEADS)
        return mk_ref[rows, :], mv_ref[rows, :]

    _mem_attn_heads(q_ref, gate_ref, o_ref, kv_of_head)


def _mem_attn_prompt(z, col_q, mkv, *, name):
    rows = z.shape[0]
    mem_len = mkv.shape[0]
    tm = _row_tile(rows, 512, 8)
    return pl.pallas_call(
        _mem_attn_prompt_kernel,
        grid=(rows // tm,),
        in_specs=[pl.BlockSpec((tm, MEM_WIDTH), lambda t: (t, col_q)),
                  pl.BlockSpec((tm, MEM_WIDTH), lambda t: (t, col_q + 1)),
                  pl.BlockSpec((mem_len, 2 * MEM_WIDTH), lambda t: (0, 0))],
        out_specs=pl.BlockSpec((tm, MEM_WIDTH), lambda t: (t, 0)),
        out_shape=jax.ShapeDtypeStruct((rows, MEM_WIDTH), BF16),
        compiler_params=_params("parallel"),
        name=name,
    )(z, z, mkv)


def _mem_attn_sample(z, col_q, cache_mk, cache_mv, layer, ts, *, name):
    _, nb, mem_len, heads, hd = cache_mk.shape
    assert heads == MEM_HEADS and hd == HEAD_DIM
    view = lambda c: c.reshape(c.shape[0], nb, mem_len * heads, hd)
    cache_spec = pl.BlockSpec((None, None, mem_len * heads, hd), lambda b: (layer, b, 0, 0))
    return pl.pallas_call(
        _mem_attn_sample_kernel,
        grid=(nb,),
        in_specs=[pl.BlockSpec((ts, MEM_WIDTH), lambda b: (b, col_q)),
                  pl.BlockSpec((ts, MEM_WIDTH), lambda b: (b, col_q + 1)),
                  cache_spec, cache_spec],
        out_specs=pl.BlockSpec((ts, MEM_WIDTH), lambda b: (b, 0)),
        out_shape=jax.ShapeDtypeStruct((nb * ts, MEM_WIDTH), F32),
        compiler_params=_params("parallel"),
        name=name,
    )(z, z, view(cache_mk), view(cache_mv))


def _cumsum_rows(x):
    n = x.shape[0]
    row = lax.broadcasted_iota(jnp.int32, x.shape, 0)
    s = 1
    while s < n:
        x = x + jnp.where(row >= s, pltpu.roll(x, s, 0), 0.0)
        s *= 2
    return x


def _lower_bound(lbp, layer):
    e = jnp.exp(lbp - jnp.max(lbp, axis=0, keepdims=True))
    return jnp.sum(e[:layer + 1], axis=0, keepdims=True) / jnp.sum(e, axis=0, keepdims=True)


def _hgrn_gates(hq, hf, lb):
    q = _silu(hq)
    f = lb + (1.0 - lb) * _sigmoid(hf)
    return q, 1.0 - f, jnp.log(f)


def _hgrn_finish(o, hg, go):
    ms = jnp.mean(o * o, axis=-1, keepdims=True)
    return o * lax.rsqrt(ms + EPS) * go * _silu(hg)


def _diag_tiles(q, k, b, width):
    sub = SUBLANES
    lane = lax.broadcasted_iota(jnp.int32, (sub, width), 1)
    trow = lax.broadcasted_iota(jnp.int32, (sub, width), 0)
    tiles = []
    for r0 in range(0, q.shape[0], sub):
        q8, k8, b8 = q[r0:r0 + sub], k[r0:r0 + sub], b[r0:r0 + sub]
        tile = jnp.zeros((sub, width), F32)
        for s in range(sub):
            e = jnp.exp(b8 - b8[s:s + 1])
            a_col = jnp.sum(q8 * e * k8[s:s + 1], axis=1, keepdims=True)
            tile = jnp.where(lane == r0 + s, a_col, tile)
        tiles.append(jnp.where(trow >= lane - r0, tile, 0.0))
    return jnp.concatenate(tiles, axis=0)


def _level_ref(b, level):
    parts = []
    for start in range(0, b.shape[0], 2 * level):
        r = start + level - 1
        parts.append(jnp.broadcast_to(b[r:r + 1, :], (2 * level, b.shape[1])))
    return jnp.concatenate(parts, axis=0)


def _level_masks(c):
    row = lax.broadcasted_iota(jnp.int32, (c, c), 0)
    col = lax.broadcasted_iota(jnp.int32, (c, c), 1)
    masks = []
    level = SUBLANES
    while level < c:
        sh = level.bit_length() - 1
        same = (row >> (sh + 1)) == (col >> (sh + 1))
        masks.append((level, same & (((row >> sh) & 1) == 1) & (((col >> sh) & 1) == 0)))
        level *= 2
    return masks


def _hgrn_chunk(q, k, g, v, st, masks):
    c = HG_CHUNK
    b = _cumsum_rows(g)
    a = _diag_tiles(q, k, b, c)
    for level, mask in masks:
        e = jnp.exp(-jnp.abs(b - _level_ref(b, level)))
        a = jnp.where(mask, _dot_nt((q * e).astype(BF16), (k * e).astype(BF16)), a)
    b_end = b[c - 1:c, :]
    vb = v.astype(BF16)
    o = _dot(a.astype(BF16), vb) + _dot_nt((q * jnp.exp(b)).astype(BF16), st.astype(BF16))
    st_new = st * jnp.exp(b_end) + _dot_tn(vb, (k * jnp.exp(b_end - b)).astype(BF16))
    return o, st_new


def _hgrn_prompt_kernel(hq_ref, hf_ref, hi_ref, hg_ref, lbp_ref, go_ref, mix_ref, sout_ref, st_ref, *, layer):
    n = pl.program_id(1)

    @pl.when(n == 0)
    def _():
        st_ref[...] = jnp.zeros_like(st_ref)

    lb = _lower_bound(lbp_ref[...], layer)
    go = go_ref[...]
    masks = _level_masks(HG_CHUNK)
    for c in range(HG_BLOCK_CHUNKS):
        rows = slice(c * HG_CHUNK, (c + 1) * HG_CHUNK)
        q, k, g = _hgrn_gates(hq_ref[rows, :], hf_ref[rows, :], lb)
        o, st_new = _hgrn_chunk(q, k, g, hi_ref[rows, :], st_ref[...], masks)
        st_ref[...] = st_new
        mix_ref[rows, :] = _hgrn_finish(o, hg_ref[rows, :], go).astype(mix_ref.dtype)

    @pl.when(n == pl.num_programs(1) - 1)
    def _():
        sout_ref[0] = st_ref[...].T


def _hgrn_prompt(z, hg_lb, g_o, tp, layer, *, name):
    rb = HG_CHUNK * HG_BLOCK_CHUNKS
    assert tp % rb == 0
    h_ = HG_HEADS
    n_lb = hg_lb.shape[0]
    zspec = lambda off: pl.BlockSpec((rb, HEAD_DIM), lambda h, n: (n, off + h))
    return pl.pallas_call(
        functools.partial(_hgrn_prompt_kernel, layer=layer),
        grid=(h_, tp // rb),
        in_specs=[zspec(0), zspec(h_), zspec(2 * h_), zspec(3 * h_),
                  pl.BlockSpec((n_lb, HEAD_DIM), lambda h, n: (0, h)),
                  pl.BlockSpec((1, HEAD_DIM), lambda h, n: (0, h))],
        out_specs=[pl.BlockSpec((rb, HEAD_DIM), lambda h, n: (n, h)),
                   pl.BlockSpec((1, HEAD_DIM, HEAD_DIM), lambda h, n: (h, 0, 0))],
        out_shape=[jax.ShapeDtypeStruct((tp, HG_WIDTH), BF16),
                   jax.ShapeDtypeStruct((h_, HEAD_DIM, HEAD_DIM), F32)],
        scratch_shapes=[pltpu.VMEM((HEAD_DIM, HEAD_DIM), F32)],
        compiler_params=_params("parallel", "arbitrary"),
        name=name,
    )(z, z, z, z, hg_lb, g_o.reshape(1, HG_WIDTH))


def _hgrn_sample_kernel(hq_ref, hf_ref, hi_ref, hg_ref, lbp_ref, go_ref, s0_ref, mix_ref, sout_ref, *, layer, ts, nb):
    lb = _lower_bound(lbp_ref[...], layer)
    go = go_ref[...]

    def one_sequence(bi):
        rows = pl.ds(pl.multiple_of(bi * ts, ts), ts)
        q, k, g = _hgrn_gates(hq_ref[rows, :], hf_ref[rows, :], lb)
        b = _cumsum_rows(g)
        b_end = b[ts - 1:ts, :]
        st = s0_ref[bi, 0].T
        a = _diag_tiles(q, k, b, LANES)
        vb = _pad_rows(hi_ref[rows, :], LANES).astype(BF16)
        kh = _pad_rows(k * jnp.exp(b_end - b), LANES).astype(BF16)
        o = _dot(a.astype(BF16), vb) + _dot_nt((q * jnp.exp(b)).astype(BF16), st.astype(BF16))
        st_new = st * jnp.exp(b_end) + _dot_tn(vb, kh)
        sout_ref[bi, 0] = st_new.T
        mix_ref[rows, :] = _hgrn_finish(o, hg_ref[rows, :], go)

    def body(p, carry):
        for e in range(HG_SAMPLE_UNROLL):
            one_sequence(HG_SAMPLE_UNROLL * p + e)
        return carry

    lax.fori_loop(0, nb // HG_SAMPLE_UNROLL, body, 0)


def _hgrn_sample(z, hg_lb, g_o, s0, ts, layer, *, name):
    nb = s0.shape[0]
    rows = nb * ts
    assert rows == z.shape[0] and ts == SUBLANES and nb % HG_SAMPLE_UNROLL == 0
    h_ = HG_HEADS
    n_lb = hg_lb.shape[0]
    zspec = lambda off: pl.BlockSpec((rows, HEAD_DIM), lambda h: (0, off + h))
    return pl.pallas_call(
        functools.partial(_hgrn_sample_kernel, layer=layer, ts=ts, nb=nb),
        grid=(h_,),
        in_specs=[zspec(0), zspec(h_), zspec(2 * h_), zspec(3 * h_),
                  pl.BlockSpec((n_lb, HEAD_DIM), lambda h: (0, h)),
                  pl.BlockSpec((1, HEAD_DIM), lambda h: (0, h)),
                  pl.BlockSpec((nb, 1, HEAD_DIM, HEAD_DIM), lambda h: (0, h, 0, 0))],
        out_specs=[pl.BlockSpec((rows, HEAD_DIM), lambda h: (0, h)),
                   pl.BlockSpec((nb, 1, HEAD_DIM, HEAD_DIM), lambda h: (0, h, 0, 0))],
        out_shape=[jax.ShapeDtypeStruct((rows, HG_WIDTH), F32),
                   jax.ShapeDtypeStruct(s0.shape, F32)],
        compiler_params=_params("parallel"),
        name=name,
    )(z, z, z, z, hg_lb, g_o.reshape(1, HG_WIDTH), s0)


def _t5_bucket(dist):
    exact = N_BUCKETS // 2
    d = jnp.maximum(dist, exact).astype(F32)
    large = exact + (jnp.log(d / exact) / math.log(MAX_DISTANCE / exact) * (N_BUCKETS - exact)).astype(jnp.int32)
    return jnp.where(dist < exact, dist, jnp.minimum(large, N_BUCKETS - 1))


def _bias_table_kernel(rb_ref, bucket_ref, o_ref):
    hi, mid, lo = _split3(rb_ref[...])
    row = lax.broadcasted_iota(jnp.int32, (LANES, bucket_ref.shape[1]), 0)
    oh = jnp.where(row == bucket_ref[...], 1.0, 0.0).astype(BF16)
    o_ref[...] = _dot(hi, oh) + _dot(mid, oh) + _dot(lo, oh)


def _bias_table(rel_bias, dist):
    n = dist.shape[0]
    rb = jnp.zeros((16, LANES), F32).at[:MB_HEADS, :N_BUCKETS].set(rel_bias.T)
    bucket = _t5_bucket(jnp.maximum(dist, 0).astype(jnp.int32)).reshape(1, n)
    return pl.pallas_call(
        _bias_table_kernel,
        out_shape=jax.ShapeDtypeStruct((16, n), F32),
        name="bias_table",
    )(rb, bucket)


def _block_mean_kernel(k_ref, o_ref):
    for n in range(o_ref.shape[0]):
        rows = slice(n * MB_BLOCK, (n + 1) * MB_BLOCK)
        o_ref[n:n + 1, :] = jnp.sum(k_ref[rows, :], axis=0, keepdims=True) * (1.0 / MB_BLOCK)


def _block_mean(k, n_blocks):
    return pl.pallas_call(
        _block_mean_kernel,
        grid=(MB_HEADS,),
        in_specs=[pl.BlockSpec((None, n_blocks * MB_BLOCK, HEAD_DIM), lambda h: (h, 0, 0))],
        out_specs=pl.BlockSpec((None, n_blocks, HEAD_DIM), lambda h: (h, 0, 0)),
        out_shape=jax.ShapeDtypeStruct((MB_HEADS, n_blocks, HEAD_DIM), F32),
        compiler_params=_params("parallel"),
        name="block_mean",
    )(k)


def _select_topk(gate, axis, n_blocks):
    idx = lax.broadcasted_iota(jnp.int32, gate.shape, axis)
    sel = jnp.zeros(gate.shape, F32)
    for _ in range(min(MB_TOPK, n_blocks)):
        mx = jnp.max(gate, axis=axis, keepdims=True)
        first = jnp.min(jnp.where(gate == mx, idx, n_blocks), axis=axis, keepdims=True)
        pick = idx == first
        sel = jnp.where(pick & (mx > -jnp.inf), 1.0, sel)
        gate = jnp.where(pick, -jnp.inf, gate)
    return sel


MASK_BIG = 2.0 ** 17
DUMMY_LANE = 125
FAR_LANES = (126, 127)
MOBA_GROUP_LOG2 = 3
MOBA_GROUP = 1 << MOBA_GROUP_LOG2
MOBA_HEADS_PER_STEP = 2


def _moba_prompt_kernel(q_ref, gate_ref, k_ref, v_ref, km_ref, brow_ref, o_ref, bias_ref, s_ref, mx_ref, acc_ref,
                        *, n_blocks):
    i = pl.program_id(1)
    blk = MB_BLOCK
    near = NEAR_BLOCKS
    heads = range(k_ref.shape[0])
    inv_scale = HEAD_DIM ** 0.5
    exp2_scale = HEAD_DIM ** -0.5 * math.log2(math.e)
    filler = n_blocks + near

    @pl.when(i == 0)
    def _build_bias():
        rowi = lax.broadcasted_iota(jnp.int32, (blk, 2 * blk), 0)
        for hh in heads:
            for d in range(near):
                x = jnp.broadcast_to(brow_ref[hh, d:d + 1, :], (blk, 2 * blk))
                for bit in range(blk.bit_length() - 1):
                    x = jnp.where(((rowi >> bit) & 1) == 1, pltpu.roll(x, 1 << bit, 1), x)
                bias_ref[hh, d] = x[:, blk:] * inv_scale
            s_ref[hh, filler] = jnp.full((blk, blk), -MASK_BIG, F32)

    tq = lax.broadcasted_iota(jnp.int32, (blk, blk), 0)
    tk = lax.broadcasted_iota(jnp.int32, (blk, blk), 1)
    lane = lax.broadcasted_iota(jnp.int32, (blk, LANES), 1)
    lane_r = lax.broadcasted_iota(jnp.int32, (1, LANES), 1)
    fold = lambda t: jnp.maximum(t[:, :LANES], t[:, LANES:])
    qa, far_row = [], []
    for hh in heads:
        qh, ql = _split2(q_ref[:, hh * HEAD_DIM:(hh + 1) * HEAD_DIM])

        own = _dot_nt(qh, k_ref[hh, pl.ds(pl.multiple_of(i * blk, blk), blk), :]) + bias_ref[hh, 0]
        own = jnp.where(tq >= tk, own, -MASK_BIG)
        s_ref[hh, i] = own
        mx_ref[hh] = fold(own)

        kmh, kml = _split2(_pad_rows(km_ref[hh], -(-n_blocks // 16) * 16))
        gate_t = _dot_nt(kmh, qh) + _dot_nt(kml, qh) + _dot_nt(kmh, ql)
        blk_id = lax.broadcasted_iota(jnp.int32, gate_t.shape, 0)
        sel_t = _select_topk(jnp.where(blk_id < i, gate_t, -jnp.inf), 0, n_blocks)
        sel = _pad_rows(sel_t, LANES).T

        qa.append(jnp.concatenate([qh, jnp.where(lane >= FAR_LANES[0], 1.0, sel - 1.0).astype(BF16)], axis=1))
        c_far = brow_ref[hh, near - 1:near, 0:1] * inv_scale
        c_hi = c_far.astype(BF16).astype(F32)
        far_row.append(jnp.where(lane_r == FAR_LANES[0], c_hi, jnp.where(lane_r == FAR_LANES[1], c_far - c_hi, 0.0)))

    def scores(hh, j, big_lane, row_vals):
        rows = pl.ds(pl.multiple_of(jnp.minimum(j, i) * blk, blk), blk)
        right = jnp.broadcast_to(jnp.where(lane_r == big_lane, MASK_BIG, row_vals), (blk, LANES)).astype(BF16)
        return _dot_nt(qa[hh], jnp.concatenate([k_ref[hh, rows, :], right], axis=1))

    first_near = jnp.maximum(i - (near - 1), 0)

    def far_group(p, carry):
        mx = [mx_ref[hh] for hh in heads]
        for e in range(MOBA_GROUP):
            j = MOBA_GROUP * p + e
            is_far = j < first_near
            for hh in heads:
                raw = scores(hh, j, jnp.where(is_far, j, DUMMY_LANE), far_row[hh])
                s_ref[hh, jnp.where(is_far, j, n_blocks)] = raw
                mx[hh] = jnp.maximum(mx[hh], fold(raw))
        for hh in heads:
            mx_ref[hh] = mx[hh]
        return carry

    lax.fori_loop(0, (first_near + MOBA_GROUP - 1) >> MOBA_GROUP_LOG2, far_group, 0)

    mx = [mx_ref[hh] for hh in heads]
    for dlt in range(1, near):
        j = i - dlt
        jc = jnp.maximum(j, 0)
        for hh in heads:
            raw = scores(hh, jc, jnp.where(j >= 0, jc, DUMMY_LANE), 0.0) + bias_ref[hh, dlt]
            s_ref[hh, jnp.where(j >= 0, jc, n_blocks + dlt)] = raw
            mx[hh] = jnp.maximum(mx[hh], fold(raw))
    for hh in heads:
        mx_ref[hh] = jnp.broadcast_to(jnp.max(mx[hh], axis=1, keepdims=True), (blk, LANES))

    ones = jnp.ones((blk, LANES), BF16)

    def pv_part(hh, first):
        top = jnp.concatenate([mx_ref[hh]] * 2, axis=1)
        probs, vals = [], []
        for e in range(MOBA_GROUP // 2):
            j = first + e
            tile = s_ref[hh, jnp.where(j <= i, j, filler)]
            probs.append(jnp.exp2((tile - top) * exp2_scale).astype(BF16))
            rows = pl.ds(pl.multiple_of(jnp.minimum(j, i) * blk, blk), blk)
            vals.append(jnp.concatenate([v_ref[hh, rows, :], ones], axis=1))
        return _dot(jnp.concatenate(probs, axis=1), jnp.concatenate(vals, axis=0))

    def pv_group(p, carry):
        first = MOBA_GROUP * p
        for hh in heads:
            acc_ref[hh] = acc_ref[hh] + (pv_part(hh, first) + pv_part(hh, first + MOBA_GROUP // 2))
        return carry

    acc_ref[...] = jnp.zeros_like(acc_ref)
    lax.fori_loop(0, (i + MOBA_GROUP) >> MOBA_GROUP_LOG2, pv_group, 0)
    for hh in heads:
        cols = slice(hh * HEAD_DIM, (hh + 1) * HEAD_DIM)
        acc = acc_ref[hh]
        o_ref[:, cols] = (acc[:, :HEAD_DIM] / acc[:, HEAD_DIM:] * _silu(gate_ref[:, cols])).astype(o_ref.dtype)


def _moba_prompt(z, kb, vb, kmean, brows, tp, *, name):
    nq = tp // MB_BLOCK
    hp = MOBA_HEADS_PER_STEP
    steps_h = MB_HEADS // hp
    assert nq <= DUMMY_LANE and MB_HEADS % hp == 0
    head_spec = lambda rows: pl.BlockSpec((hp, rows, HEAD_DIM), lambda h, i: (h, 0, 0))
    return pl.pallas_call(
        functools.partial(_moba_prompt_kernel, n_blocks=nq),
        grid=(steps_h, nq),
        in_specs=[pl.BlockSpec((MB_BLOCK, hp * HEAD_DIM), lambda h, i: (i, h)),
                  pl.BlockSpec((MB_BLOCK, hp * HEAD_DIM), lambda h, i: (i, steps_h + h)),
                  head_spec(tp), head_spec(tp), head_spec(nq),
                  pl.BlockSpec((hp, NEAR_BLOCKS, 2 * MB_BLOCK), lambda h, i: (h, 0, 0))],
        out_specs=pl.BlockSpec((MB_BLOCK, hp * HEAD_DIM), lambda h, i: (i, h)),
        out_shape=jax.ShapeDtypeStruct((tp, MB_WIDTH), BF16),
        scratch_shapes=[pltpu.VMEM((hp, NEAR_BLOCKS, MB_BLOCK, MB_BLOCK), F32),
                        pltpu.VMEM((hp, nq + NEAR_BLOCKS + 1, MB_BLOCK, MB_BLOCK), F32),
                        pltpu.VMEM((hp, MB_BLOCK, LANES), F32),
                        pltpu.VMEM((hp, MB_BLOCK, MB_BLOCK), F32)],
        compiler_params=_params("parallel", "arbitrary"),
        name=name,
    )(z, z, kb, vb, kmean, brows)


def _cat_heads(ref, *lead):
    return jnp.concatenate([ref[lead + (h,)] for h in range(MB_HEADS)], axis=1)


def _sample_scores_kernel(pt_ref, q_ref, knew_ref, *rest, n_pages, pps, ts):
    del pt_ref
    kc = rest[:pps]
    bs_ref, cfar_ref, p_ref, l_ref, wq_ref, wql_ref, s_ref, km_ref, sel_ref = rest[pps:]
    g = pl.program_id(1)
    n_steps = n_pages // pps
    n_blocks = n_pages * PAGE_SIZE // MB_BLOCK
    near_pages = min(NEAR_PAGES, n_pages)
    scale = HEAD_DIM ** -0.5
    pg = PAGE_SIZE
    ppb = MB_BLOCK // pg

    @pl.when(g == 0)
    def _start_sequence():
        rep = _pad_rows(jnp.concatenate([q_ref[...]] * MB_HEADS, axis=0), LANES)
        r_h = lax.broadcasted_iota(jnp.int32, rep.shape, 0) // ts
        c_h = lax.broadcasted_iota(jnp.int32, rep.shape, 1) // HEAD_DIM
        hi, lo = _split2(jnp.where(r_h == c_h, rep, 0.0))
        wq_ref[...] = hi
        wql_ref[...] = lo

    pages = [_cat_heads(kc[u], 0) for u in range(pps)]
    st = _dot_nt(jnp.concatenate([kp.astype(BF16) for kp in pages], axis=0), wq_ref[...])
    for u in range(pps):
        p = g * pps + u
        near_idx = jnp.maximum(p - (n_pages - near_pages), 0)
        b_near = bs_ref[pl.ds(pl.multiple_of(near_idx * pg, pg), pg), :]
        bias = jnp.where(p >= n_pages - near_pages, b_near, cfar_ref[...])
        s_ref[pl.ds(pl.multiple_of(p * pg, pg), pg), :] = st[u * pg:(u + 1) * pg] * scale + bias
    for n in range(pps // ppb):
        ksum = sum(jnp.sum(pages[n * ppb + u], axis=0, keepdims=True) for u in range(ppb))
        km_ref[pl.ds(g * (pps // ppb) + n, 1), :] = ksum * (1.0 / MB_BLOCK)

    @pl.when(g == n_steps - 1)
    def _softmax():
        st_new = _dot_nt(_pad_rows(_cat_heads(knew_ref), pg).astype(BF16), wq_ref[...])
        krow = lax.broadcasted_iota(jnp.int32, (pg, LANES), 0)
        qcol = lax.broadcasted_iota(jnp.int32, (pg, LANES), 1)
        valid = (krow < ts) & (krow <= (qcol & (ts - 1)))
        s_cur = jnp.where(valid, st_new * scale + bs_ref[pl.ds(near_pages * pg, pg), :], -jnp.inf)

        kmh, kml = _split2(km_ref[...])
        gate = _dot_nt(kmh, wq_ref[...]) + _dot_nt(kmh, wql_ref[...]) + _dot_nt(kml, wq_ref[...])
        sel_ref[...] = _select_topk(gate, 0, n_blocks)

        def max_body(n, m):
            keep = sel_ref[pl.ds(n, 1), :] > 0.5
            for u in range(ppb):
                tile = s_ref[pl.ds(pl.multiple_of((n * ppb + u) * pg, pg), pg), :]
                m = jnp.maximum(m, jnp.where(keep, tile, -jnp.inf))
            return m

        m = lax.fori_loop(0, n_blocks, max_body, s_cur)
        mrow = jnp.max(m, axis=0, keepdims=True)

        def exp_body(n, l):
            keep = sel_ref[pl.ds(n, 1), :] > 0.5
            for u in range(ppb):
                rows = pl.ds(pl.multiple_of((n * ppb + u) * pg, pg), pg)
                e = jnp.exp(jnp.where(keep, s_ref[rows, :] - mrow, -jnp.inf))
                p_ref[0, rows, :] = e.astype(BF16)
                l = l + e
            return l

        e_cur = jnp.exp(s_cur - mrow)
        p_ref[0, pl.ds(n_pages * pg, pg), :] = e_cur.astype(BF16)
        l = lax.fori_loop(0, n_blocks, exp_body, e_cur)
        l_ref[0] = jnp.sum(l, axis=0, keepdims=True)


def _sample_values_kernel(pt_ref, gate_ref, vnew_ref, p_ref, pcur_ref, l_ref, *rest, n_pages, pps, ts):
    del pt_ref
    vc = rest[:pps]
    o_ref, acc_ref = rest[pps:]
    g = pl.program_id(1)
    pg = PAGE_SIZE

    @pl.when(g == 0)
    def _own_block():
        acc_ref[...] = _dot_tn(pcur_ref[0], _pad_rows(_cat_heads(vnew_ref), pg).astype(BF16))

    vals = jnp.concatenate([_cat_heads(vc[u], 0).astype(BF16) for u in range(pps)], axis=0)
    acc_ref[...] = acc_ref[...] + _dot_tn(p_ref[0], vals)

    @pl.when(g == n_pages // pps - 1)
    def _finish():
        r = lax.broadcasted_iota(jnp.int32, (LANES, LANES), 0)
        c = lax.broadcasted_iota(jnp.int32, (LANES, LANES), 1)
        lcol = jnp.sum(jnp.where(r == c, jnp.broadcast_to(l_ref[0], (LANES, LANES)), 0.0), axis=1, keepdims=True)
        for h in range(MB_HEADS):
            cols = slice(h * HEAD_DIM, (h + 1) * HEAD_DIM)
            o = acc_ref[h * ts:(h + 1) * ts, cols] / lcol[h * ts:(h + 1) * ts, :]
            o_ref[:, cols] = o * _silu(gate_ref[:, cols])


def _moba_sample(z, k_s, v_s, cache_k, cache_v, page_table, bs, cfar, ts, *, name):
    nb, n_pages = page_table.shape
    pps = math.gcd(n_pages, MAX_PAGES_PER_STEP)
    assert (pps * PAGE_SIZE) % MB_BLOCK == 0
    assert ts & (ts - 1) == 0 and MB_HEADS * ts <= LANES and z.shape[0] == nb * ts
    n_steps = n_pages // pps
    n_blocks = n_pages * PAGE_SIZE // MB_BLOCK
    n_keys = (n_pages + 1) * PAGE_SIZE
    kc = cache_k.transpose(0, 2, 1, 3)
    vc = cache_v.transpose(0, 2, 1, 3)
    row_spec = lambda col: pl.BlockSpec((ts, MB_WIDTH), lambda b, g, pt: (b, col))
    new_spec = pl.BlockSpec((MB_HEADS, ts, HEAD_DIM), lambda b, g, pt: (0, b, 0))
    page_spec = lambda u: pl.BlockSpec((1, MB_HEADS, PAGE_SIZE, HEAD_DIM), lambda b, g, pt: (pt[b, g * pps + u], 0, 0, 0))
    const2 = lambda b, g, pt: (0, 0)

    probs, denom = pl.pallas_call(
        functools.partial(_sample_scores_kernel, n_pages=n_pages, pps=pps, ts=ts),
        grid_spec=pltpu.PrefetchScalarGridSpec(
            num_scalar_prefetch=1,
            grid=(nb, n_steps),
            in_specs=[row_spec(0), new_spec] + [page_spec(u) for u in range(pps)]
                     + [pl.BlockSpec(bs.shape, const2), pl.BlockSpec(cfar.shape, const2)],
            out_specs=[pl.BlockSpec((1, n_keys, LANES), lambda b, g, pt: (b, 0, 0)),
                       pl.BlockSpec((1, 1, LANES), lambda b, g, pt: (b, 0, 0))],
            scratch_shapes=[pltpu.VMEM((LANES, MB_WIDTH), BF16),
                            pltpu.VMEM((LANES, MB_WIDTH), BF16),
                            pltpu.VMEM((n_pages * PAGE_SIZE, LANES), F32),
                            pltpu.VMEM((n_blocks, MB_WIDTH), F32),
                            pltpu.VMEM((n_blocks, LANES), F32)],
        ),
        out_shape=[jax.ShapeDtypeStruct((nb, n_keys, LANES), BF16),
                   jax.ShapeDtypeStruct((nb, 1, LANES), F32)],
        compiler_params=_params("arbitrary", "arbitrary"),
        name=name + "_scores",
    )(page_table, z, k_s, *([kc] * pps), bs, cfar)

    return pl.pallas_call(
        functools.partial(_sample_values_kernel, n_pages=n_pages, pps=pps, ts=ts),
        grid_spec=pltpu.PrefetchScalarGridSpec(
            num_scalar_prefetch=1,
            grid=(nb, n_steps),
            in_specs=[row_spec(1), new_spec,
                      pl.BlockSpec((1, pps * PAGE_SIZE, LANES), lambda b, g, pt: (b, g, 0)),
                      pl.BlockSpec((1, PAGE_SIZE, LANES), lambda b, g, pt: (b, n_pages, 0)),
                      pl.BlockSpec((1, 1, LANES), lambda b, g, pt: (b, 0, 0))]
                     + [page_spec(u) for u in range(pps)],
            out_specs=pl.BlockSpec((ts, MB_WIDTH), lambda b, g, pt: (b, 0)),
            scratch_shapes=[pltpu.VMEM((LANES, MB_WIDTH), F32)],
        ),
        out_shape=jax.ShapeDtypeStruct((nb * ts, MB_WIDTH), F32),
        compiler_params=_params("arbitrary", "arbitrary"),
        name=name + "_values",
    )(page_table, z, v_s, probs, probs, denom, *([vc] * pps))


def kernel(x_prompt, x_sample, cache_k, cache_v, cache_mem_k, cache_mem_v, state_hgrn, page_table, mem_prompt,
           g_norm, w_in_a, hg_lb, g_hg_out, w_out_a, w_in_b, w_out_b, g_kv, w_kv, rel_bias, g_mem, w_mem_kv,
           g_final):
    bp, tp, d = x_prompt.shape
    bs_, ts, _ = x_sample.shape
    assert bp == 1 and w_in_a.shape[0] == 1 and w_in_b.shape[0] == 1
    n_pages = page_table.shape[1]
    assert (n_pages * PAGE_SIZE) % MB_BLOCK == 0 and tp % MB_BLOCK == 0
    rows_s = bs_ * ts

    x0_p = x_prompt.reshape(tp, d)
    x0_s = x_sample.reshape(rows_s, d)
    bf = lambda w: w.astype(BF16)

    mem_kv = [_norm_matmul(mem_prompt.reshape(-1, d), g_mem[l], bf(w_mem_kv[l]), name=f"mem_kv_{l}")
              for l in range(2)]
    mem_len = mem_kv[0].shape[0]

    def in_proj(xp, xs, g, w, tag):
        w = bf(w)
        return (_norm_matmul(xp, g, w, name=f"in_proj_{tag}_prompt"), _norm_matmul(xs, g, w, name=f"in_proj_{tag}_sample"))

    def mem_attn(zp, zs, col_q, l, tag):
        return (_mem_attn_prompt(zp, col_q, mem_kv[l], name=f"mem_attn_{tag}_prompt"),
                _mem_attn_sample(zs, col_q, cache_mem_k, cache_mem_v, l, ts, name=f"mem_attn_{tag}_sample"))

    z_p, z_s = in_proj(x0_p, x0_s, g_norm[0], w_in_a[0], "a")
    mix_p, s_prompt = _hgrn_prompt(z_p, hg_lb, g_hg_out[0], tp, 0, name="hgrn_prompt")
    mix_s, s_sample = _hgrn_sample(z_s, hg_lb, g_hg_out[0], state_hgrn[0], ts, 0, name="hgrn_sample")
    mem_p, mem_s = mem_attn(z_p, z_s, 4 * HG_WIDTH // MEM_WIDTH, 0, "a")
    x1_p, x1_s = _out_proj(mix_p, mix_s, mem_p, mem_s, bf(w_out_a[0]), x0_p, x0_s, name="out_proj_a")

    k_p, v_p, kb_p, vb_p = _shared_kv(x1_p, g_kv, bf(w_kv), True, name="shared_kv_prompt")
    k_s, v_s = _shared_kv(x1_s, g_kv, bf(w_kv), False, name="shared_kv_sample")
    kmean = _block_mean(k_p, tp // MB_BLOCK)

    blk = MB_BLOCK
    c = jnp.arange(2 * blk, dtype=jnp.int32)
    dist_p = (jnp.arange(NEAR_BLOCKS, dtype=jnp.int32)[:, None] * blk + blk - c[None, :]).reshape(-1)
    brows = _bias_table(rel_bias, dist_p)[:MB_HEADS].reshape(MB_HEADS, NEAR_BLOCKS, 2 * blk)
    near_pages = min(NEAR_PAGES, n_pages)
    n_keys = (near_pages + 1) * PAGE_SIZE
    key_x = jnp.arange(n_keys, dtype=jnp.int32)
    dist_s = (near_pages * PAGE_SIZE - key_x[None, :] + jnp.arange(ts, dtype=jnp.int32)[:, None]).reshape(-1)
    bs_tab = _bias_table(rel_bias, dist_s)[:MB_HEADS].reshape(MB_HEADS, ts, n_keys)
    bs_tab = jnp.pad(bs_tab.transpose(2, 0, 1).reshape(n_keys, MB_HEADS * ts), ((0, 0), (0, LANES - MB_HEADS * ts)))
    cfar = jnp.pad(jnp.repeat(brows[:, NEAR_BLOCKS - 1, 0], ts), (0, LANES - MB_HEADS * ts)).reshape(1, LANES)

    zb_p, zb_s = in_proj(x1_p, x1_s, g_norm[1], w_in_b[0], "b")
    mix_p = _moba_prompt(zb_p, kb_p, vb_p, kmean, brows, tp, name="moba_prompt")
    mix_s = _moba_sample(zb_s, k_s, v_s, cache_k, cache_v, page_table, bs_tab, cfar, ts, name="moba_sample")
    mem_p, mem_s = mem_attn(zb_p, zb_s, 2 * MB_WIDTH // MEM_WIDTH, 1, "b")
    y_p, y_s = _out_proj(mix_p, mix_s, mem_p, mem_s, bf(w_out_b[0]), x1_p, x1_s, g_final, name="out_proj_b")

    heads = lambda a, b_, t: a.reshape(MB_HEADS, b_, t, HEAD_DIM).transpose(1, 2, 0, 3)
    memh = lambda lo: jnp.stack([kv[:, lo:lo + MEM_WIDTH] for kv in mem_kv]).reshape(2, bp, mem_len, MEM_HEADS, HEAD_DIM)
    return (y_p.reshape(bp, tp, d), y_s.reshape(bs_, ts, d),
            heads(k_p, bp, tp), heads(v_p, bp, tp), heads(k_s, bs_, ts), heads(v_s, bs_, ts),
            s_prompt[None, None].astype(state_hgrn.dtype), s_sample[None].astype(state_hgrn.dtype),
            memh(0), memh(MEM_WIDTH))
```

```python
import functools
import math

import jax
import jax.numpy as jnp
from jax import lax
from jax.experimental import pallas as pl
from jax.experimental.pallas import tpu as pltpu

F32 = jnp.float32
BF16 = jnp.bfloat16

HEAD_DIM = 128
HG_HEADS = 12
MB_HEADS = 12
MEM_HEADS = 4
MB_BLOCK = 256
MB_TOPK = 3
PAGE_SIZE = 128
N_BUCKETS = 32
MAX_DISTANCE = 1024
EPS = 1e-6
HG_WIDTH = HG_HEADS * HEAD_DIM
MB_WIDTH = MB_HEADS * HEAD_DIM
MEM_WIDTH = MEM_HEADS * HEAD_DIM

SUBLANES = 8
LANES = 128
HG_CHUNK = 128
HG_BLOCK_CHUNKS = 4
HG_SAMPLE_UNROLL = 4
MEM_SAMPLE_SEQS = 2
NEAR_BLOCKS = 5
NEAR_PAGES = (NEAR_BLOCKS - 1) * MB_BLOCK // PAGE_SIZE
MAX_PAGES_PER_STEP = 16
VMEM_LIMIT = 48 * 1024 * 1024

NT_DIMS = (((1,), (1,)), ((), ()))
TN_DIMS = (((0,), (0,)), ((), ()))


def _dot(a, b):
    return jnp.dot(a, b, preferred_element_type=F32)


def _dot_nt(a, b):
    return lax.dot_general(a, b, NT_DIMS, preferred_element_type=F32)


def _dot_tn(a, b):
    return lax.dot_general(a, b, TN_DIMS, preferred_element_type=F32)


def _sigmoid(x):
    return 1.0 / (1.0 + jnp.exp(-x))


def _silu(x):
    return x * _sigmoid(x)


def _split2(x):
    hi = x.astype(BF16)
    lo = (x - hi.astype(F32)).astype(BF16)
    return hi, lo


def _split3(x):
    hi = x.astype(BF16)
    r = x - hi.astype(F32)
    mid = r.astype(BF16)
    lo = (r - mid.astype(F32)).astype(BF16)
    return hi, mid, lo


def _row_tile(m, cap, mult):
    best = None
    for t in range(mult, min(m, cap) + 1, mult):
        if m % t == 0:
            best = t
    assert best is not None, (m, cap, mult)
    return best


def _params(*sem):
    return pltpu.CompilerParams(dimension_semantics=sem, vmem_limit_bytes=VMEM_LIMIT)


def _pad_rows(x, rows):
    if rows == x.shape[0]:
        return x
    return jnp.concatenate([x, jnp.zeros((rows - x.shape[0], x.shape[1]), x.dtype)], axis=0)


def _normalize_rows(x_ref, g_ref, xn_ref):
    rows = x_ref.shape[0]
    step = 128 if rows % 128 == 0 else rows
    for r in range(0, rows, step):
        x = x_ref[r:r + step, :]
        ms = jnp.mean(x * x, axis=-1, keepdims=True)
        xn_ref[r:r + step, :] = (x * lax.rsqrt(ms + EPS) * g_ref[...]).astype(BF16)


def _norm_matmul_kernel(x_ref, g_ref, w_ref, o_ref, xn_ref):
    @pl.when(pl.program_id(1) == 0)
    def _():
        _normalize_rows(x_ref, g_ref, xn_ref)

    o_ref[...] = _dot(xn_ref[...], w_ref[...])


def _norm_matmul(x, g, w, *, name):
    m, d = x.shape
    n = w.shape[1]
    tm = _row_tile(m, 1024, 256)
    tn = _row_tile(n, 1024, 256)
    return pl.pallas_call(
        _norm_matmul_kernel,
        grid=(m // tm, n // tn),
        in_specs=[pl.BlockSpec((tm, d), lambda i, j: (i, 0)),
                  pl.BlockSpec((1, d), lambda i, j: (0, 0)),
                  pl.BlockSpec((d, tn), lambda i, j: (0, j))],
        out_specs=pl.BlockSpec((tm, tn), lambda i, j: (i, j)),
        out_shape=jax.ShapeDtypeStruct((m, n), F32),
        scratch_shapes=[pltpu.VMEM((tm, d), BF16)],
        compiler_params=_params("parallel", "arbitrary"),
        name=name,
    )(x, g.reshape(1, d), w)


def _shared_kv_kernel(x_ref, g_ref, w_ref, *rest):
    *o_refs, xn_ref = rest
    j = pl.program_id(1)

    @pl.when(j == 0)
    def _():
        _normalize_rows(x_ref, g_ref, xn_ref)

    y = _dot(xn_ref[...], w_ref[...])

    def write(refs):
        for h in range(MB_HEADS):
            for ref in refs:
                ref[h] = y[:, h * HEAD_DIM:(h + 1) * HEAD_DIM].astype(ref.dtype)

    @pl.when(j == 0)
    def _():
        write(o_refs[0::2])

    @pl.when(j == 1)
    def _():
        write(o_refs[1::2])


def _shared_kv(x, g, w, with_bf16, *, name):
    rows, d = x.shape
    tm = _row_tile(rows, 512, 256)
    assert w.shape[1] == 2 * MB_WIDTH
    shape = (MB_HEADS, rows, HEAD_DIM)
    spec = pl.BlockSpec((MB_HEADS, tm, HEAD_DIM), lambda i, j: (0, i, 0))
    dtypes = [F32, F32] + ([BF16, BF16] if with_bf16 else [])
    return pl.pallas_call(
        _shared_kv_kernel,
        grid=(rows // tm, 2),
        in_specs=[pl.BlockSpec((tm, d), lambda i, j: (i, 0)),
                  pl.BlockSpec((1, d), lambda i, j: (0, 0)),
                  pl.BlockSpec((d, MB_WIDTH), lambda i, j: (0, j))],
        out_specs=[spec] * len(dtypes),
        out_shape=[jax.ShapeDtypeStruct(shape, t) for t in dtypes],
        scratch_shapes=[pltpu.VMEM((tm, d), BF16)],
        compiler_params=_params("parallel", "arbitrary"),
        name=name,
    )(x, g.reshape(1, d), w)


def _out_proj_kernel(ap_ref, as_ref, bp_ref, bs_ref, wa_ref, wb_ref, xp_ref, xs_ref, *rest, prompt_tiles,
                     final_norm):
    if final_norm:
        g_ref, yp_ref, ys_ref = rest
    else:
        yp_ref, ys_ref = rest
    i = pl.program_id(0)

    def run(a_ref, b_ref, x_ref, o_ref):
        y = _dot(a_ref[...].astype(BF16), wa_ref[...]) + _dot(b_ref[...].astype(BF16), wb_ref[...])
        y = x_ref[...] + y
        if final_norm:
            ms = jnp.mean(y * y, axis=-1, keepdims=True)
            y = y * lax.rsqrt(ms + EPS) * g_ref[...]
        o_ref[...] = y

    @pl.when(i < prompt_tiles)
    def _():
        run(ap_ref, bp_ref, xp_ref, yp_ref)

    @pl.when(i >= prompt_tiles)
    def _():
        run(as_ref, bs_ref, xs_ref, ys_ref)


def _out_proj(main_p, main_s, mem_p, mem_s, w, x_p, x_s, g_final=None, *, name):
    tp, d = x_p.shape
    rows_s = x_s.shape[0]
    wm, wmem = main_p.shape[1], mem_p.shape[1]
    assert wm % wmem == 0
    tm = _row_tile(math.gcd(tp, rows_s), 256, 8)
    pt = tp // tm
    p_map = lambda i: (jnp.minimum(i, pt - 1), 0)
    s_map = lambda i: (jnp.maximum(i - pt, 0), 0)
    in_specs = [pl.BlockSpec((tm, wm), p_map), pl.BlockSpec((tm, wm), s_map),
                pl.BlockSpec((tm, wmem), p_map), pl.BlockSpec((tm, wmem), s_map),
                pl.BlockSpec((wm, d), lambda i: (0, 0)),
                pl.BlockSpec((wmem, d), lambda i: (wm // wmem, 0)),
                pl.BlockSpec((tm, d), p_map), pl.BlockSpec((tm, d), s_map)]
    args = [main_p, main_s, mem_p, mem_s, w, w, x_p, x_s]
    if g_final is not None:
        in_specs.append(pl.BlockSpec((1, d), lambda i: (0, 0)))
        args.append(g_final.reshape(1, d))
    return pl.pallas_call(
        functools.partial(_out_proj_kernel, prompt_tiles=pt, final_norm=g_final is not None),
        grid=((tp + rows_s) // tm,),
        in_specs=in_specs,
        out_specs=[pl.BlockSpec((tm, d), p_map), pl.BlockSpec((tm, d), s_map)],
        out_shape=[jax.ShapeDtypeStruct((tp, d), F32), jax.ShapeDtypeStruct((rows_s, d), F32)],
        compiler_params=_params("arbitrary"),
        name=name,
    )(*args)


def _mem_attn_heads(q_ref, gate_ref, o_ref, rows, kv_of_head):
    scale = HEAD_DIM ** -0.5
    for h in range(MEM_HEADS):
        cols = slice(h * HEAD_DIM, (h + 1) * HEAD_DIM)
        k, v = kv_of_head(h)
        s = _dot_nt(q_ref[rows, cols].astype(BF16), k.astype(BF16)) * scale
        e = jnp.exp(s - jnp.max(s, axis=-1, keepdims=True))
        o = _dot(e.astype(BF16), v.astype(BF16)) / jnp.sum(e, axis=-1, keepdims=True)
        o_ref[rows, cols] = (o * _silu(gate_ref[rows, cols])).astype(o_ref.dtype)


def _mem_attn_prompt_kernel(q_ref, gate_ref, mkv_ref, o_ref):
    def kv_of_head(h):
        return (mkv_ref[:, h * HEAD_DIM:(h + 1) * HEAD_DIM],
                mkv_ref[:, MEM_WIDTH + h * HEAD_DIM:MEM_WIDTH + (h + 1) * HEAD_DIM])

    _mem_attn_heads(q_ref, gate_ref, o_ref, slice(None), kv_of_head)


def _mem_attn_sample_kernel(q_ref, gate_ref, mk_ref, mv_ref, o_ref, *, ts):
    mem_len = mk_ref.shape[1] // MEM_HEADS
    for s in range(mk_ref.shape[0]):
        def kv_of_head(h, s=s):
            rows = pl.ds(h, mem_len, stride=MEM_HEADS)
            return mk_ref[s, rows, :], mv_ref[s, rows, :]

        _mem_attn_heads(q_ref, gate_ref, o_ref, slice(s * ts, (s + 1) * ts), kv_of_head)


def _mem_attn_prompt(z, col_q, mkv, *, name):
    rows = z.shape[0]
    mem_len = mkv.shape[0]
    tm = _row_tile(rows, 512, 8)
    return pl.pallas_call(
        _mem_attn_prompt_kernel,
        grid=(rows // tm,),
        in_specs=[pl.BlockSpec((tm, MEM_WIDTH), lambda t: (t, col_q)),
                  pl.BlockSpec((tm, MEM_WIDTH), lambda t: (t, col_q + 1)),
                  pl.BlockSpec((mem_len, 2 * MEM_WIDTH), lambda t: (0, 0))],
        out_specs=pl.BlockSpec((tm, MEM_WIDTH), lambda t: (t, 0)),
        out_shape=jax.ShapeDtypeStruct((rows, MEM_WIDTH), BF16),
        compiler_params=_params("parallel"),
        name=name,
    )(z, z, mkv)


def _mem_attn_sample(z, col_q, cache_mk, cache_mv, layer, ts, *, name):
    _, nb, mem_len, heads, hd = cache_mk.shape
    assert heads == MEM_HEADS and hd == HEAD_DIM
    view = lambda c: c.reshape(c.shape[0], nb, mem_len * heads, hd)
    ns = math.gcd(nb, MEM_SAMPLE_SEQS)
    cache_spec = pl.BlockSpec((None, ns, mem_len * heads, hd), lambda b: (layer, b, 0, 0))
    return pl.pallas_call(
        functools.partial(_mem_attn_sample_kernel, ts=ts),
        grid=(nb // ns,),
        in_specs=[pl.BlockSpec((ns * ts, MEM_WIDTH), lambda b: (b, col_q)),
                  pl.BlockSpec((ns * ts, MEM_WIDTH), lambda b: (b, col_q + 1)),
                  cache_spec, cache_spec],
        out_specs=pl.BlockSpec((ns * ts, MEM_WIDTH), lambda b: (b, 0)),
        out_shape=jax.ShapeDtypeStruct((nb * ts, MEM_WIDTH), F32),
        compiler_params=_params("parallel"),
        name=name,
    )(z, z, view(cache_mk), view(cache_mv))


def _cumsum_rows(x):
    n = x.shape[0]
    row = lax.broadcasted_iota(jnp.int32, x.shape, 0)
    s = 1
    while s < n:
        x = x + jnp.where(row >= s, pltpu.roll(x, s, 0), 0.0)
        s *= 2
    return x


def _lower_bound(lbp, layer):
    e = jnp.exp(lbp - jnp.max(lbp, axis=0, keepdims=True))
    return jnp.sum(e[:layer + 1], axis=0, keepdims=True) / jnp.sum(e, axis=0, keepdims=True)


def _hgrn_gates(hq, hf, lb):
    q = _silu(hq)
    f = lb + (1.0 - lb) * _sigmoid(hf)
    return q, 1.0 - f, jnp.log(f)


def _hgrn_finish(o, hg, go):
    ms = jnp.mean(o * o, axis=-1, keepdims=True)
    return o * lax.rsqrt(ms + EPS) * go * _silu(hg)


def _diag_tiles(q, k, b, width):
    sub = SUBLANES
    lane = lax.broadcasted_iota(jnp.int32, (sub, width), 1)
    trow = lax.broadcasted_iota(jnp.int32, (sub, width), 0)
    tiles = []
    for r0 in range(0, q.shape[0], sub):
        q8, k8, b8 = q[r0:r0 + sub], k[r0:r0 + sub], b[r0:r0 + sub]
        tile = jnp.zeros((sub, width), F32)
        for s in range(sub):
            e = jnp.exp(b8 - b8[s:s + 1])
            a_col = jnp.sum(q8 * e * k8[s:s + 1], axis=1, keepdims=True)
            tile = jnp.where(lane == r0 + s, a_col, tile)
        tiles.append(jnp.where(trow >= lane - r0, tile, 0.0))
    return jnp.concatenate(tiles, axis=0)


def _level_ref(b, level):
    parts = []
    for start in range(0, b.shape[0], 2 * level):
        r = start + level - 1
        parts.append(jnp.broadcast_to(b[r:r + 1, :], (2 * level, b.shape[1])))
    return jnp.concatenate(parts, axis=0)


def _level_masks(c):
    row = lax.broadcasted_iota(jnp.int32, (c, c), 0)
    col = lax.broadcasted_iota(jnp.int32, (c, c), 1)
    masks = []
    level = SUBLANES
    while level < c:
        sh = level.bit_length() - 1
        same = (row >> (sh + 1)) == (col >> (sh + 1))
        masks.append((level, same & (((row >> sh) & 1) == 1) & (((col >> sh) & 1) == 0)))
        level *= 2
    return masks


def _hgrn_chunk(q, k, g, v, st, masks):
    c = HG_CHUNK
    b = _cumsum_rows(g)
    a = _diag_tiles(q, k, b, c)
    for level, mask in masks:
        e = jnp.exp(-jnp.abs(b - _level_ref(b, level)))
        a = jnp.where(mask, _dot_nt((q * e).astype(BF16), (k * e).astype(BF16)), a)
    b_end = b[c - 1:c, :]
    vb = v.astype(BF16)
    o = _dot(a.astype(BF16), vb) + _dot_nt((q * jnp.exp(b)).astype(BF16), st.astype(BF16))
    st_new = st * jnp.exp(b_end) + _dot_tn(vb, (k * jnp.exp(b_end - b)).astype(BF16))
    return o, st_new


def _hgrn_prompt_kernel(hq_ref, hf_ref, hi_ref, hg_ref, lbp_ref, go_ref, mix_ref, sout_ref, st_ref, *, layer):
    n = pl.program_id(1)

    @pl.when(n == 0)
    def _():
        st_ref[...] = jnp.zeros_like(st_ref)

    lb = _lower_bound(lbp_ref[...], layer)
    go = go_ref[...]
    masks = _level_masks(HG_CHUNK)
    for c in range(HG_BLOCK_CHUNKS):
        rows = slice(c * HG_CHUNK, (c + 1) * HG_CHUNK)
        q, k, g = _hgrn_gates(hq_ref[rows, :], hf_ref[rows, :], lb)
        o, st_new = _hgrn_chunk(q, k, g, hi_ref[rows, :], st_ref[...], masks)
        st_ref[...] = st_new
        mix_ref[rows, :] = _hgrn_finish(o, hg_ref[rows, :], go).astype(mix_ref.dtype)

    @pl.when(n == pl.num_programs(1) - 1)
    def _():
        sout_ref[0] = st_ref[...].T


def _hgrn_prompt(z, hg_lb, g_o, tp, layer, *, name):
    rb = HG_CHUNK * HG_BLOCK_CHUNKS
    assert tp % rb == 0
    h_ = HG_HEADS
    n_lb = hg_lb.shape[0]
    zspec = lambda off: pl.BlockSpec((rb, HEAD_DIM), lambda h, n: (n, off + h))
    return pl.pallas_call(
        functools.partial(_hgrn_prompt_kernel, layer=layer),
        grid=(h_, tp // rb),
        in_specs=[zspec(0), zspec(h_), zspec(2 * h_), zspec(3 * h_),
                  pl.BlockSpec((n_lb, HEAD_DIM), lambda h, n: (0, h)),
                  pl.BlockSpec((1, HEAD_DIM), lambda h, n: (0, h))],
        out_specs=[pl.BlockSpec((rb, HEAD_DIM), lambda h, n: (n, h)),
                   pl.BlockSpec((1, HEAD_DIM, HEAD_DIM), lambda h, n: (h, 0, 0))],
        out_shape=[jax.ShapeDtypeStruct((tp, HG_WIDTH), BF16),
                   jax.ShapeDtypeStruct((h_, HEAD_DIM, HEAD_DIM), F32)],
        scratch_shapes=[pltpu.VMEM((HEAD_DIM, HEAD_DIM), F32)],
        compiler_params=_params("parallel", "arbitrary"),
        name=name,
    )(z, z, z, z, hg_lb, g_o.reshape(1, HG_WIDTH))


def _hgrn_sample_kernel(hq_ref, hf_ref, hi_ref, hg_ref, lbp_ref, go_ref, s0_ref, mix_ref, sout_ref, *, layer, ts, nb):
    lb = _lower_bound(lbp_ref[...], layer)
    go = go_ref[...]

    def one_sequence(bi):
        rows = pl.ds(pl.multiple_of(bi * ts, ts), ts)
        q, k, g = _hgrn_gates(hq_ref[rows, :], hf_ref[rows, :], lb)
        b = _cumsum_rows(g)
        b_end = b[ts - 1:ts, :]
        st = s0_ref[bi, 0].T
        a = _diag_tiles(q, k, b, LANES)
        vb = _pad_rows(hi_ref[rows, :], LANES).astype(BF16)
        kh = _pad_rows(k * jnp.exp(b_end - b), LANES).astype(BF16)
        o = _dot(a.astype(BF16), vb) + _dot_nt((q * jnp.exp(b)).astype(BF16), st.astype(BF16))
        st_new = st * jnp.exp(b_end) + _dot_tn(vb, kh)
        sout_ref[bi, 0] = st_new.T
        mix_ref[rows, :] = _hgrn_finish(o, hg_ref[rows, :], go)

    def body(p, carry):
        for e in range(HG_SAMPLE_UNROLL):
            one_sequence(HG_SAMPLE_UNROLL * p + e)
        return carry

    lax.fori_loop(0, nb // HG_SAMPLE_UNROLL, body, 0)


def _hgrn_sample(z, hg_lb, g_o, s0, ts, layer, *, name):
    nb = s0.shape[0]
    rows = nb * ts
    assert rows == z.shape[0] and ts == SUBLANES and nb % HG_SAMPLE_UNROLL == 0
    h_ = HG_HEADS
    n_lb = hg_lb.shape[0]
    zspec = lambda off: pl.BlockSpec((rows, HEAD_DIM), lambda h: (0, off + h))
    return pl.pallas_call(
        functools.partial(_hgrn_sample_kernel, layer=layer, ts=ts, nb=nb),
        grid=(h_,),
        in_specs=[zspec(0), zspec(h_), zspec(2 * h_), zspec(3 * h_),
                  pl.BlockSpec((n_lb, HEAD_DIM), lambda h: (0, h)),
                  pl.BlockSpec((1, HEAD_DIM), lambda h: (0, h)),
                  pl.BlockSpec((nb, 1, HEAD_DIM, HEAD_DIM), lambda h: (0, h, 0, 0))],
        out_specs=[pl.BlockSpec((rows, HEAD_DIM), lambda h: (0, h)),
                   pl.BlockSpec((nb, 1, HEAD_DIM, HEAD_DIM), lambda h: (0, h, 0, 0))],
        out_shape=[jax.ShapeDtypeStruct((rows, HG_WIDTH), F32),
                   jax.ShapeDtypeStruct(s0.shape, F32)],
        compiler_params=_params("parallel"),
        name=name,
    )(z, z, z, z, hg_lb, g_o.reshape(1, HG_WIDTH), s0)


def _t5_bucket(dist):
    exact = N_BUCKETS // 2
    d = jnp.maximum(dist, exact).astype(F32)
    large = exact + (jnp.log(d / exact) / math.log(MAX_DISTANCE / exact) * (N_BUCKETS - exact)).astype(jnp.int32)
    return jnp.where(dist < exact, dist, jnp.minimum(large, N_BUCKETS - 1))


def _bias_table_kernel(rb_ref, bucket_ref, o_ref):
    hi, mid, lo = _split3(rb_ref[...])
    row = lax.broadcasted_iota(jnp.int32, (LANES, bucket_ref.shape[1]), 0)
    oh = jnp.where(row == bucket_ref[...], 1.0, 0.0).astype(BF16)
    o_ref[...] = _dot(hi, oh) + _dot(mid, oh) + _dot(lo, oh)


def _bias_table(rel_bias, dist):
    n = dist.shape[0]
    rb = jnp.zeros((16, LANES), F32).at[:MB_HEADS, :N_BUCKETS].set(rel_bias.T)
    bucket = _t5_bucket(jnp.maximum(dist, 0).astype(jnp.int32)).reshape(1, n)
    return pl.pallas_call(
        _bias_table_kernel,
        out_shape=jax.ShapeDtypeStruct((16, n), F32),
        name="bias_table",
    )(rb, bucket)


def _block_mean_kernel(k_ref, o_ref):
    for n in range(o_ref.shape[0]):
        rows = slice(n * MB_BLOCK, (n + 1) * MB_BLOCK)
        o_ref[n:n + 1, :] = jnp.sum(k_ref[rows, :], axis=0, keepdims=True) * (1.0 / MB_BLOCK)


def _block_mean(k, n_blocks):
    return pl.pallas_call(
        _block_mean_kernel,
        grid=(MB_HEADS,),
        in_specs=[pl.BlockSpec((None, n_blocks * MB_BLOCK, HEAD_DIM), lambda h: (h, 0, 0))],
        out_specs=pl.BlockSpec((None, n_blocks, HEAD_DIM), lambda h: (h, 0, 0)),
        out_shape=jax.ShapeDtypeStruct((MB_HEADS, n_blocks, HEAD_DIM), F32),
        compiler_params=_params("parallel"),
        name="block_mean",
    )(k)


def _select_topk(gate, axis, n_blocks):
    idx = lax.broadcasted_iota(jnp.int32, gate.shape, axis)
    sel = jnp.zeros(gate.shape, F32)
    for _ in range(min(MB_TOPK, n_blocks)):
        mx = jnp.max(gate, axis=axis, keepdims=True)
        first = jnp.min(jnp.where(gate == mx, idx, n_blocks), axis=axis, keepdims=True)
        pick = idx == first
        sel = jnp.where(pick & (mx > -jnp.inf), 1.0, sel)
        gate = jnp.where(pick, -jnp.inf, gate)
    return sel


MASK_BIG = 2.0 ** 17
DUMMY_LANE = 125
FAR_LANES = (126, 127)
MOBA_GROUP_LOG2 = 3
MOBA_GROUP = 1 << MOBA_GROUP_LOG2
MOBA_HEADS_PER_STEP = 2


def _moba_prompt_kernel(q_ref, gate_ref, k_ref, v_ref, km_ref, brow_ref, o_ref, bias_ref, s_ref, mx_ref, acc_ref,
                        *, n_blocks):
    i = pl.program_id(1)
    blk = MB_BLOCK
    near = NEAR_BLOCKS
    heads = range(k_ref.shape[0])
    inv_scale = HEAD_DIM ** 0.5
    exp2_scale = HEAD_DIM ** -0.5 * math.log2(math.e)
    filler = n_blocks + near

    @pl.when(i == 0)
    def _build_bias():
        rowi = lax.broadcasted_iota(jnp.int32, (blk, 2 * blk), 0)
        for hh in heads:
            for d in range(near):
                x = jnp.broadcast_to(brow_ref[hh, d:d + 1, :], (blk, 2 * blk))
                for bit in range(blk.bit_length() - 1):
                    x = jnp.where(((rowi >> bit) & 1) == 1, pltpu.roll(x, 1 << bit, 1), x)
                bias_ref[hh, d] = x[:, blk:] * inv_scale
            s_ref[hh, filler] = jnp.full((blk, blk), -MASK_BIG, F32)

    tq = lax.broadcasted_iota(jnp.int32, (blk, blk), 0)
    tk = lax.broadcasted_iota(jnp.int32, (blk, blk), 1)
    lane = lax.broadcasted_iota(jnp.int32, (blk, LANES), 1)
    lane_r = lax.broadcasted_iota(jnp.int32, (1, LANES), 1)
    fold = lambda t: jnp.maximum(t[:, :LANES], t[:, LANES:])
    qa, far_row = [], []
    for hh in heads:
        qh, ql = _split2(q_ref[:, hh * HEAD_DIM:(hh + 1) * HEAD_DIM])

        own = _dot_nt(qh, k_ref[hh, pl.ds(pl.multiple_of(i * blk, blk), blk), :]) + bias_ref[hh, 0]
        own = jnp.where(tq >= tk, own, -MASK_BIG)
        s_ref[hh, i] = own
        mx_ref[hh] = fold(own)

        kmh, kml = _split2(_pad_rows(km_ref[hh], -(-n_blocks // 16) * 16))
        gate_t = _dot_nt(kmh, qh) + _dot_nt(kml, qh) + _dot_nt(kmh, ql)
        blk_id = lax.broadcasted_iota(jnp.int32, gate_t.shape, 0)
        sel_t = _select_topk(jnp.where(blk_id < i, gate_t, -jnp.inf), 0, n_blocks)
        sel = _pad_rows(sel_t, LANES).T

        qa.append(jnp.concatenate([qh, jnp.where(lane >= FAR_LANES[0], 1.0, sel - 1.0).astype(BF16)], axis=1))
        c_far = brow_ref[hh, near - 1:near, 0:1] * inv_scale
        c_hi = c_far.astype(BF16).astype(F32)
        far_row.append(jnp.where(lane_r == FAR_LANES[0], c_hi, jnp.where(lane_r == FAR_LANES[1], c_far - c_hi, 0.0)))

    def scores(hh, j, big_lane, row_vals):
        rows = pl.ds(pl.multiple_of(jnp.minimum(j, i) * blk, blk), blk)
        right = jnp.broadcast_to(jnp.where(lane_r == big_lane, MASK_BIG, row_vals), (blk, LANES)).astype(BF16)
        return _dot_nt(qa[hh], jnp.concatenate([k_ref[hh, rows, :], right], axis=1))

    first_near = jnp.maximum(i - (near - 1), 0)

    def grouped(count, blocks_fn):
        n_full = count >> MOBA_GROUP_LOG2
        rem = count & (MOBA_GROUP - 1)
        half = MOBA_GROUP // 2
        for group, lo, hi in ((MOBA_GROUP, 0, n_full + jnp.where(rem > half, 1, 0)),
                              (half, 2 * n_full, 2 * n_full + jnp.where((rem > 0) & (rem <= half), 1, 0))):
            def body(p, carry, group=group):
                blocks_fn(group * p, group)
                return carry

            lax.fori_loop(lo, hi, body, 0)

    def far_blocks(first, group):
        mx = [mx_ref[hh] for hh in heads]
        for e in range(group):
            j = first + e
            is_far = j < first_near
            for hh in heads:
                raw = scores(hh, j, jnp.where(is_far, j, DUMMY_LANE), far_row[hh])
                s_ref[hh, jnp.where(is_far, j, n_blocks)] = raw
                mx[hh] = jnp.maximum(mx[hh], fold(raw))
        for hh in heads:
            mx_ref[hh] = mx[hh]

    grouped(first_near, far_blocks)

    mx = [mx_ref[hh] for hh in heads]
    for dlt in range(1, near):
        j = i - dlt
        jc = jnp.maximum(j, 0)
        for hh in heads:
            raw = scores(hh, jc, jnp.where(j >= 0, jc, DUMMY_LANE), 0.0) + bias_ref[hh, dlt]
            s_ref[hh, jnp.where(j >= 0, jc, n_blocks + dlt)] = raw
            mx[hh] = jnp.maximum(mx[hh], fold(raw))
    for hh in heads:
        mx_ref[hh] = jnp.broadcast_to(jnp.max(mx[hh], axis=1, keepdims=True), (blk, LANES))

    ones = jnp.ones((blk, LANES), BF16)

    def pv_part(hh, first, count):
        top = jnp.concatenate([mx_ref[hh]] * 2, axis=1)
        probs, vals = [], []
        for e in range(count):
            j = first + e
            tile = s_ref[hh, jnp.where(j <= i, j, filler)]
            probs.append(jnp.exp2((tile - top) * exp2_scale).astype(BF16))
            rows = pl.ds(pl.multiple_of(jnp.minimum(j, i) * blk, blk), blk)
            vals.append(jnp.concatenate([v_ref[hh, rows, :], ones], axis=1))
        return _dot(jnp.concatenate(probs, axis=1), jnp.concatenate(vals, axis=0))

    def pv_blocks(first, group):
        half = group // 2
        for hh in heads:
            acc_ref[hh] = acc_ref[hh] + (pv_part(hh, first, half) + pv_part(hh, first + half, half))

    acc_ref[...] = jnp.zeros_like(acc_ref)
    grouped(i + 1, pv_blocks)
    for hh in heads:
        cols = slice(hh * HEAD_DIM, (hh + 1) * HEAD_DIM)
        acc = acc_ref[hh]
        o_ref[:, cols] = (acc[:, :HEAD_DIM] / acc[:, HEAD_DIM:] * _silu(gate_ref[:, cols])).astype(o_ref.dtype)


def _moba_prompt(z, kb, vb, kmean, brows, tp, *, name):
    nq = tp // MB_BLOCK
    hp = MOBA_HEADS_PER_STEP
    steps_h = MB_HEADS // hp
    assert nq <= DUMMY_LANE and MB_HEADS % hp == 0
    head_spec = lambda rows: pl.BlockSpec((hp, rows, HEAD_DIM), lambda h, i: (h, 0, 0))
    return pl.pallas_call(
        functools.partial(_moba_prompt_kernel, n_blocks=nq),
        grid=(steps_h, nq),
        in_specs=[pl.BlockSpec((MB_BLOCK, hp * HEAD_DIM), lambda h, i: (i, h)),
                  pl.BlockSpec((MB_BLOCK, hp * HEAD_DIM), lambda h, i: (i, steps_h + h)),
                  head_spec(tp), head_spec(tp), head_spec(nq),
                  pl.BlockSpec((hp, NEAR_BLOCKS, 2 * MB_BLOCK), lambda h, i: (h, 0, 0))],
        out_specs=pl.BlockSpec((MB_BLOCK, hp * HEAD_DIM), lambda h, i: (i, h)),
        out_shape=jax.ShapeDtypeStruct((tp, MB_WIDTH), BF16),
        scratch_shapes=[pltpu.VMEM((hp, NEAR_BLOCKS, MB_BLOCK, MB_BLOCK), F32),
                        pltpu.VMEM((hp, nq + NEAR_BLOCKS + 1, MB_BLOCK, MB_BLOCK), F32),
                        pltpu.VMEM((hp, MB_BLOCK, LANES), F32),
                        pltpu.VMEM((hp, MB_BLOCK, MB_BLOCK), F32)],
        compiler_params=_params("parallel", "arbitrary"),
        name=name,
    )(z, z, kb, vb, kmean, brows)


def _cat_heads(ref, *lead):
    return jnp.concatenate([ref[lead + (h,)] for h in range(MB_HEADS)], axis=1)


def _sample_scores_kernel(pt_ref, q_ref, knew_ref, *rest, n_pages, pps, ts):
    del pt_ref
    kc = rest[:pps]
    bs_ref, cfar_ref, p_ref, l_ref, wq_ref, wql_ref, s_ref, km_ref, sel_ref = rest[pps:]
    g = pl.program_id(1)
    n_steps = n_pages // pps
    n_blocks = n_pages * PAGE_SIZE // MB_BLOCK
    near_pages = min(NEAR_PAGES, n_pages)
    scale = HEAD_DIM ** -0.5
    pg = PAGE_SIZE
    ppb = MB_BLOCK // pg

    @pl.when(g == 0)
    def _start_sequence():
        rep = _pad_rows(jnp.concatenate([q_ref[...]] * MB_HEADS, axis=0), LANES)
        r_h = lax.broadcasted_iota(jnp.int32, rep.shape, 0) // ts
        c_h = lax.broadcasted_iota(jnp.int32, rep.shape, 1) // HEAD_DIM
        hi, lo = _split2(jnp.where(r_h == c_h, rep, 0.0))
        wq_ref[...] = hi
        wql_ref[...] = lo

    pages = [_cat_heads(kc[u], 0) for u in range(pps)]
    st = _dot_nt(jnp.concatenate([kp.astype(BF16) for kp in pages], axis=0), wq_ref[...])
    for u in range(pps):
        p = g * pps + u
        near_idx = jnp.maximum(p - (n_pages - near_pages), 0)
        b_near = bs_ref[pl.ds(pl.multiple_of(near_idx * pg, pg), pg), :]
        bias = jnp.where(p >= n_pages - near_pages, b_near, cfar_ref[...])
        s_ref[pl.ds(pl.multiple_of(p * pg, pg), pg), :] = st[u * pg:(u + 1) * pg] * scale + bias
    for n in range(pps // ppb):
        ksum = sum(jnp.sum(pages[n * ppb + u], axis=0, keepdims=True) for u in range(ppb))
        km_ref[pl.ds(g * (pps // ppb) + n, 1), :] = ksum * (1.0 / MB_BLOCK)

    @pl.when(g == n_steps - 1)
    def _softmax():
        st_new = _dot_nt(_pad_rows(_cat_heads(knew_ref), pg).astype(BF16), wq_ref[...])
        krow = lax.broadcasted_iota(jnp.int32, (pg, LANES), 0)
        qcol = lax.broadcasted_iota(jnp.int32, (pg, LANES), 1)
        valid = (krow < ts) & (krow <= (qcol & (ts - 1)))
        s_cur = jnp.where(valid, st_new * scale + bs_ref[pl.ds(near_pages * pg, pg), :], -jnp.inf)

        kmh, kml = _split2(km_ref[...])
        gate = _dot_nt(kmh, wq_ref[...]) + _dot_nt(kmh, wql_ref[...]) + _dot_nt(kml, wq_ref[...])
        sel_ref[...] = _select_topk(gate, 0, n_blocks)

        def max_body(n, m):
            keep = sel_ref[pl.ds(n, 1), :] > 0.5
            for u in range(ppb):
                tile = s_ref[pl.ds(pl.multiple_of((n * ppb + u) * pg, pg), pg), :]
                m = jnp.maximum(m, jnp.where(keep, tile, -jnp.inf))
            return m

        m = lax.fori_loop(0, n_blocks, max_body, s_cur)
        mrow = jnp.max(m, axis=0, keepdims=True)

        def exp_body(n, l):
            keep = sel_ref[pl.ds(n, 1), :] > 0.5
            for u in range(ppb):
                rows = pl.ds(pl.multiple_of((n * ppb + u) * pg, pg), pg)
                e = jnp.exp(jnp.where(keep, s_ref[rows, :] - mrow, -jnp.inf))
                p_ref[0, rows, :] = e.astype(BF16)
                l = l + e
            return l

        e_cur = jnp.exp(s_cur - mrow)
        p_ref[0, pl.ds(n_pages * pg, pg), :] = e_cur.astype(BF16)
        l = lax.fori_loop(0, n_blocks, exp_body, e_cur)
        l_ref[0] = jnp.sum(l, axis=0, keepdims=True)


def _sample_values_kernel(pt_ref, gate_ref, vnew_ref, p_ref, pcur_ref, l_ref, *rest, n_pages, pps, ts):
    del pt_ref
    vc = rest[:pps]
    o_ref, acc_ref = rest[pps:]
    g = pl.program_id(1)
    pg = PAGE_SIZE

    @pl.when(g == 0)
    def _own_block():
        acc_ref[...] = _dot_tn(pcur_ref[0], _pad_rows(_cat_heads(vnew_ref), pg).astype(BF16))

    vals = jnp.concatenate([_cat_heads(vc[u], 0).astype(BF16) for u in range(pps)], axis=0)
    acc_ref[...] = acc_ref[...] + _dot_tn(p_ref[0], vals)

    @pl.when(g == n_pages // pps - 1)
    def _finish():
        r = lax.broadcasted_iota(jnp.int32, (LANES, LANES), 0)
        c = lax.broadcasted_iota(jnp.int32, (LANES, LANES), 1)
        lcol = jnp.sum(jnp.where(r == c, jnp.broadcast_to(l_ref[0], (LANES, LANES)), 0.0), axis=1, keepdims=True)
        for h in range(MB_HEADS):
            cols = slice(h * HEAD_DIM, (h + 1) * HEAD_DIM)
            o = acc_ref[h * ts:(h + 1) * ts, cols] / lcol[h * ts:(h + 1) * ts, :]
            o_ref[:, cols] = o * _silu(gate_ref[:, cols])


def _moba_sample(z, k_s, v_s, cache_k, cache_v, page_table, bs, cfar, ts, *, name):
    nb, n_pages = page_table.shape
    pps = math.gcd(n_pages, MAX_PAGES_PER_STEP)
    assert (pps * PAGE_SIZE) % MB_BLOCK == 0
    assert ts & (ts - 1) == 0 and MB_HEADS * ts <= LANES and z.shape[0] == nb * ts
    n_steps = n_pages // pps
    n_blocks = n_pages * PAGE_SIZE // MB_BLOCK
    n_keys = (n_pages + 1) * PAGE_SIZE
    kc = cache_k.transpose(0, 2, 1, 3)
    vc = cache_v.transpose(0, 2, 1, 3)
    row_spec = lambda col: pl.BlockSpec((ts, MB_WIDTH), lambda b, g, pt: (b, col))
    new_spec = pl.BlockSpec((MB_HEADS, ts, HEAD_DIM), lambda b, g, pt: (0, b, 0))
    page_spec = lambda u: pl.BlockSpec((1, MB_HEADS, PAGE_SIZE, HEAD_DIM), lambda b, g, pt: (pt[b, g * pps + u], 0, 0, 0))
    const2 = lambda b, g, pt: (0, 0)

    probs, denom = pl.pallas_call(
        functools.partial(_sample_scores_kernel, n_pages=n_pages, pps=pps, ts=ts),
        grid_spec=pltpu.PrefetchScalarGridSpec(
            num_scalar_prefetch=1,
            grid=(nb, n_steps),
            in_specs=[row_spec(0), new_spec] + [page_spec(u) for u in range(pps)]
                     + [pl.BlockSpec(bs.shape, const2), pl.BlockSpec(cfar.shape, const2)],
            out_specs=[pl.BlockSpec((1, n_keys, LANES), lambda b, g, pt: (b, 0, 0)),
                       pl.BlockSpec((1, 1, LANES), lambda b, g, pt: (b, 0, 0))],
            scratch_shapes=[pltpu.VMEM((LANES, MB_WIDTH), BF16),
                            pltpu.VMEM((LANES, MB_WIDTH), BF16),
                            pltpu.VMEM((n_pages * PAGE_SIZE, LANES), F32),
                            pltpu.VMEM((n_blocks, MB_WIDTH), F32),
                            pltpu.VMEM((n_blocks, LANES), F32)],
        ),
        out_shape=[jax.ShapeDtypeStruct((nb, n_keys, LANES), BF16),
                   jax.ShapeDtypeStruct((nb, 1, LANES), F32)],
        compiler_params=_params("arbitrary", "arbitrary"),
        name=name + "_scores",
    )(page_table, z, k_s, *([kc] * pps), bs, cfar)

    return pl.pallas_call(
        functools.partial(_sample_values_kernel, n_pages=n_pages, pps=pps, ts=ts),
        grid_spec=pltpu.PrefetchScalarGridSpec(
            num_scalar_prefetch=1,
            grid=(nb, n_steps),
            in_specs=[row_spec(1), new_spec,
                      pl.BlockSpec((1, pps * PAGE_SIZE, LANES), lambda b, g, pt: (b, g, 0)),
                      pl.BlockSpec((1, PAGE_SIZE, LANES), lambda b, g, pt: (b, n_pages, 0)),
                      pl.BlockSpec((1, 1, LANES), lambda b, g, pt: (b, 0, 0))]
                     + [page_spec(u) for u in range(pps)],
            out_specs=pl.BlockSpec((ts, MB_WIDTH), lambda b, g, pt: (b, 0)),
            scratch_shapes=[pltpu.VMEM((LANES, MB_WIDTH), F32)],
        ),
        out_shape=jax.ShapeDtypeStruct((nb * ts, MB_WIDTH), F32),
        compiler_params=_params("arbitrary", "arbitrary"),
        name=name + "_values",
    )(page_table, z, v_s, probs, probs, denom, *([vc] * pps))


def kernel(x_prompt, x_sample, cache_k, cache_v, cache_mem_k, cache_mem_v, state_hgrn, page_table, mem_prompt,
           g_norm, w_in_a, hg_lb, g_hg_out, w_out_a, w_in_b, w_out_b, g_kv, w_kv, rel_bias, g_mem, w_mem_kv,
           g_final):
    bp, tp, d = x_prompt.shape
    bs_, ts, _ = x_sample.shape
    assert bp == 1 and w_in_a.shape[0] == 1 and w_in_b.shape[0] == 1
    n_pages = page_table.shape[1]
    assert (n_pages * PAGE_SIZE) % MB_BLOCK == 0 and tp % MB_BLOCK == 0
    rows_s = bs_ * ts

    x0_p = x_prompt.reshape(tp, d)
    x0_s = x_sample.reshape(rows_s, d)
    bf = lambda w: w.astype(BF16)

    mem_kv = [_norm_matmul(mem_prompt.reshape(-1, d), g_mem[l], bf(w_mem_kv[l]), name=f"mem_kv_{l}")
              for l in range(2)]
    mem_len = mem_kv[0].shape[0]

    def in_proj(xp, xs, g, w, tag):
        w = bf(w)
        return (_norm_matmul(xp, g, w, name=f"in_proj_{tag}_prompt"), _norm_matmul(xs, g, w, name=f"in_proj_{tag}_sample"))

    def mem_attn(zp, zs, col_q, l, tag):
        return (_mem_attn_prompt(zp, col_q, mem_kv[l], name=f"mem_attn_{tag}_prompt"),
                _mem_attn_sample(zs, col_q, cache_mem_k, cache_mem_v, l, ts, name=f"mem_attn_{tag}_sample"))

    z_p, z_s = in_proj(x0_p, x0_s, g_norm[0], w_in_a[0], "a")
    mix_p, s_prompt = _hgrn_prompt(z_p, hg_lb, g_hg_out[0], tp, 0, name="hgrn_prompt")
    mix_s, s_sample = _hgrn_sample(z_s, hg_lb, g_hg_out[0], state_hgrn[0], ts, 0, name="hgrn_sample")
    mem_p, mem_s = mem_attn(z_p, z_s, 4 * HG_WIDTH // MEM_WIDTH, 0, "a")
    x1_p, x1_s = _out_proj(mix_p, mix_s, mem_p, mem_s, bf(w_out_a[0]), x0_p, x0_s, name="out_proj_a")

    k_p, v_p, kb_p, vb_p = _shared_kv(x1_p, g_kv, bf(w_kv), True, name="shared_kv_prompt")
    k_s, v_s = _shared_kv(x1_s, g_kv, bf(w_kv), False, name="shared_kv_sample")
    kmean = _block_mean(k_p, tp // MB_BLOCK)

    blk = MB_BLOCK
    c = jnp.arange(2 * blk, dtype=jnp.int32)
    dist_p = (jnp.arange(NEAR_BLOCKS, dtype=jnp.int32)[:, None] * blk + blk - c[None, :]).reshape(-1)
    brows = _bias_table(rel_bias, dist_p)[:MB_HEADS].reshape(MB_HEADS, NEAR_BLOCKS, 2 * blk)
    near_pages = min(NEAR_PAGES, n_pages)
    n_keys = (near_pages + 1) * PAGE_SIZE
    key_x = jnp.arange(n_keys, dtype=jnp.int32)
    dist_s = (near_pages * PAGE_SIZE - key_x[None, :] + jnp.arange(ts, dtype=jnp.int32)[:, None]).reshape(-1)
    bs_tab = _bias_table(rel_bias, dist_s)[:MB_HEADS].reshape(MB_HEADS, ts, n_keys)
    bs_tab = jnp.pad(bs_tab.transpose(2, 0, 1).reshape(n_keys, MB_HEADS * ts), ((0, 0), (0, LANES - MB_HEADS * ts)))
    cfar = jnp.pad(jnp.repeat(brows[:, NEAR_BLOCKS - 1, 0], ts), (0, LANES - MB_HEADS * ts)).reshape(1, LANES)

    zb_p, zb_s = in_proj(x1_p, x1_s, g_norm[1], w_in_b[0], "b")
    mix_p = _moba_prompt(zb_p, kb_p, vb_p, kmean, brows, tp, name="moba_prompt")
    mix_s = _moba_sample(zb_s, k_s, v_s, cache_k, cache_v, page_table, bs_tab, cfar, ts, name="moba_sample")
    mem_p, mem_s = mem_attn(zb_p, zb_s, 2 * MB_WIDTH // MEM_WIDTH, 1, "b")
    y_p, y_s = _out_proj(mix_p, mix_s, mem_p, mem_s, bf(w_out_b[0]), x1_p, x1_s, g_final, name="out_proj_b")

    heads = lambda a, b_, t: a.reshape(MB_HEADS, b_, t, HEAD_DIM).transpose(1, 2, 0, 3)
    memh = lambda lo: jnp.stack([kv[:, lo:lo + MEM_WIDTH] for kv in mem_kv]).reshape(2, bp, mem_len, MEM_HEADS, HEAD_DIM)
    return (y_p.reshape(bp, tp, d), y_s.reshape(bs_, ts, d),
            heads(k_p, bp, tp), heads(v_p, bp, tp), heads(k_s, bs_, ts), heads(v_s, bs_, ts),
            s_prompt[None, None].astype(state_hgrn.dtype), s_sample[None].astype(state_hgrn.dtype),
            memh(0), memh(MEM_WIDTH))
```

```python
import functools
import math

import jax
import jax.numpy as jnp
from jax import lax
from jax.experimental import pallas as pl
from jax.experimental.pallas import tpu as pltpu

F32 = jnp.float32
BF16 = jnp.bfloat16

HEAD_DIM = 128
HG_HEADS = 12
MB_HEADS = 12
MEM_HEADS = 4
MB_BLOCK = 256
MB_TOPK = 3
PAGE_SIZE = 128
N_BUCKETS = 32
MAX_DISTANCE = 1024
EPS = 1e-6
HG_WIDTH = HG_HEADS * HEAD_DIM
MB_WIDTH = MB_HEADS * HEAD_DIM
MEM_WIDTH = MEM_HEADS * HEAD_DIM

SUBLANES = 8
LANES = 128
HG_CHUNK = 128
HG_BLOCK_CHUNKS = 16
HG_SAMPLE_UNROLL = 4
MEM_SAMPLE_SEQS = 2
NEAR_BLOCKS = 5
NEAR_PAGES = (NEAR_BLOCKS - 1) * MB_BLOCK // PAGE_SIZE
MAX_PAGES_PER_STEP = 16
VMEM_LIMIT = 48 * 1024 * 1024

NT_DIMS = (((1,), (1,)), ((), ()))
TN_DIMS = (((0,), (0,)), ((), ()))


def _dot(a, b):
    return jnp.dot(a, b, preferred_element_type=F32)


def _dot_nt(a, b):
    return lax.dot_general(a, b, NT_DIMS, preferred_element_type=F32)


def _dot_tn(a, b):
    return lax.dot_general(a, b, TN_DIMS, preferred_element_type=F32)


def _sigmoid(x):
    return 1.0 / (1.0 + jnp.exp(-x))


def _silu(x):
    return x * _sigmoid(x)


def _split2(x):
    hi = x.astype(BF16)
    lo = (x - hi.astype(F32)).astype(BF16)
    return hi, lo


def _split3(x):
    hi = x.astype(BF16)
    r = x - hi.astype(F32)
    mid = r.astype(BF16)
    lo = (r - mid.astype(F32)).astype(BF16)
    return hi, mid, lo


def _row_tile(m, cap, mult):
    best = None
    for t in range(mult, min(m, cap) + 1, mult):
        if m % t == 0:
            best = t
    assert best is not None, (m, cap, mult)
    return best


def _params(*sem):
    return pltpu.CompilerParams(dimension_semantics=sem, vmem_limit_bytes=VMEM_LIMIT)


def _pad_rows(x, rows):
    if rows == x.shape[0]:
        return x
    return jnp.concatenate([x, jnp.zeros((rows - x.shape[0], x.shape[1]), x.dtype)], axis=0)


def _normalize_rows(x_ref, g_ref, xn_ref):
    rows = x_ref.shape[0]
    step = 128 if rows % 128 == 0 else rows
    for r in range(0, rows, step):
        x = x_ref[r:r + step, :]
        ms = jnp.mean(x * x, axis=-1, keepdims=True)
        xn_ref[r:r + step, :] = (x * lax.rsqrt(ms + EPS) * g_ref[...]).astype(BF16)


def _norm_matmul_kernel(x_ref, g_ref, w_ref, o_ref, xn_ref):
    @pl.when(pl.program_id(1) == 0)
    def _():
        _normalize_rows(x_ref, g_ref, xn_ref)

    o_ref[...] = _dot(xn_ref[...], w_ref[...])


def _norm_matmul(x, g, w, *, name):
    m, d = x.shape
    n = w.shape[1]
    tm = _row_tile(m, 1024, 256)
    tn = _row_tile(n, 1024, 256)
    return pl.pallas_call(
        _norm_matmul_kernel,
        grid=(m // tm, n // tn),
        in_specs=[pl.BlockSpec((tm, d), lambda i, j: (i, 0)),
                  pl.BlockSpec((1, d), lambda i, j: (0, 0)),
                  pl.BlockSpec((d, tn), lambda i, j: (0, j))],
        out_specs=pl.BlockSpec((tm, tn), lambda i, j: (i, j)),
        out_shape=jax.ShapeDtypeStruct((m, n), F32),
        scratch_shapes=[pltpu.VMEM((tm, d), BF16)],
        compiler_params=_params("parallel", "arbitrary"),
        name=name,
    )(x, g.reshape(1, d), w)


def _shared_kv_kernel(x_ref, g_ref, w_ref, *rest):
    *o_refs, xn_ref = rest
    j = pl.program_id(1)

    @pl.when(j == 0)
    def _():
        _normalize_rows(x_ref, g_ref, xn_ref)

    y = _dot(xn_ref[...], w_ref[...])

    def write(refs):
        for h in range(MB_HEADS):
            for ref in refs:
                ref[h] = y[:, h * HEAD_DIM:(h + 1) * HEAD_DIM].astype(ref.dtype)

    @pl.when(j == 0)
    def _():
        write(o_refs[0::2])

    @pl.when(j == 1)
    def _():
        write(o_refs[1::2])


def _shared_kv(x, g, w, with_bf16, *, name):
    rows, d = x.shape
    tm = _row_tile(rows, 512, 256)
    assert w.shape[1] == 2 * MB_WIDTH
    shape = (MB_HEADS, rows, HEAD_DIM)
    spec = pl.BlockSpec((MB_HEADS, tm, HEAD_DIM), lambda i, j: (0, i, 0))
    dtypes = [F32, F32] + ([BF16, BF16] if with_bf16 else [])
    return pl.pallas_call(
        _shared_kv_kernel,
        grid=(rows // tm, 2),
        in_specs=[pl.BlockSpec((tm, d), lambda i, j: (i, 0)),
                  pl.BlockSpec((1, d), lambda i, j: (0, 0)),
                  pl.BlockSpec((d, MB_WIDTH), lambda i, j: (0, j))],
        out_specs=[spec] * len(dtypes),
        out_shape=[jax.ShapeDtypeStruct(shape, t) for t in dtypes],
        scratch_shapes=[pltpu.VMEM((tm, d), BF16)],
        compiler_params=_params("parallel", "arbitrary"),
        name=name,
    )(x, g.reshape(1, d), w)


def _out_proj_kernel(ap_ref, as_ref, bp_ref, bs_ref, wa_ref, wb_ref, xp_ref, xs_ref, *rest, prompt_tiles,
                     final_norm):
    if final_norm:
        g_ref, yp_ref, ys_ref = rest
    else:
        yp_ref, ys_ref = rest
    i = pl.program_id(0)

    def run(a_ref, b_ref, x_ref, o_ref):
        y = _dot(a_ref[...].astype(BF16), wa_ref[...]) + _dot(b_ref[...].astype(BF16), wb_ref[...])
        y = x_ref[...] + y
        if final_norm:
            ms = jnp.mean(y * y, axis=-1, keepdims=True)
            y = y * lax.rsqrt(ms + EPS) * g_ref[...]
        o_ref[...] = y

    @pl.when(i < prompt_tiles)
    def _():
        run(ap_ref, bp_ref, xp_ref, yp_ref)

    @pl.when(i >= prompt_tiles)
    def _():
        run(as_ref, bs_ref, xs_ref, ys_ref)


def _out_proj(main_p, main_s, mem_p, mem_s, w, x_p, x_s, g_final=None, *, name):
    tp, d = x_p.shape
    rows_s = x_s.shape[0]
    wm, wmem = main_p.shape[1], mem_p.shape[1]
    assert wm % wmem == 0
    tm = _row_tile(math.gcd(tp, rows_s), 256, 8)
    pt = tp // tm
    p_map = lambda i: (jnp.minimum(i, pt - 1), 0)
    s_map = lambda i: (jnp.maximum(i - pt, 0), 0)
    in_specs = [pl.BlockSpec((tm, wm), p_map), pl.BlockSpec((tm, wm), s_map),
                pl.BlockSpec((tm, wmem), p_map), pl.BlockSpec((tm, wmem), s_map),
                pl.BlockSpec((wm, d), lambda i: (0, 0)),
                pl.BlockSpec((wmem, d), lambda i: (wm // wmem, 0)),
                pl.BlockSpec((tm, d), p_map), pl.BlockSpec((tm, d), s_map)]
    args = [main_p, main_s, mem_p, mem_s, w, w, x_p, x_s]
    if g_final is not None:
        in_specs.append(pl.BlockSpec((1, d), lambda i: (0, 0)))
        args.append(g_final.reshape(1, d))
    return pl.pallas_call(
        functools.partial(_out_proj_kernel, prompt_tiles=pt, final_norm=g_final is not None),
        grid=((tp + rows_s) // tm,),
        in_specs=in_specs,
        out_specs=[pl.BlockSpec((tm, d), p_map), pl.BlockSpec((tm, d), s_map)],
        out_shape=[jax.ShapeDtypeStruct((tp, d), F32), jax.ShapeDtypeStruct((rows_s, d), F32)],
        compiler_params=_params("arbitrary"),
        name=name,
    )(*args)


def _mem_attn_heads(q_ref, gate_ref, o_ref, rows, kv_of_head):
    scale = HEAD_DIM ** -0.5
    for h in range(MEM_HEADS):
        cols = slice(h * HEAD_DIM, (h + 1) * HEAD_DIM)
        k, v = kv_of_head(h)
        s = _dot_nt(q_ref[rows, cols].astype(BF16), k.astype(BF16)) * scale
        e = jnp.exp(s - jnp.max(s, axis=-1, keepdims=True))
        o = _dot(e.astype(BF16), v.astype(BF16)) / jnp.sum(e, axis=-1, keepdims=True)
        o_ref[rows, cols] = (o * _silu(gate_ref[rows, cols])).astype(o_ref.dtype)


def _mem_attn_prompt_kernel(q_ref, gate_ref, mkv_ref, o_ref):
    def kv_of_head(h):
        return (mkv_ref[:, h * HEAD_DIM:(h + 1) * HEAD_DIM],
                mkv_ref[:, MEM_WIDTH + h * HEAD_DIM:MEM_WIDTH + (h + 1) * HEAD_DIM])

    _mem_attn_heads(q_ref, gate_ref, o_ref, slice(None), kv_of_head)


def _mem_attn_sample_kernel(q_ref, gate_ref, mk_ref, mv_ref, o_ref, *, ts):
    mem_len = mk_ref.shape[1] // MEM_HEADS
    for s in range(mk_ref.shape[0]):
        def kv_of_head(h, s=s):
            rows = pl.ds(h, mem_len, stride=MEM_HEADS)
            return mk_ref[s, rows, :], mv_ref[s, rows, :]

        _mem_attn_heads(q_ref, gate_ref, o_ref, slice(s * ts, (s + 1) * ts), kv_of_head)


def _mem_attn_prompt(z, col_q, mkv, *, name):
    rows = z.shape[0]
    mem_len = mkv.shape[0]
    tm = _row_tile(rows, 512, 8)
    return pl.pallas_call(
        _mem_attn_prompt_kernel,
        grid=(rows // tm,),
        in_specs=[pl.BlockSpec((tm, MEM_WIDTH), lambda t: (t, col_q)),
                  pl.BlockSpec((tm, MEM_WIDTH), lambda t: (t, col_q + 1)),
                  pl.BlockSpec((mem_len, 2 * MEM_WIDTH), lambda t: (0, 0))],
        out_specs=pl.BlockSpec((tm, MEM_WIDTH), lambda t: (t, 0)),
        out_shape=jax.ShapeDtypeStruct((rows, MEM_WIDTH), BF16),
        compiler_params=_params("parallel"),
        name=name,
    )(z, z, mkv)


def _mem_attn_sample(z, col_q, cache_mk, cache_mv, layer, ts, *, name):
    _, nb, mem_len, heads, hd = cache_mk.shape
    assert heads == MEM_HEADS and hd == HEAD_DIM
    view = lambda c: c.reshape(c.shape[0], nb, mem_len * heads, hd)
    ns = math.gcd(nb, MEM_SAMPLE_SEQS)
    cache_spec = pl.BlockSpec((None, ns, mem_len * heads, hd), lambda b: (layer, b, 0, 0))
    return pl.pallas_call(
        functools.partial(_mem_attn_sample_kernel, ts=ts),
        grid=(nb // ns,),
        in_specs=[pl.BlockSpec((ns * ts, MEM_WIDTH), lambda b: (b, col_q)),
                  pl.BlockSpec((ns * ts, MEM_WIDTH), lambda b: (b, col_q + 1)),
                  cache_spec, cache_spec],
        out_specs=pl.BlockSpec((ns * ts, MEM_WIDTH), lambda b: (b, 0)),
        out_shape=jax.ShapeDtypeStruct((nb * ts, MEM_WIDTH), F32),
        compiler_params=_params("parallel"),
        name=name,
    )(z, z, view(cache_mk), view(cache_mv))


def _cumsum_rows(x):
    n = x.shape[0]
    row = lax.broadcasted_iota(jnp.int32, x.shape, 0)
    s = 1
    while s < n:
        x = x + jnp.where(row >= s, pltpu.roll(x, s, 0), 0.0)
        s *= 2
    return x


def _lower_bound(lbp, layer):
    e = jnp.exp(lbp - jnp.max(lbp, axis=0, keepdims=True))
    return jnp.sum(e[:layer + 1], axis=0, keepdims=True) / jnp.sum(e, axis=0, keepdims=True)


def _hgrn_gates(hq, hf, lb):
    q = _silu(hq)
    f = lb + (1.0 - lb) * _sigmoid(hf)
    return q, 1.0 - f, jnp.log2(f)


def _hgrn_finish(o, hg, go):
    ms = jnp.mean(o * o, axis=-1, keepdims=True)
    return o * lax.rsqrt(ms + EPS) * go * _silu(hg)


def _diag_tiles(q, k, b, width):
    sub = SUBLANES
    lane = lax.broadcasted_iota(jnp.int32, (sub, width), 1)
    trow = lax.broadcasted_iota(jnp.int32, (sub, width), 0)
    tiles = []
    for r0 in range(0, q.shape[0], sub):
        q8, k8, b8 = q[r0:r0 + sub], k[r0:r0 + sub], b[r0:r0 + sub]
        tile = jnp.zeros((sub, width), F32)
        for s in range(sub):
            e = jnp.exp2(b8 - b8[s:s + 1])
            a_col = jnp.sum(q8 * e * k8[s:s + 1], axis=1, keepdims=True)
            tile = jnp.where(lane == r0 + s, a_col, tile)
        tiles.append(jnp.where(trow >= lane - r0, tile, 0.0))
    return jnp.concatenate(tiles, axis=0)


def _level_ref(b, level):
    parts = []
    for start in range(0, b.shape[0], 2 * level):
        r = start + level - 1
        parts.append(jnp.broadcast_to(b[r:r + 1, :], (2 * level, b.shape[1])))
    return jnp.concatenate(parts, axis=0)


def _level_masks(c):
    row = lax.broadcasted_iota(jnp.int32, (c, c), 0)
    col = lax.broadcasted_iota(jnp.int32, (c, c), 1)
    masks = []
    level = SUBLANES
    while level < c:
        sh = level.bit_length() - 1
        same = (row >> (sh + 1)) == (col >> (sh + 1))
        masks.append((level, same & (((row >> sh) & 1) == 1) & (((col >> sh) & 1) == 0)))
        level *= 2
    return masks


def _hgrn_chunk(q, k, g, v, st, masks):
    c = HG_CHUNK
    b = _cumsum_rows(g)
    a = _diag_tiles(q, k, b, c)
    for level, mask in masks:
        e = jnp.exp2(-jnp.abs(b - _level_ref(b, level)))
        a = jnp.where(mask, _dot_nt((q * e).astype(BF16), (k * e).astype(BF16)), a)
    b_end = b[c - 1:c, :]
    vb = v.astype(BF16)
    o = _dot(a.astype(BF16), vb) + _dot_nt((q * jnp.exp2(b)).astype(BF16), st.astype(BF16))
    st_new = st * jnp.exp2(b_end) + _dot_tn(vb, (k * jnp.exp2(b_end - b)).astype(BF16))
    return o, st_new


def _hgrn_prompt_kernel(hq_ref, hf_ref, hi_ref, hg_ref, lbp_ref, go_ref, mix_ref, sout_ref, st_ref, *, layer):
    n = pl.program_id(1)

    @pl.when(n == 0)
    def _():
        st_ref[...] = jnp.zeros_like(st_ref)

    lb = _lower_bound(lbp_ref[...], layer)
    go = go_ref[...]
    masks = _level_masks(HG_CHUNK)
    for c in range(HG_BLOCK_CHUNKS):
        rows = slice(c * HG_CHUNK, (c + 1) * HG_CHUNK)
        q, k, g = _hgrn_gates(hq_ref[rows, :], hf_ref[rows, :], lb)
        o, st_new = _hgrn_chunk(q, k, g, hi_ref[rows, :], st_ref[...], masks)
        st_ref[...] = st_new
        mix_ref[rows, :] = _hgrn_finish(o, hg_ref[rows, :], go).astype(mix_ref.dtype)

    @pl.when(n == pl.num_programs(1) - 1)
    def _():
        sout_ref[0] = st_ref[...].T


def _hgrn_prompt(z, hg_lb, g_o, tp, layer, *, name):
    rb = HG_CHUNK * HG_BLOCK_CHUNKS
    assert tp % rb == 0
    h_ = HG_HEADS
    n_lb = hg_lb.shape[0]
    zspec = lambda off: pl.BlockSpec((rb, HEAD_DIM), lambda h, n: (n, off + h))
    return pl.pallas_call(
        functools.partial(_hgrn_prompt_kernel, layer=layer),
        grid=(h_, tp // rb),
        in_specs=[zspec(0), zspec(h_), zspec(2 * h_), zspec(3 * h_),
                  pl.BlockSpec((n_lb, HEAD_DIM), lambda h, n: (0, h)),
                  pl.BlockSpec((1, HEAD_DIM), lambda h, n: (0, h))],
        out_specs=[pl.BlockSpec((rb, HEAD_DIM), lambda h, n: (n, h)),
                   pl.BlockSpec((1, HEAD_DIM, HEAD_DIM), lambda h, n: (h, 0, 0))],
        out_shape=[jax.ShapeDtypeStruct((tp, HG_WIDTH), BF16),
                   jax.ShapeDtypeStruct((h_, HEAD_DIM, HEAD_DIM), F32)],
        scratch_shapes=[pltpu.VMEM((HEAD_DIM, HEAD_DIM), F32)],
        compiler_params=_params("parallel", "arbitrary"),
        name=name,
    )(z, z, z, z, hg_lb, g_o.reshape(1, HG_WIDTH))


def _hgrn_sample_kernel(hq_ref, hf_ref, hi_ref, hg_ref, lbp_ref, go_ref, s0_ref, mix_ref, sout_ref, *, layer, ts, nb):
    lb = _lower_bound(lbp_ref[...], layer)
    go = go_ref[...]

    def one_sequence(bi):
        rows = pl.ds(pl.multiple_of(bi * ts, ts), ts)
        q, k, g = _hgrn_gates(hq_ref[rows, :], hf_ref[rows, :], lb)
        b = _cumsum_rows(g)
        b_end = b[ts - 1:ts, :]
        st = s0_ref[bi, 0].T
        a = _diag_tiles(q, k, b, LANES)
        vb = _pad_rows(hi_ref[rows, :], LANES).astype(BF16)
        kh = _pad_rows(k * jnp.exp2(b_end - b), LANES).astype(BF16)
        o = _dot(a.astype(BF16), vb) + _dot_nt((q * jnp.exp2(b)).astype(BF16), st.astype(BF16))
        st_new = st * jnp.exp2(b_end) + _dot_tn(vb, kh)
        sout_ref[bi, 0] = st_new.T
        mix_ref[rows, :] = _hgrn_finish(o, hg_ref[rows, :], go)

    def body(p, carry):
        for e in range(HG_SAMPLE_UNROLL):
            one_sequence(HG_SAMPLE_UNROLL * p + e)
        return carry

    lax.fori_loop(0, nb // HG_SAMPLE_UNROLL, body, 0)


def _hgrn_sample(z, hg_lb, g_o, s0, ts, layer, *, name):
    nb = s0.shape[0]
    rows = nb * ts
    assert rows == z.shape[0] and ts == SUBLANES and nb % HG_SAMPLE_UNROLL == 0
    h_ = HG_HEADS
    n_lb = hg_lb.shape[0]
    zspec = lambda off: pl.BlockSpec((rows, HEAD_DIM), lambda h: (0, off + h))
    return pl.pallas_call(
        functools.partial(_hgrn_sample_kernel, layer=layer, ts=ts, nb=nb),
        grid=(h_,),
        in_specs=[zspec(0), zspec(h_), zspec(2 * h_), zspec(3 * h_),
                  pl.BlockSpec((n_lb, HEAD_DIM), lambda h: (0, h)),
                  pl.BlockSpec((1, HEAD_DIM), lambda h: (0, h)),
                  pl.BlockSpec((nb, 1, HEAD_DIM, HEAD_DIM), lambda h: (0, h, 0, 0))],
        out_specs=[pl.BlockSpec((rows, HEAD_DIM), lambda h: (0, h)),
                   pl.BlockSpec((nb, 1, HEAD_DIM, HEAD_DIM), lambda h: (0, h, 0, 0))],
        out_shape=[jax.ShapeDtypeStruct((rows, HG_WIDTH), F32),
                   jax.ShapeDtypeStruct(s0.shape, F32)],
        compiler_params=_params("parallel"),
        name=name,
    )(z, z, z, z, hg_lb, g_o.reshape(1, HG_WIDTH), s0)


def _t5_bucket(dist):
    exact = N_BUCKETS // 2
    d = jnp.maximum(dist, exact).astype(F32)
    large = exact + (jnp.log(d / exact) / math.log(MAX_DISTANCE / exact) * (N_BUCKETS - exact)).astype(jnp.int32)
    return jnp.where(dist < exact, dist, jnp.minimum(large, N_BUCKETS - 1))


def _bias_table_kernel(rb_ref, bucket_ref, o_ref):
    hi, mid, lo = _split3(rb_ref[...])
    row = lax.broadcasted_iota(jnp.int32, (LANES, bucket_ref.shape[1]), 0)
    oh = jnp.where(row == bucket_ref[...], 1.0, 0.0).astype(BF16)
    o_ref[...] = _dot(hi, oh) + _dot(mid, oh) + _dot(lo, oh)


def _bias_table(rel_bias, dist):
    n = dist.shape[0]
    rb = jnp.zeros((16, LANES), F32).at[:MB_HEADS, :N_BUCKETS].set(rel_bias.T)
    bucket = _t5_bucket(jnp.maximum(dist, 0).astype(jnp.int32)).reshape(1, n)
    return pl.pallas_call(
        _bias_table_kernel,
        out_shape=jax.ShapeDtypeStruct((16, n), F32),
        name="bias_table",
    )(rb, bucket)


def _block_mean_kernel(k_ref, o_ref):
    for n in range(o_ref.shape[0]):
        rows = slice(n * MB_BLOCK, (n + 1) * MB_BLOCK)
        o_ref[n:n + 1, :] = jnp.sum(k_ref[rows, :], axis=0, keepdims=True) * (1.0 / MB_BLOCK)


def _block_mean(k, n_blocks):
    return pl.pallas_call(
        _block_mean_kernel,
        grid=(MB_HEADS,),
        in_specs=[pl.BlockSpec((None, n_blocks * MB_BLOCK, HEAD_DIM), lambda h: (h, 0, 0))],
        out_specs=pl.BlockSpec((None, n_blocks, HEAD_DIM), lambda h: (h, 0, 0)),
        out_shape=jax.ShapeDtypeStruct((MB_HEADS, n_blocks, HEAD_DIM), F32),
        compiler_params=_params("parallel"),
        name="block_mean",
    )(k)


def _select_topk(gate, axis, n_blocks):
    idx = lax.broadcasted_iota(jnp.int32, gate.shape, axis)
    sel = jnp.zeros(gate.shape, F32)
    for _ in range(min(MB_TOPK, n_blocks)):
        mx = jnp.max(gate, axis=axis, keepdims=True)
        first = jnp.min(jnp.where(gate == mx, idx, n_blocks), axis=axis, keepdims=True)
        pick = idx == first
        sel = jnp.where(pick & (mx > -jnp.inf), 1.0, sel)
        gate = jnp.where(pick, -jnp.inf, gate)
    return sel


MASK_BIG = 2.0 ** 100
DUMMY_LANE = 125
FAR_LANES = (126, 127)
MOBA_GROUP_LOG2 = 3
MOBA_GROUP = 1 << MOBA_GROUP_LOG2
MOBA_HEADS_PER_STEP = 2


def _moba_prompt_kernel(q_ref, gate_ref, k_ref, v_ref, km_ref, brow_ref, o_ref, bias_ref, s_ref, mx_ref, acc_ref,
                        *, n_blocks):
    i = pl.program_id(1)
    blk = MB_BLOCK
    near = NEAR_BLOCKS
    heads = range(k_ref.shape[0])
    inv_scale = HEAD_DIM ** 0.5
    exp2_scale = HEAD_DIM ** -0.5 * math.log2(math.e)
    filler = n_blocks + near

    @pl.when(i == 0)
    def _build_bias():
        rowi = lax.broadcasted_iota(jnp.int32, (blk, 2 * blk), 0)
        for hh in heads:
            for d in range(near):
                x = jnp.broadcast_to(brow_ref[hh, d:d + 1, :], (blk, 2 * blk))
                for bit in range(blk.bit_length() - 1):
                    x = jnp.where(((rowi >> bit) & 1) == 1, pltpu.roll(x, 1 << bit, 1), x)
                bias_ref[hh, d] = x[:, blk:] * inv_scale
            s_ref[hh, filler] = jnp.full((blk, blk), -MASK_BIG, F32)

    tq = lax.broadcasted_iota(jnp.int32, (blk, blk), 0)
    tk = lax.broadcasted_iota(jnp.int32, (blk, blk), 1)
    lane = lax.broadcasted_iota(jnp.int32, (blk, LANES), 1)
    lane_r = lax.broadcasted_iota(jnp.int32, (1, LANES), 1)
    fold = lambda t: jnp.maximum(t[:, :LANES], t[:, LANES:])
    qa, far_row = [], []
    for hh in heads:
        qh, ql = _split2(q_ref[:, hh * HEAD_DIM:(hh + 1) * HEAD_DIM])

        own = _dot_nt(qh, k_ref[hh, pl.ds(pl.multiple_of(i * blk, blk), blk), :]) + bias_ref[hh, 0]
        own = jnp.where(tq >= tk, own, -MASK_BIG)
        s_ref[hh, i] = own
        mx_ref[hh] = fold(own)

        kmh, kml = _split2(_pad_rows(km_ref[hh], -(-n_blocks // 16) * 16))
        gate_t = _dot_nt(kmh, qh) + _dot_nt(kml, qh) + _dot_nt(kmh, ql)
        blk_id = lax.broadcasted_iota(jnp.int32, gate_t.shape, 0)
        sel_t = _select_topk(jnp.where(blk_id < i, gate_t, -jnp.inf), 0, n_blocks)
        sel = _pad_rows(sel_t, LANES).T

        qa.append(jnp.concatenate([qh, jnp.where(lane >= FAR_LANES[0], 1.0, sel - 1.0).astype(BF16)], axis=1))
        c_far = brow_ref[hh, near - 1:near, 0:1] * inv_scale
        c_hi = c_far.astype(BF16).astype(F32)
        far_row.append(jnp.where(lane_r == FAR_LANES[0], c_hi, jnp.where(lane_r == FAR_LANES[1], c_far - c_hi, 0.0)))

    def scores(hh, j, big_lane, row_vals):
        rows = pl.ds(pl.multiple_of(jnp.minimum(j, i) * blk, blk), blk)
        right = jnp.broadcast_to(jnp.where(lane_r == big_lane, MASK_BIG, row_vals), (blk, LANES)).astype(BF16)
        return _dot_nt(qa[hh], jnp.concatenate([k_ref[hh, rows, :], right], axis=1))

    first_near = jnp.maximum(i - (near - 1), 0)

    def grouped(count, blocks_fn):
        n_full = count >> MOBA_GROUP_LOG2
        rem = count & (MOBA_GROUP - 1)
        half = MOBA_GROUP // 2
        for group, lo, hi in ((MOBA_GROUP, 0, n_full + jnp.where(rem > half, 1, 0)),
                              (half, 2 * n_full, 2 * n_full + jnp.where((rem > 0) & (rem <= half), 1, 0))):
            def body(p, carry, group=group):
                blocks_fn(group * p, group)
                return carry

            lax.fori_loop(lo, hi, body, 0)

    def far_blocks(first, group):
        mx = [mx_ref[hh] for hh in heads]
        for e in range(group):
            j = first + e
            is_far = j < first_near
            for hh in heads:
                raw = scores(hh, j, jnp.where(is_far, j, DUMMY_LANE), far_row[hh])
                s_ref[hh, jnp.where(is_far, j, n_blocks)] = raw
                mx[hh] = jnp.maximum(mx[hh], fold(raw))
        for hh in heads:
            mx_ref[hh] = mx[hh]

    grouped(first_near, far_blocks)

    mx = [mx_ref[hh] for hh in heads]
    for dlt in range(1, near):
        j = i - dlt
        jc = jnp.maximum(j, 0)
        for hh in heads:
            raw = scores(hh, jc, jnp.where(j >= 0, jc, DUMMY_LANE), 0.0) + bias_ref[hh, dlt]
            s_ref[hh, jnp.where(j >= 0, jc, n_blocks + dlt)] = raw
            mx[hh] = jnp.maximum(mx[hh], fold(raw))
    for hh in heads:
        mx_ref[hh] = jnp.broadcast_to(jnp.max(mx[hh], axis=1, keepdims=True), (blk, LANES))

    ones = jnp.ones((blk, LANES), BF16)

    def pv_part(hh, first, count):
        top = jnp.concatenate([mx_ref[hh]] * 2, axis=1)
        probs, vals = [], []
        for e in range(count):
            j = first + e
            tile = s_ref[hh, jnp.where(j <= i, j, filler)]
            probs.append(jnp.exp2((tile - top) * exp2_scale).astype(BF16))
            rows = pl.ds(pl.multiple_of(jnp.minimum(j, i) * blk, blk), blk)
            vals.append(jnp.concatenate([v_ref[hh, rows, :], ones], axis=1))
        return _dot(jnp.concatenate(probs, axis=1), jnp.concatenate(vals, axis=0))

    def pv_blocks(first, group):
        half = group // 2
        for hh in heads:
            acc_ref[hh] = acc_ref[hh] + (pv_part(hh, first, half) + pv_part(hh, first + half, half))

    acc_ref[...] = jnp.zeros_like(acc_ref)
    grouped(i + 1, pv_blocks)
    for hh in heads:
        cols = slice(hh * HEAD_DIM, (hh + 1) * HEAD_DIM)
        acc = acc_ref[hh]
        o_ref[:, cols] = (acc[:, :HEAD_DIM] / acc[:, HEAD_DIM:] * _silu(gate_ref[:, cols])).astype(o_ref.dtype)


def _moba_prompt(z, kb, vb, kmean, brows, tp, *, name):
    nq = tp // MB_BLOCK
    hp = MOBA_HEADS_PER_STEP
    steps_h = MB_HEADS // hp
    assert nq <= DUMMY_LANE and MB_HEADS % hp == 0
    head_spec = lambda rows: pl.BlockSpec((hp, rows, HEAD_DIM), lambda h, i: (h, 0, 0))
    return pl.pallas_call(
        functools.partial(_moba_prompt_kernel, n_blocks=nq),
        grid=(steps_h, nq),
        in_specs=[pl.BlockSpec((MB_BLOCK, hp * HEAD_DIM), lambda h, i: (i, h)),
                  pl.BlockSpec((MB_BLOCK, hp * HEAD_DIM), lambda h, i: (i, steps_h + h)),
                  head_spec(tp), head_spec(tp), head_spec(nq),
                  pl.BlockSpec((hp, NEAR_BLOCKS, 2 * MB_BLOCK), lambda h, i: (h, 0, 0))],
        out_specs=pl.BlockSpec((MB_BLOCK, hp * HEAD_DIM), lambda h, i: (i, h)),
        out_shape=jax.ShapeDtypeStruct((tp, MB_WIDTH), BF16),
        scratch_shapes=[pltpu.VMEM((hp, NEAR_BLOCKS, MB_BLOCK, MB_BLOCK), F32),
                        pltpu.VMEM((hp, nq + NEAR_BLOCKS + 1, MB_BLOCK, MB_BLOCK), F32),
                        pltpu.VMEM((hp, MB_BLOCK, LANES), F32),
                        pltpu.VMEM((hp, MB_BLOCK, MB_BLOCK), F32)],
        compiler_params=_params("parallel", "arbitrary"),
        name=name,
    )(z, z, kb, vb, kmean, brows)


def _cat_heads(ref, *lead):
    return jnp.concatenate([ref[lead + (h,)] for h in range(MB_HEADS)], axis=1)


def _sample_scores_kernel(pt_ref, q_ref, knew_ref, *rest, n_pages, pps, ts):
    del pt_ref
    kc = rest[:pps]
    bs_ref, cfar_ref, p_ref, l_ref, wq_ref, wql_ref, s_ref, km_ref, sel_ref = rest[pps:]
    g = pl.program_id(1)
    n_steps = n_pages // pps
    n_blocks = n_pages * PAGE_SIZE // MB_BLOCK
    near_pages = min(NEAR_PAGES, n_pages)
    scale = HEAD_DIM ** -0.5
    pg = PAGE_SIZE
    ppb = MB_BLOCK // pg

    @pl.when(g == 0)
    def _start_sequence():
        rep = _pad_rows(jnp.concatenate([q_ref[...]] * MB_HEADS, axis=0), LANES)
        r_h = lax.broadcasted_iota(jnp.int32, rep.shape, 0) // ts
        c_h = lax.broadcasted_iota(jnp.int32, rep.shape, 1) // HEAD_DIM
        hi, lo = _split2(jnp.where(r_h == c_h, rep, 0.0))
        wq_ref[...] = hi
        wql_ref[...] = lo

    pages = [_cat_heads(kc[u], 0) for u in range(pps)]
    st = _dot_nt(jnp.concatenate([kp.astype(BF16) for kp in pages], axis=0), wq_ref[...])
    for u in range(pps):
        p = g * pps + u
        near_idx = jnp.maximum(p - (n_pages - near_pages), 0)
        b_near = bs_ref[pl.ds(pl.multiple_of(near_idx * pg, pg), pg), :]
        bias = jnp.where(p >= n_pages - near_pages, b_near, cfar_ref[...])
        s_ref[pl.ds(pl.multiple_of(p * pg, pg), pg), :] = st[u * pg:(u + 1) * pg] * scale + bias
    for n in range(pps // ppb):
        ksum = sum(jnp.sum(pages[n * ppb + u], axis=0, keepdims=True) for u in range(ppb))
        km_ref[pl.ds(g * (pps // ppb) + n, 1), :] = ksum * (1.0 / MB_BLOCK)

    @pl.when(g == n_steps - 1)
    def _softmax():
        st_new = _dot_nt(_pad_rows(_cat_heads(knew_ref), pg).astype(BF16), wq_ref[...])
        krow = lax.broadcasted_iota(jnp.int32, (pg, LANES), 0)
        qcol = lax.broadcasted_iota(jnp.int32, (pg, LANES), 1)
        valid = (krow < ts) & (krow <= (qcol & (ts - 1)))
        s_cur = jnp.where(valid, st_new * scale + bs_ref[pl.ds(near_pages * pg, pg), :], -jnp.inf)

        kmh, kml = _split2(km_ref[...])
        gate = _dot_nt(kmh, wq_ref[...]) + _dot_nt(kmh, wql_ref[...]) + _dot_nt(kml, wq_ref[...])
        sel_ref[...] = _select_topk(gate, 0, n_blocks)

        def max_body(n, m):
            keep = sel_ref[pl.ds(n, 1), :] > 0.5
            for u in range(ppb):
                tile = s_ref[pl.ds(pl.multiple_of((n * ppb + u) * pg, pg), pg), :]
                m = jnp.maximum(m, jnp.where(keep, tile, -jnp.inf))
            return m

        m = lax.fori_loop(0, n_blocks, max_body, s_cur)
        mrow = jnp.max(m, axis=0, keepdims=True)

        def exp_body(n, l):
            keep = sel_ref[pl.ds(n, 1), :] > 0.5
            for u in range(ppb):
                rows = pl.ds(pl.multiple_of((n * ppb + u) * pg, pg), pg)
                e = jnp.exp(jnp.where(keep, s_ref[rows, :] - mrow, -jnp.inf))
                p_ref[0, rows, :] = e.astype(BF16)
                l = l + e
            return l

        e_cur = jnp.exp(s_cur - mrow)
        p_ref[0, pl.ds(n_pages * pg, pg), :] = e_cur.astype(BF16)
        l = lax.fori_loop(0, n_blocks, exp_body, e_cur)
        l_ref[0] = jnp.sum(l, axis=0, keepdims=True)


def _sample_values_kernel(pt_ref, gate_ref, vnew_ref, p_ref, pcur_ref, l_ref, *rest, n_pages, pps, ts):
    del pt_ref
    vc = rest[:pps]
    o_ref, acc_ref = rest[pps:]
    g = pl.program_id(1)
    pg = PAGE_SIZE

    @pl.when(g == 0)
    def _own_block():
        acc_ref[...] = _dot_tn(pcur_ref[0], _pad_rows(_cat_heads(vnew_ref), pg).astype(BF16))

    vals = jnp.concatenate([_cat_heads(vc[u], 0).astype(BF16) for u in range(pps)], axis=0)
    acc_ref[...] = acc_ref[...] + _dot_tn(p_ref[0], vals)

    @pl.when(g == n_pages // pps - 1)
    def _finish():
        r = lax.broadcasted_iota(jnp.int32, (LANES, LANES), 0)
        c = lax.broadcasted_iota(jnp.int32, (LANES, LANES), 1)
        lcol = jnp.sum(jnp.where(r == c, jnp.broadcast_to(l_ref[0], (LANES, LANES)), 0.0), axis=1, keepdims=True)
        for h in range(MB_HEADS):
            cols = slice(h * HEAD_DIM, (h + 1) * HEAD_DIM)
            o = acc_ref[h * ts:(h + 1) * ts, cols] / lcol[h * ts:(h + 1) * ts, :]
            o_ref[:, cols] = o * _silu(gate_ref[:, cols])


def _moba_sample(z, k_s, v_s, cache_k, cache_v, page_table, bs, cfar, ts, *, name):
    nb, n_pages = page_table.shape
    pps = math.gcd(n_pages, MAX_PAGES_PER_STEP)
    assert (pps * PAGE_SIZE) % MB_BLOCK == 0
    assert ts & (ts - 1) == 0 and MB_HEADS * ts <= LANES and z.shape[0] == nb * ts
    n_steps = n_pages // pps
    n_blocks = n_pages * PAGE_SIZE // MB_BLOCK
    n_keys = (n_pages + 1) * PAGE_SIZE
    kc = cache_k.transpose(0, 2, 1, 3)
    vc = cache_v.transpose(0, 2, 1, 3)
    row_spec = lambda col: pl.BlockSpec((ts, MB_WIDTH), lambda b, g, pt: (b, col))
    new_spec = pl.BlockSpec((MB_HEADS, ts, HEAD_DIM), lambda b, g, pt: (0, b, 0))
    page_spec = lambda u: pl.BlockSpec((1, MB_HEADS, PAGE_SIZE, HEAD_DIM), lambda b, g, pt: (pt[b, g * pps + u], 0, 0, 0))
    const2 = lambda b, g, pt: (0, 0)

    probs, denom = pl.pallas_call(
        functools.partial(_sample_scores_kernel, n_pages=n_pages, pps=pps, ts=ts),
        grid_spec=pltpu.PrefetchScalarGridSpec(
            num_scalar_prefetch=1,
            grid=(nb, n_steps),
            in_specs=[row_spec(0), new_spec] + [page_spec(u) for u in range(pps)]
                     + [pl.BlockSpec(bs.shape, const2), pl.BlockSpec(cfar.shape, const2)],
            out_specs=[pl.BlockSpec((1, n_keys, LANES), lambda b, g, pt: (b, 0, 0)),
                       pl.BlockSpec((1, 1, LANES), lambda b, g, pt: (b, 0, 0))],
            scratch_shapes=[pltpu.VMEM((LANES, MB_WIDTH), BF16),
                            pltpu.VMEM((LANES, MB_WIDTH), BF16),
                            pltpu.VMEM((n_pages * PAGE_SIZE, LANES), F32),
                            pltpu.VMEM((n_blocks, MB_WIDTH), F32),
                            pltpu.VMEM((n_blocks, LANES), F32)],
        ),
        out_shape=[jax.ShapeDtypeStruct((nb, n_keys, LANES), BF16),
                   jax.ShapeDtypeStruct((nb, 1, LANES), F32)],
        compiler_params=_params("arbitrary", "arbitrary"),
        name=name + "_scores",
    )(page_table, z, k_s, *([kc] * pps), bs, cfar)

    return pl.pallas_call(
        functools.partial(_sample_values_kernel, n_pages=n_pages, pps=pps, ts=ts),
        grid_spec=pltpu.PrefetchScalarGridSpec(
            num_scalar_prefetch=1,
            grid=(nb, n_steps),
            in_specs=[row_spec(1), new_spec,
                      pl.BlockSpec((1, pps * PAGE_SIZE, LANES), lambda b, g, pt: (b, g, 0)),
                      pl.BlockSpec((1, PAGE_SIZE, LANES), lambda b, g, pt: (b, n_pages, 0)),
                      pl.BlockSpec((1, 1, LANES), lambda b, g, pt: (b, 0, 0))]
                     + [page_spec(u) for u in range(pps)],
            out_specs=pl.BlockSpec((ts, MB_WIDTH), lambda b, g, pt: (b, 0)),
            scratch_shapes=[pltpu.VMEM((LANES, MB_WIDTH), F32)],
        ),
        out_shape=jax.ShapeDtypeStruct((nb * ts, MB_WIDTH), F32),
        compiler_params=_params("arbitrary", "arbitrary"),
        name=name + "_values",
    )(page_table, z, v_s, probs, probs, denom, *([vc] * pps))


def kernel(x_prompt, x_sample, cache_k, cache_v, cache_mem_k, cache_mem_v, state_hgrn, page_table, mem_prompt,
           g_norm, w_in_a, hg_lb, g_hg_out, w_out_a, w_in_b, w_out_b, g_kv, w_kv, rel_bias, g_mem, w_mem_kv,
           g_final):
    bp, tp, d = x_prompt.shape
    bs_, ts, _ = x_sample.shape
    assert bp == 1 and w_in_a.shape[0] == 1 and w_in_b.shape[0] == 1
    n_pages = page_table.shape[1]
    assert (n_pages * PAGE_SIZE) % MB_BLOCK == 0 and tp % MB_BLOCK == 0
    rows_s = bs_ * ts

    x0_p = x_prompt.reshape(tp, d)
    x0_s = x_sample.reshape(rows_s, d)
    bf = lambda w: w.astype(BF16)

    mem_kv = [_norm_matmul(mem_prompt.reshape(-1, d), g_mem[l], bf(w_mem_kv[l]), name=f"mem_kv_{l}")
              for l in range(2)]
    mem_len = mem_kv[0].shape[0]

    def in_proj(xp, xs, g, w, tag):
        w = bf(w)
        return (_norm_matmul(xp, g, w, name=f"in_proj_{tag}_prompt"), _norm_matmul(xs, g, w, name=f"in_proj_{tag}_sample"))

    def mem_attn(zp, zs, col_q, l, tag):
        return (_mem_attn_prompt(zp, col_q, mem_kv[l], name=f"mem_attn_{tag}_prompt"),
                _mem_attn_sample(zs, col_q, cache_mem_k, cache_mem_v, l, ts, name=f"mem_attn_{tag}_sample"))

    z_p, z_s = in_proj(x0_p, x0_s, g_norm[0], w_in_a[0], "a")
    mix_p, s_prompt = _hgrn_prompt(z_p, hg_lb, g_hg_out[0], tp, 0, name="hgrn_prompt")
    mix_s, s_sample = _hgrn_sample(z_s, hg_lb, g_hg_out[0], state_hgrn[0], ts, 0, name="hgrn_sample")
    mem_p, mem_s = mem_attn(z_p, z_s, 4 * HG_WIDTH // MEM_WIDTH, 0, "a")
    x1_p, x1_s = _out_proj(mix_p, mix_s, mem_p, mem_s, bf(w_out_a[0]), x0_p, x0_s, name="out_proj_a")

    k_p, v_p, kb_p, vb_p = _shared_kv(x1_p, g_kv, bf(w_kv), True, name="shared_kv_prompt")
    k_s, v_s = _shared_kv(x1_s, g_kv, bf(w_kv), False, name="shared_kv_sample")
    kmean = _block_mean(k_p, tp // MB_BLOCK)

    blk = MB_BLOCK
    c = jnp.arange(2 * blk, dtype=jnp.int32)
    dist_p = (jnp.arange(NEAR_BLOCKS, dtype=jnp.int32)[:, None] * blk + blk - c[None, :]).reshape(-1)
    brows = _bias_table(rel_bias, dist_p)[:MB_HEADS].reshape(MB_HEADS, NEAR_BLOCKS, 2 * blk)
    near_pages = min(NEAR_PAGES, n_pages)
    n_keys = (near_pages + 1) * PAGE_SIZE
    key_x = jnp.arange(n_keys, dtype=jnp.int32)
    dist_s = (near_pages * PAGE_SIZE - key_x[None, :] + jnp.arange(ts, dtype=jnp.int32)[:, None]).reshape(-1)
    bs_tab = _bias_table(rel_bias, dist_s)[:MB_HEADS].reshape(MB_HEADS, ts, n_keys)
    bs_tab = jnp.pad(bs_tab.transpose(2, 0, 1).reshape(n_keys, MB_HEADS * ts), ((0, 0), (0, LANES - MB_HEADS * ts)))
    cfar = jnp.pad(jnp.repeat(brows[:, NEAR_BLOCKS - 1, 0], ts), (0, LANES - MB_HEADS * ts)).reshape(1, LANES)

    zb_p, zb_s = in_proj(x1_p, x1_s, g_norm[1], w_in_b[0], "b")
    mix_p = _moba_prompt(zb_p, kb_p, vb_p, kmean, brows, tp, name="moba_prompt")
    mix_s = _moba_sample(zb_s, k_s, v_s, cache_k, cache_v, page_table, bs_tab, cfar, ts, name="moba_sample")
    mem_p, mem_s = mem_attn(zb_p, zb_s, 2 * MB_WIDTH // MEM_WIDTH, 1, "b")
    y_p, y_s = _out_proj(mix_p, mix_s, mem_p, mem_s, bf(w_out_b[0]), x1_p, x1_s, g_final, name="out_proj_b")

    heads = lambda a, b_, t: a.reshape(MB_HEADS, b_, t, HEAD_DIM).transpose(1, 2, 0, 3)
    memh = lambda lo: jnp.stack([kv[:, lo:lo + MEM_WIDTH] for kv in mem_kv]).reshape(2, bp, mem_len, MEM_HEADS, HEAD_DIM)
    return (y_p.reshape(bp, tp, d), y_s.reshape(bs_, ts, d),
            heads(k_p, bp, tp), heads(v_p, bp, tp), heads(k_s, bs_, ts), heads(v_s, bs_, ts),
            s_prompt[None, None].astype(state_hgrn.dtype), s_sample[None].astype(state_hgrn.dtype),
            memh(0), memh(MEM_WIDTH))
```

```python
import functools
import math

import jax
import jax.numpy as jnp
from jax import lax
from jax.experimental import pallas as pl
from jax.experimental.pallas import tpu as pltpu

F32 = jnp.float32
BF16 = jnp.bfloat16

HEAD_DIM = 128
HG_HEADS = 12
MB_HEADS = 12
MEM_HEADS = 4
MB_BLOCK = 256
MB_TOPK = 3
PAGE_SIZE = 128
N_BUCKETS = 32
MAX_DISTANCE = 1024
EPS = 1e-6
HG_WIDTH = HG_HEADS * HEAD_DIM
MB_WIDTH = MB_HEADS * HEAD_DIM
MEM_WIDTH = MEM_HEADS * HEAD_DIM

SUBLANES = 8
LANES = 128
HG_CHUNK = 128
HG_BLOCK_CHUNKS = 16
HG_SAMPLE_UNROLL = 4
MEM_SAMPLE_SEQS = 2
NEAR_BLOCKS = 5
NEAR_PAGES = (NEAR_BLOCKS - 1) * MB_BLOCK // PAGE_SIZE
MAX_PAGES_PER_STEP = 16
VMEM_LIMIT = 48 * 1024 * 1024

NT_DIMS = (((1,), (1,)), ((), ()))
TN_DIMS = (((0,), (0,)), ((), ()))


def _dot(a, b):
    return jnp.dot(a, b, preferred_element_type=F32)


def _dot_nt(a, b):
    return lax.dot_general(a, b, NT_DIMS, preferred_element_type=F32)


def _dot_tn(a, b):
    return lax.dot_general(a, b, TN_DIMS, preferred_element_type=F32)


def _sigmoid(x):
    return 1.0 / (1.0 + jnp.exp(-x))


def _silu(x):
    return x * _sigmoid(x)


def _split2(x):
    hi = x.astype(BF16)
    lo = (x - hi.astype(F32)).astype(BF16)
    return hi, lo


def _split3(x):
    hi = x.astype(BF16)
    r = x - hi.astype(F32)
    mid = r.astype(BF16)
    lo = (r - mid.astype(F32)).astype(BF16)
    return hi, mid, lo


def _row_tile(m, cap, mult):
    best = None
    for t in range(mult, min(m, cap) + 1, mult):
        if m % t == 0:
            best = t
    assert best is not None, (m, cap, mult)
    return best


def _params(*sem, vmem_limit=VMEM_LIMIT):
    return pltpu.CompilerParams(dimension_semantics=sem, vmem_limit_bytes=vmem_limit)


def _pad_rows(x, rows):
    if rows == x.shape[0]:
        return x
    return jnp.concatenate([x, jnp.zeros((rows - x.shape[0], x.shape[1]), x.dtype)], axis=0)


def _normalize_rows(x_ref, g_ref, xn_ref):
    rows = x_ref.shape[0]
    step = 128 if rows % 128 == 0 else rows
    for r in range(0, rows, step):
        x = x_ref[r:r + step, :]
        ms = jnp.mean(x * x, axis=-1, keepdims=True)
        xn_ref[r:r + step, :] = (x * lax.rsqrt(ms + EPS) * g_ref[...]).astype(BF16)


def _norm_matmul_kernel(x_ref, g_ref, w_ref, o_ref, xn_ref):
    @pl.when(pl.program_id(1) == 0)
    def _():
        _normalize_rows(x_ref, g_ref, xn_ref)

    o_ref[...] = _dot(xn_ref[...], w_ref[...])


def _norm_matmul(x, g, w, *, name):
    m, d = x.shape
    n = w.shape[1]
    tm = _row_tile(m, 1024, 256)
    tn = _row_tile(n, 1024, 256)
    return pl.pallas_call(
        _norm_matmul_kernel,
        grid=(m // tm, n // tn),
        in_specs=[pl.BlockSpec((tm, d), lambda i, j: (i, 0)),
                  pl.BlockSpec((1, d), lambda i, j: (0, 0)),
                  pl.BlockSpec((d, tn), lambda i, j: (0, j))],
        out_specs=pl.BlockSpec((tm, tn), lambda i, j: (i, j)),
        out_shape=jax.ShapeDtypeStruct((m, n), F32),
        scratch_shapes=[pltpu.VMEM((tm, d), BF16)],
        compiler_params=_params("parallel", "arbitrary"),
        name=name,
    )(x, g.reshape(1, d), w)


def _shared_kv_kernel(x_ref, g_ref, w_ref, *rest):
    *o_refs, xn_ref = rest
    j = pl.program_id(1)

    @pl.when(j == 0)
    def _():
        _normalize_rows(x_ref, g_ref, xn_ref)

    y = _dot(xn_ref[...], w_ref[...])

    def write(refs):
        for h in range(MB_HEADS):
            for ref in refs:
                ref[h] = y[:, h * HEAD_DIM:(h + 1) * HEAD_DIM].astype(ref.dtype)

    @pl.when(j == 0)
    def _():
        write(o_refs[0::2])

    @pl.when(j == 1)
    def _():
        write(o_refs[1::2])


def _shared_kv(x, g, w, with_bf16, *, name):
    rows, d = x.shape
    tm = _row_tile(rows, 512, 256)
    assert w.shape[1] == 2 * MB_WIDTH
    shape = (MB_HEADS, rows, HEAD_DIM)
    spec = pl.BlockSpec((MB_HEADS, tm, HEAD_DIM), lambda i, j: (0, i, 0))
    dtypes = [F32, F32] + ([BF16, BF16] if with_bf16 else [])
    return pl.pallas_call(
        _shared_kv_kernel,
        grid=(rows // tm, 2),
        in_specs=[pl.BlockSpec((tm, d), lambda i, j: (i, 0)),
                  pl.BlockSpec((1, d), lambda i, j: (0, 0)),
                  pl.BlockSpec((d, MB_WIDTH), lambda i, j: (0, j))],
        out_specs=[spec] * len(dtypes),
        out_shape=[jax.ShapeDtypeStruct(shape, t) for t in dtypes],
        scratch_shapes=[pltpu.VMEM((tm, d), BF16)],
        compiler_params=_params("parallel", "arbitrary"),
        name=name,
    )(x, g.reshape(1, d), w)


def _out_proj_kernel(ap_ref, as_ref, bp_ref, bs_ref, wa_ref, wb_ref, xp_ref, xs_ref, *rest, prompt_tiles,
                     final_norm):
    if final_norm:
        g_ref, yp_ref, ys_ref = rest
    else:
        yp_ref, ys_ref = rest
    i = pl.program_id(0)

    def run(a_ref, b_ref, x_ref, o_ref):
        y = _dot(a_ref[...].astype(BF16), wa_ref[...]) + _dot(b_ref[...].astype(BF16), wb_ref[...])
        y = x_ref[...] + y
        if final_norm:
            ms = jnp.mean(y * y, axis=-1, keepdims=True)
            y = y * lax.rsqrt(ms + EPS) * g_ref[...]
        o_ref[...] = y

    @pl.when(i < prompt_tiles)
    def _():
        run(ap_ref, bp_ref, xp_ref, yp_ref)

    @pl.when(i >= prompt_tiles)
    def _():
        run(as_ref, bs_ref, xs_ref, ys_ref)


def _out_proj(main_p, main_s, mem_p, mem_s, w, x_p, x_s, g_final=None, *, name):
    tp, d = x_p.shape
    rows_s = x_s.shape[0]
    wm, wmem = main_p.shape[1], mem_p.shape[1]
    assert wm % wmem == 0
    tm = _row_tile(math.gcd(tp, rows_s), 256, 8)
    pt = tp // tm
    p_map = lambda i: (jnp.minimum(i, pt - 1), 0)
    s_map = lambda i: (jnp.maximum(i - pt, 0), 0)
    in_specs = [pl.BlockSpec((tm, wm), p_map), pl.BlockSpec((tm, wm), s_map),
                pl.BlockSpec((tm, wmem), p_map), pl.BlockSpec((tm, wmem), s_map),
                pl.BlockSpec((wm, d), lambda i: (0, 0)),
                pl.BlockSpec((wmem, d), lambda i: (wm // wmem, 0)),
                pl.BlockSpec((tm, d), p_map), pl.BlockSpec((tm, d), s_map)]
    args = [main_p, main_s, mem_p, mem_s, w, w, x_p, x_s]
    if g_final is not None:
        in_specs.append(pl.BlockSpec((1, d), lambda i: (0, 0)))
        args.append(g_final.reshape(1, d))
    return pl.pallas_call(
        functools.partial(_out_proj_kernel, prompt_tiles=pt, final_norm=g_final is not None),
        grid=((tp + rows_s) // tm,),
        in_specs=in_specs,
        out_specs=[pl.BlockSpec((tm, d), p_map), pl.BlockSpec((tm, d), s_map)],
        out_shape=[jax.ShapeDtypeStruct((tp, d), F32), jax.ShapeDtypeStruct((rows_s, d), F32)],
        compiler_params=_params("arbitrary"),
        name=name,
    )(*args)


def _mem_attn_heads(q_ref, gate_ref, o_ref, rows, kv_of_head):
    scale = HEAD_DIM ** -0.5
    for h in range(MEM_HEADS):
        cols = slice(h * HEAD_DIM, (h + 1) * HEAD_DIM)
        k, v = kv_of_head(h)
        s = _dot_nt(q_ref[rows, cols].astype(BF16), k.astype(BF16)) * scale
        e = jnp.exp(s - jnp.max(s, axis=-1, keepdims=True))
        o = _dot(e.astype(BF16), v.astype(BF16)) / jnp.sum(e, axis=-1, keepdims=True)
        o_ref[rows, cols] = (o * _silu(gate_ref[rows, cols])).astype(o_ref.dtype)


def _mem_attn_prompt_kernel(q_ref, gate_ref, mkv_ref, o_ref):
    def kv_of_head(h):
        return (mkv_ref[:, h * HEAD_DIM:(h + 1) * HEAD_DIM],
                mkv_ref[:, MEM_WIDTH + h * HEAD_DIM:MEM_WIDTH + (h + 1) * HEAD_DIM])

    _mem_attn_heads(q_ref, gate_ref, o_ref, slice(None), kv_of_head)


def _mem_attn_sample_kernel(q_ref, gate_ref, mk_ref, mv_ref, o_ref, *, ts):
    mem_len = mk_ref.shape[1] // MEM_HEADS
    for s in range(mk_ref.shape[0]):
        def kv_of_head(h, s=s):
            rows = pl.ds(h, mem_len, stride=MEM_HEADS)
            return mk_ref[s, rows, :], mv_ref[s, rows, :]

        _mem_attn_heads(q_ref, gate_ref, o_ref, slice(s * ts, (s + 1) * ts), kv_of_head)


def _mem_attn_prompt(z, col_q, mkv, *, name):
    rows = z.shape[0]
    mem_len = mkv.shape[0]
    tm = _row_tile(rows, 512, 8)
    return pl.pallas_call(
        _mem_attn_prompt_kernel,
        grid=(rows // tm,),
        in_specs=[pl.BlockSpec((tm, MEM_WIDTH), lambda t: (t, col_q)),
                  pl.BlockSpec((tm, MEM_WIDTH), lambda t: (t, col_q + 1)),
                  pl.BlockSpec((mem_len, 2 * MEM_WIDTH), lambda t: (0, 0))],
        out_specs=pl.BlockSpec((tm, MEM_WIDTH), lambda t: (t, 0)),
        out_shape=jax.ShapeDtypeStruct((rows, MEM_WIDTH), BF16),
        compiler_params=_params("parallel"),
        name=name,
    )(z, z, mkv)


def _mem_attn_sample(z, col_q, cache_mk, cache_mv, layer, ts, *, name):
    _, nb, mem_len, heads, hd = cache_mk.shape
    assert heads == MEM_HEADS and hd == HEAD_DIM
    view = lambda c: c.reshape(c.shape[0], nb, mem_len * heads, hd)
    ns = math.gcd(nb, MEM_SAMPLE_SEQS)
    cache_spec = pl.BlockSpec((None, ns, mem_len * heads, hd), lambda b: (layer, b, 0, 0))
    return pl.pallas_call(
        functools.partial(_mem_attn_sample_kernel, ts=ts),
        grid=(nb // ns,),
        in_specs=[pl.BlockSpec((ns * ts, MEM_WIDTH), lambda b: (b, col_q)),
                  pl.BlockSpec((ns * ts, MEM_WIDTH), lambda b: (b, col_q + 1)),
                  cache_spec, cache_spec],
        out_specs=pl.BlockSpec((ns * ts, MEM_WIDTH), lambda b: (b, 0)),
        out_shape=jax.ShapeDtypeStruct((nb * ts, MEM_WIDTH), F32),
        compiler_params=_params("parallel"),
        name=name,
    )(z, z, view(cache_mk), view(cache_mv))


def _cumsum_rows(x):
    n = x.shape[0]
    row = lax.broadcasted_iota(jnp.int32, x.shape, 0)
    s = 1
    while s < n:
        x = x + jnp.where(row >= s, pltpu.roll(x, s, 0), 0.0)
        s *= 2
    return x


def _lower_bound(lbp, layer):
    e = jnp.exp(lbp - jnp.max(lbp, axis=0, keepdims=True))
    return jnp.sum(e[:layer + 1], axis=0, keepdims=True) / jnp.sum(e, axis=0, keepdims=True)


def _hgrn_gates(hq, hf, lb):
    q = _silu(hq)
    f = lb + (1.0 - lb) * _sigmoid(hf)
    return q, 1.0 - f, jnp.log2(f)


def _hgrn_finish(o, hg, go):
    ms = jnp.mean(o * o, axis=-1, keepdims=True)
    return o * lax.rsqrt(ms + EPS) * go * _silu(hg)


def _diag_tiles(q, k, b, width):
    sub = SUBLANES
    lane = lax.broadcasted_iota(jnp.int32, (sub, width), 1)
    trow = lax.broadcasted_iota(jnp.int32, (sub, width), 0)
    tiles = []
    for r0 in range(0, q.shape[0], sub):
        q8, k8, b8 = q[r0:r0 + sub], k[r0:r0 + sub], b[r0:r0 + sub]
        tile = jnp.zeros((sub, width), F32)
        for s in range(sub):
            e = jnp.exp2(b8 - b8[s:s + 1])
            a_col = jnp.sum(q8 * e * k8[s:s + 1], axis=1, keepdims=True)
            tile = jnp.where(lane == r0 + s, a_col, tile)
        tiles.append(jnp.where(trow >= lane - r0, tile, 0.0))
    return jnp.concatenate(tiles, axis=0)


def _level_ref(b, level):
    parts = []
    for start in range(0, b.shape[0], 2 * level):
        r = start + level - 1
        parts.append(jnp.broadcast_to(b[r:r + 1, :], (2 * level, b.shape[1])))
    return jnp.concatenate(parts, axis=0)


def _level_masks(c):
    row = lax.broadcasted_iota(jnp.int32, (c, c), 0)
    col = lax.broadcasted_iota(jnp.int32, (c, c), 1)
    masks = []
    level = SUBLANES
    while level < c:
        sh = level.bit_length() - 1
        same = (row >> (sh + 1)) == (col >> (sh + 1))
        masks.append((level, same & (((row >> sh) & 1) == 1) & (((col >> sh) & 1) == 0)))
        level *= 2
    return masks


def _hgrn_chunk(q, k, g, v, st, masks):
    c = HG_CHUNK
    b = _cumsum_rows(g)
    a = _diag_tiles(q, k, b, c)
    for level, mask in masks:
        e = jnp.exp2(-jnp.abs(b - _level_ref(b, level)))
        a = jnp.where(mask, _dot_nt((q * e).astype(BF16), (k * e).astype(BF16)), a)
    b_end = b[c - 1:c, :]
    vb = v.astype(BF16)
    o = _dot(a.astype(BF16), vb) + _dot_nt((q * jnp.exp2(b)).astype(BF16), st.astype(BF16))
    st_new = st * jnp.exp2(b_end) + _dot_tn(vb, (k * jnp.exp2(b_end - b)).astype(BF16))
    return o, st_new


def _hgrn_prompt_kernel(hq_ref, hf_ref, hi_ref, hg_ref, lbp_ref, go_ref, mix_ref, sout_ref, st_ref, *, layer):
    n = pl.program_id(1)

    @pl.when(n == 0)
    def _():
        st_ref[...] = jnp.zeros_like(st_ref)

    lb = _lower_bound(lbp_ref[...], layer)
    go = go_ref[...]
    masks = _level_masks(HG_CHUNK)
    for c in range(HG_BLOCK_CHUNKS):
        rows = slice(c * HG_CHUNK, (c + 1) * HG_CHUNK)
        q, k, g = _hgrn_gates(hq_ref[rows, :], hf_ref[rows, :], lb)
        o, st_new = _hgrn_chunk(q, k, g, hi_ref[rows, :], st_ref[...], masks)
        st_ref[...] = st_new
        mix_ref[rows, :] = _hgrn_finish(o, hg_ref[rows, :], go).astype(mix_ref.dtype)

    @pl.when(n == pl.num_programs(1) - 1)
    def _():
        sout_ref[0] = st_ref[...].T


def _hgrn_prompt(z, hg_lb, g_o, tp, layer, *, name):
    rb = HG_CHUNK * HG_BLOCK_CHUNKS
    assert tp % rb == 0
    h_ = HG_HEADS
    n_lb = hg_lb.shape[0]
    zspec = lambda off: pl.BlockSpec((rb, HEAD_DIM), lambda h, n: (n, off + h))
    return pl.pallas_call(
        functools.partial(_hgrn_prompt_kernel, layer=layer),
        grid=(h_, tp // rb),
        in_specs=[zspec(0), zspec(h_), zspec(2 * h_), zspec(3 * h_),
                  pl.BlockSpec((n_lb, HEAD_DIM), lambda h, n: (0, h)),
                  pl.BlockSpec((1, HEAD_DIM), lambda h, n: (0, h))],
        out_specs=[pl.BlockSpec((rb, HEAD_DIM), lambda h, n: (n, h)),
                   pl.BlockSpec((1, HEAD_DIM, HEAD_DIM), lambda h, n: (h, 0, 0))],
        out_shape=[jax.ShapeDtypeStruct((tp, HG_WIDTH), BF16),
                   jax.ShapeDtypeStruct((h_, HEAD_DIM, HEAD_DIM), F32)],
        scratch_shapes=[pltpu.VMEM((HEAD_DIM, HEAD_DIM), F32)],
        compiler_params=_params("parallel", "arbitrary"),
        name=name,
    )(z, z, z, z, hg_lb, g_o.reshape(1, HG_WIDTH))


def _hgrn_sample_kernel(hq_ref, hf_ref, hi_ref, hg_ref, lbp_ref, go_ref, s0_ref, mix_ref, sout_ref, *, layer, ts, nb):
    lb = _lower_bound(lbp_ref[...], layer)
    go = go_ref[...]

    def one_sequence(bi):
        rows = pl.ds(pl.multiple_of(bi * ts, ts), ts)
        q, k, g = _hgrn_gates(hq_ref[rows, :], hf_ref[rows, :], lb)
        b = _cumsum_rows(g)
        b_end = b[ts - 1:ts, :]
        st = s0_ref[bi, 0].T
        a = _diag_tiles(q, k, b, LANES)
        vb = _pad_rows(hi_ref[rows, :], LANES).astype(BF16)
        kh = _pad_rows(k * jnp.exp2(b_end - b), LANES).astype(BF16)
        o = _dot(a.astype(BF16), vb) + _dot_nt((q * jnp.exp2(b)).astype(BF16), st.astype(BF16))
        st_new = st * jnp.exp2(b_end) + _dot_tn(vb, kh)
        sout_ref[bi, 0] = st_new.T
        mix_ref[rows, :] = _hgrn_finish(o, hg_ref[rows, :], go)

    def body(p, carry):
        for e in range(HG_SAMPLE_UNROLL):
            one_sequence(HG_SAMPLE_UNROLL * p + e)
        return carry

    lax.fori_loop(0, nb // HG_SAMPLE_UNROLL, body, 0)


def _hgrn_sample(z, hg_lb, g_o, s0, ts, layer, *, name):
    nb = s0.shape[0]
    rows = nb * ts
    assert rows == z.shape[0] and ts == SUBLANES and nb % HG_SAMPLE_UNROLL == 0
    h_ = HG_HEADS
    n_lb = hg_lb.shape[0]
    zspec = lambda off: pl.BlockSpec((rows, HEAD_DIM), lambda h: (0, off + h))
    return pl.pallas_call(
        functools.partial(_hgrn_sample_kernel, layer=layer, ts=ts, nb=nb),
        grid=(h_,),
        in_specs=[zspec(0), zspec(h_), zspec(2 * h_), zspec(3 * h_),
                  pl.BlockSpec((n_lb, HEAD_DIM), lambda h: (0, h)),
                  pl.BlockSpec((1, HEAD_DIM), lambda h: (0, h)),
                  pl.BlockSpec((nb, 1, HEAD_DIM, HEAD_DIM), lambda h: (0, h, 0, 0))],
        out_specs=[pl.BlockSpec((rows, HEAD_DIM), lambda h: (0, h)),
                   pl.BlockSpec((nb, 1, HEAD_DIM, HEAD_DIM), lambda h: (0, h, 0, 0))],
        out_shape=[jax.ShapeDtypeStruct((rows, HG_WIDTH), F32),
                   jax.ShapeDtypeStruct(s0.shape, F32)],
        compiler_params=_params("parallel"),
        name=name,
    )(z, z, z, z, hg_lb, g_o.reshape(1, HG_WIDTH), s0)


def _t5_bucket(dist):
    exact = N_BUCKETS // 2
    d = jnp.maximum(dist, exact).astype(F32)
    large = exact + (jnp.log(d / exact) / math.log(MAX_DISTANCE / exact) * (N_BUCKETS - exact)).astype(jnp.int32)
    return jnp.where(dist < exact, dist, jnp.minimum(large, N_BUCKETS - 1))


def _bias_table_kernel(rb_ref, bucket_ref, o_ref):
    hi, mid, lo = _split3(rb_ref[...])
    row = lax.broadcasted_iota(jnp.int32, (LANES, bucket_ref.shape[1]), 0)
    oh = jnp.where(row == bucket_ref[...], 1.0, 0.0).astype(BF16)
    o_ref[...] = _dot(hi, oh) + _dot(mid, oh) + _dot(lo, oh)


def _bias_table(rel_bias, dist):
    n = dist.shape[0]
    rb = jnp.zeros((16, LANES), F32).at[:MB_HEADS, :N_BUCKETS].set(rel_bias.T)
    bucket = _t5_bucket(jnp.maximum(dist, 0).astype(jnp.int32)).reshape(1, n)
    return pl.pallas_call(
        _bias_table_kernel,
        out_shape=jax.ShapeDtypeStruct((16, n), F32),
        name="bias_table",
    )(rb, bucket)


def _block_mean_kernel(k_ref, o_ref):
    for n in range(o_ref.shape[0]):
        rows = slice(n * MB_BLOCK, (n + 1) * MB_BLOCK)
        o_ref[n:n + 1, :] = jnp.sum(k_ref[rows, :], axis=0, keepdims=True) * (1.0 / MB_BLOCK)


def _block_mean(k, n_blocks):
    return pl.pallas_call(
        _block_mean_kernel,
        grid=(MB_HEADS,),
        in_specs=[pl.BlockSpec((None, n_blocks * MB_BLOCK, HEAD_DIM), lambda h: (h, 0, 0))],
        out_specs=pl.BlockSpec((None, n_blocks, HEAD_DIM), lambda h: (h, 0, 0)),
        out_shape=jax.ShapeDtypeStruct((MB_HEADS, n_blocks, HEAD_DIM), F32),
        compiler_params=_params("parallel"),
        name="block_mean",
    )(k)


def _select_topk(gate, axis, n_blocks):
    idx = lax.broadcasted_iota(jnp.int32, gate.shape, axis)
    sel = jnp.zeros(gate.shape, F32)
    for _ in range(min(MB_TOPK, n_blocks)):
        mx = jnp.max(gate, axis=axis, keepdims=True)
        first = jnp.min(jnp.where(gate == mx, idx, n_blocks), axis=axis, keepdims=True)
        pick = idx == first
        sel = jnp.where(pick & (mx > -jnp.inf), 1.0, sel)
        gate = jnp.where(pick, -jnp.inf, gate)
    return sel


MASK_BIG = 2.0 ** 100
DUMMY_LANE = 125
FAR_LANES = (126, 127)
MOBA_GROUP_LOG2 = 3
MOBA_GROUP = 1 << MOBA_GROUP_LOG2
MOBA_HEADS_PER_STEP = 3
MOBA_VMEM_LIMIT = 56 * 1024 * 1024


def _moba_prompt_kernel(q_ref, gate_ref, k_ref, v_ref, km_ref, brow_ref, o_ref, bias_ref, s_ref, mx_ref, acc_ref,
                        *, n_blocks):
    i = pl.program_id(1)
    blk = MB_BLOCK
    near = NEAR_BLOCKS
    heads = range(k_ref.shape[0])
    inv_scale = HEAD_DIM ** 0.5
    exp2_scale = HEAD_DIM ** -0.5 * math.log2(math.e)
    filler = n_blocks + 1

    @pl.when(i == 0)
    def _build_bias():
        rowi = lax.broadcasted_iota(jnp.int32, (blk, 2 * blk), 0)
        for hh in heads:
            for d in range(near):
                x = jnp.broadcast_to(brow_ref[hh, d:d + 1, :], (blk, 2 * blk))
                for bit in range(blk.bit_length() - 1):
                    x = jnp.where(((rowi >> bit) & 1) == 1, pltpu.roll(x, 1 << bit, 1), x)
                bias_ref[hh, d] = x[:, blk:] * inv_scale
            s_ref[hh, filler] = jnp.full((blk, blk), -MASK_BIG, F32)

    tq = lax.broadcasted_iota(jnp.int32, (blk, blk), 0)
    tk = lax.broadcasted_iota(jnp.int32, (blk, blk), 1)
    lane = lax.broadcasted_iota(jnp.int32, (blk, LANES), 1)
    lane_r = lax.broadcasted_iota(jnp.int32, (1, LANES), 1)
    fold = lambda t: jnp.maximum(t[:, :LANES], t[:, LANES:])
    qa, far_row = [], []
    for hh in heads:
        qh, ql = _split2(q_ref[:, hh * HEAD_DIM:(hh + 1) * HEAD_DIM])

        own = _dot_nt(qh, k_ref[hh, pl.ds(pl.multiple_of(i * blk, blk), blk), :]) + bias_ref[hh, 0]
        own = jnp.where(tq >= tk, own, -MASK_BIG)
        s_ref[hh, i] = own
        mx_ref[hh] = fold(own)

        kmh, kml = _split2(_pad_rows(km_ref[hh], -(-n_blocks // 16) * 16))
        gate_t = _dot_nt(kmh, qh) + _dot_nt(kml, qh) + _dot_nt(kmh, ql)
        blk_id = lax.broadcasted_iota(jnp.int32, gate_t.shape, 0)
        sel_t = _select_topk(jnp.where(blk_id < i, gate_t, -jnp.inf), 0, n_blocks)
        sel = _pad_rows(sel_t, LANES).T

        qa.append(jnp.concatenate([qh, jnp.where(lane >= FAR_LANES[0], 1.0, sel - 1.0).astype(BF16)], axis=1))
        c_far = brow_ref[hh, near - 1:near, 0:1] * inv_scale
        c_hi = c_far.astype(BF16).astype(F32)
        far_row.append(jnp.where(lane_r == FAR_LANES[0], c_hi, jnp.where(lane_r == FAR_LANES[1], c_far - c_hi, 0.0)))

    def scores(hh, j, big_lane, row_vals):
        rows = pl.ds(pl.multiple_of(jnp.minimum(j, i) * blk, blk), blk)
        right = jnp.broadcast_to(jnp.where(lane_r == big_lane, MASK_BIG, row_vals), (blk, LANES)).astype(BF16)
        return _dot_nt(qa[hh], jnp.concatenate([k_ref[hh, rows, :], right], axis=1))

    first_near = jnp.maximum(i - (near - 1), 0)

    def grouped(count, blocks_fn):
        n_full = count >> MOBA_GROUP_LOG2
        rem = count & (MOBA_GROUP - 1)
        half = MOBA_GROUP // 2
        for group, lo, hi in ((MOBA_GROUP, 0, n_full + jnp.where(rem > half, 1, 0)),
                              (half, 2 * n_full, 2 * n_full + jnp.where((rem > 0) & (rem <= half), 1, 0))):
            def body(p, carry, group=group):
                blocks_fn(group * p, group)
                return carry

            lax.fori_loop(lo, hi, body, 0)

    def far_blocks(first, group):
        mx = [mx_ref[hh] for hh in heads]
        for e in range(group):
            j = first + e
            is_far = j < first_near
            for hh in heads:
                raw = scores(hh, j, jnp.where(is_far, j, DUMMY_LANE), far_row[hh])
                s_ref[hh, jnp.where(is_far, j, n_blocks)] = raw
                mx[hh] = jnp.maximum(mx[hh], fold(raw))
        for hh in heads:
            mx_ref[hh] = mx[hh]

    grouped(first_near, far_blocks)

    mx = [mx_ref[hh] for hh in heads]
    for dlt in range(1, near):
        j = i - dlt
        jc = jnp.maximum(j, 0)
        for hh in heads:
            raw = scores(hh, jc, jnp.where(j >= 0, jc, DUMMY_LANE), 0.0) + bias_ref[hh, dlt]
            s_ref[hh, jnp.where(j >= 0, jc, n_blocks)] = raw
            mx[hh] = jnp.maximum(mx[hh], fold(raw))
    for hh in heads:
        mx_ref[hh] = jnp.broadcast_to(jnp.max(mx[hh], axis=1, keepdims=True), (blk, LANES))

    ones = jnp.ones((blk, LANES), BF16)

    def pv_part(hh, first, count):
        top = jnp.concatenate([mx_ref[hh]] * 2, axis=1)
        probs, vals = [], []
        for e in range(count):
            j = first + e
            tile = s_ref[hh, jnp.where(j <= i, j, filler)]
            probs.append(jnp.exp2((tile - top) * exp2_scale).astype(BF16))
            rows = pl.ds(pl.multiple_of(jnp.minimum(j, i) * blk, blk), blk)
            vals.append(jnp.concatenate([v_ref[hh, rows, :], ones], axis=1))
        return _dot(jnp.concatenate(probs, axis=1), jnp.concatenate(vals, axis=0))

    def pv_blocks(first, group):
        half = group // 2
        for hh in heads:
            acc_ref[hh] = acc_ref[hh] + (pv_part(hh, first, half) + pv_part(hh, first + half, half))

    acc_ref[...] = jnp.zeros_like(acc_ref)
    grouped(i + 1, pv_blocks)
    for hh in heads:
        cols = slice(hh * HEAD_DIM, (hh + 1) * HEAD_DIM)
        acc = acc_ref[hh]
        o_ref[:, cols] = (acc[:, :HEAD_DIM] / acc[:, HEAD_DIM:] * _silu(gate_ref[:, cols])).astype(o_ref.dtype)


def _moba_prompt(z, kb, vb, kmean, brows, tp, *, name):
    nq = tp // MB_BLOCK
    hp = MOBA_HEADS_PER_STEP
    steps_h = MB_HEADS // hp
    assert nq <= DUMMY_LANE and MB_HEADS % hp == 0
    head_spec = lambda rows, **kw: pl.BlockSpec((hp, rows, HEAD_DIM), lambda h, i: (h, 0, 0), **kw)
    once = dict(pipeline_mode=pl.Buffered(1))
    return pl.pallas_call(
        functools.partial(_moba_prompt_kernel, n_blocks=nq),
        grid=(steps_h, nq),
        in_specs=[pl.BlockSpec((MB_BLOCK, hp * HEAD_DIM), lambda h, i: (i, h)),
                  pl.BlockSpec((MB_BLOCK, hp * HEAD_DIM), lambda h, i: (i, steps_h + h)),
                  head_spec(tp, **once), head_spec(tp, **once), head_spec(nq),
                  pl.BlockSpec((hp, NEAR_BLOCKS, 2 * MB_BLOCK), lambda h, i: (h, 0, 0))],
        out_specs=pl.BlockSpec((MB_BLOCK, hp * HEAD_DIM), lambda h, i: (i, h)),
        out_shape=jax.ShapeDtypeStruct((tp, MB_WIDTH), BF16),
        scratch_shapes=[pltpu.VMEM((hp, NEAR_BLOCKS, MB_BLOCK, MB_BLOCK), F32),
                        pltpu.VMEM((hp, nq + 2, MB_BLOCK, MB_BLOCK), F32),
                        pltpu.VMEM((hp, MB_BLOCK, LANES), F32),
                        pltpu.VMEM((hp, MB_BLOCK, MB_BLOCK), F32)],
        compiler_params=_params("parallel", "arbitrary", vmem_limit=MOBA_VMEM_LIMIT),
        name=name,
    )(z, z, kb, vb, kmean, brows)


def _cat_heads(ref, *lead):
    return jnp.concatenate([ref[lead + (h,)] for h in range(MB_HEADS)], axis=1)


def _sample_scores_kernel(pt_ref, q_ref, knew_ref, *rest, n_pages, pps, ts):
    del pt_ref
    kc = rest[:pps]
    bs_ref, cfar_ref, p_ref, l_ref, wq_ref, wql_ref, s_ref, km_ref, sel_ref = rest[pps:]
    g = pl.program_id(1)
    n_steps = n_pages // pps
    n_blocks = n_pages * PAGE_SIZE // MB_BLOCK
    near_pages = min(NEAR_PAGES, n_pages)
    scale = HEAD_DIM ** -0.5
    pg = PAGE_SIZE
    ppb = MB_BLOCK // pg

    @pl.when(g == 0)
    def _start_sequence():
        rep = _pad_rows(jnp.concatenate([q_ref[...]] * MB_HEADS, axis=0), LANES)
        r_h = lax.broadcasted_iota(jnp.int32, rep.shape, 0) // ts
        c_h = lax.broadcasted_iota(jnp.int32, rep.shape, 1) // HEAD_DIM
        hi, lo = _split2(jnp.where(r_h == c_h, rep, 0.0))
        wq_ref[...] = hi
        wql_ref[...] = lo

    pages = [_cat_heads(kc[u], 0) for u in range(pps)]
    st = _dot_nt(jnp.concatenate([kp.astype(BF16) for kp in pages], axis=0), wq_ref[...])
    for u in range(pps):
        p = g * pps + u
        near_idx = jnp.maximum(p - (n_pages - near_pages), 0)
        b_near = bs_ref[pl.ds(pl.multiple_of(near_idx * pg, pg), pg), :]
        bias = jnp.where(p >= n_pages - near_pages, b_near, cfar_ref[...])
        s_ref[pl.ds(pl.multiple_of(p * pg, pg), pg), :] = st[u * pg:(u + 1) * pg] * scale + bias
    for n in range(pps // ppb):
        ksum = sum(jnp.sum(pages[n * ppb + u], axis=0, keepdims=True) for u in range(ppb))
        km_ref[pl.ds(g * (pps // ppb) + n, 1), :] = ksum * (1.0 / MB_BLOCK)

    @pl.when(g == n_steps - 1)
    def _softmax():
        st_new = _dot_nt(_pad_rows(_cat_heads(knew_ref), pg).astype(BF16), wq_ref[...])
        krow = lax.broadcasted_iota(jnp.int32, (pg, LANES), 0)
        qcol = lax.broadcasted_iota(jnp.int32, (pg, LANES), 1)
        valid = (krow < ts) & (krow <= (qcol & (ts - 1)))
        s_cur = jnp.where(valid, st_new * scale + bs_ref[pl.ds(near_pages * pg, pg), :], -jnp.inf)

        kmh, kml = _split2(km_ref[...])
        gate = _dot_nt(kmh, wq_ref[...]) + _dot_nt(kmh, wql_ref[...]) + _dot_nt(kml, wq_ref[...])
        sel_ref[...] = _select_topk(gate, 0, n_blocks)

        def max_body(n, m):
            keep = sel_ref[pl.ds(n, 1), :] > 0.5
            for u in range(ppb):
                tile = s_ref[pl.ds(pl.multiple_of((n * ppb + u) * pg, pg), pg), :]
                m = jnp.maximum(m, jnp.where(keep, tile, -jnp.inf))
            return m

        m = lax.fori_loop(0, n_blocks, max_body, s_cur)
        mrow = jnp.max(m, axis=0, keepdims=True)

        def exp_body(n, l):
            keep = sel_ref[pl.ds(n, 1), :] > 0.5
            for u in range(ppb):
                rows = pl.ds(pl.multiple_of((n * ppb + u) * pg, pg), pg)
                e = jnp.exp(jnp.where(keep, s_ref[rows, :] - mrow, -jnp.inf))
                p_ref[0, rows, :] = e.astype(BF16)
                l = l + e
            return l

        e_cur = jnp.exp(s_cur - mrow)
        p_ref[0, pl.ds(n_pages * pg, pg), :] = e_cur.astype(BF16)
        l = lax.fori_loop(0, n_blocks, exp_body, e_cur)
        l_ref[0] = jnp.sum(l, axis=0, keepdims=True)


def _sample_values_kernel(pt_ref, gate_ref, vnew_ref, p_ref, pcur_ref, l_ref, *rest, n_pages, pps, ts):
    del pt_ref
    vc = rest[:pps]
    o_ref, acc_ref = rest[pps:]
    g = pl.program_id(1)
    pg = PAGE_SIZE

    @pl.when(g == 0)
    def _own_block():
        acc_ref[...] = _dot_tn(pcur_ref[0], _pad_rows(_cat_heads(vnew_ref), pg).astype(BF16))

    vals = jnp.concatenate([_cat_heads(vc[u], 0).astype(BF16) for u in range(pps)], axis=0)
    acc_ref[...] = acc_ref[...] + _dot_tn(p_ref[0], vals)

    @pl.when(g == n_pages // pps - 1)
    def _finish():
        r = lax.broadcasted_iota(jnp.int32, (LANES, LANES), 0)
        c = lax.broadcasted_iota(jnp.int32, (LANES, LANES), 1)
        lcol = jnp.sum(jnp.where(r == c, jnp.broadcast_to(l_ref[0], (LANES, LANES)), 0.0), axis=1, keepdims=True)
        for h in range(MB_HEADS):
            cols = slice(h * HEAD_DIM, (h + 1) * HEAD_DIM)
            o = acc_ref[h * ts:(h + 1) * ts, cols] / lcol[h * ts:(h + 1) * ts, :]
            o_ref[:, cols] = o * _silu(gate_ref[:, cols])


def _moba_sample(z, k_s, v_s, cache_k, cache_v, page_table, bs, cfar, ts, *, name):
    nb, n_pages = page_table.shape
    pps = math.gcd(n_pages, MAX_PAGES_PER_STEP)
    assert (pps * PAGE_SIZE) % MB_BLOCK == 0
    assert ts & (ts - 1) == 0 and MB_HEADS * ts <= LANES and z.shape[0] == nb * ts
    n_steps = n_pages // pps
    n_blocks = n_pages * PAGE_SIZE // MB_BLOCK
    n_keys = (n_pages + 1) * PAGE_SIZE
    kc = cache_k.transpose(0, 2, 1, 3)
    vc = cache_v.transpose(0, 2, 1, 3)
    row_spec = lambda col: pl.BlockSpec((ts, MB_WIDTH), lambda b, g, pt: (b, col))
    new_spec = pl.BlockSpec((MB_HEADS, ts, HEAD_DIM), lambda b, g, pt: (0, b, 0))
    page_spec = lambda u: pl.BlockSpec((1, MB_HEADS, PAGE_SIZE, HEAD_DIM), lambda b, g, pt: (pt[b, g * pps + u], 0, 0, 0))
    const2 = lambda b, g, pt: (0, 0)

    probs, denom = pl.pallas_call(
        functools.partial(_sample_scores_kernel, n_pages=n_pages, pps=pps, ts=ts),
        grid_spec=pltpu.PrefetchScalarGridSpec(
            num_scalar_prefetch=1,
            grid=(nb, n_steps),
            in_specs=[row_spec(0), new_spec] + [page_spec(u) for u in range(pps)]
                     + [pl.BlockSpec(bs.shape, const2), pl.BlockSpec(cfar.shape, const2)],
            out_specs=[pl.BlockSpec((1, n_keys, LANES), lambda b, g, pt: (b, 0, 0)),
                       pl.BlockSpec((1, 1, LANES), lambda b, g, pt: (b, 0, 0))],
            scratch_shapes=[pltpu.VMEM((LANES, MB_WIDTH), BF16),
                            pltpu.VMEM((LANES, MB_WIDTH), BF16),
                            pltpu.VMEM((n_pages * PAGE_SIZE, LANES), F32),
                            pltpu.VMEM((n_blocks, MB_WIDTH), F32),
                            pltpu.VMEM((n_blocks, LANES), F32)],
        ),
        out_shape=[jax.ShapeDtypeStruct((nb, n_keys, LANES), BF16),
                   jax.ShapeDtypeStruct((nb, 1, LANES), F32)],
        compiler_params=_params("arbitrary", "arbitrary"),
        name=name + "_scores",
    )(page_table, z, k_s, *([kc] * pps), bs, cfar)

    return pl.pallas_call(
        functools.partial(_sample_values_kernel, n_pages=n_pages, pps=pps, ts=ts),
        grid_spec=pltpu.PrefetchScalarGridSpec(
            num_scalar_prefetch=1,
            grid=(nb, n_steps),
            in_specs=[row_spec(1), new_spec,
                      pl.BlockSpec((1, pps * PAGE_SIZE, LANES), lambda b, g, pt: (b, g, 0)),
                      pl.BlockSpec((1, PAGE_SIZE, LANES), lambda b, g, pt: (b, n_pages, 0)),
                      pl.BlockSpec((1, 1, LANES), lambda b, g, pt: (b, 0, 0))]
                     + [page_spec(u) for u in range(pps)],
            out_specs=pl.BlockSpec((ts, MB_WIDTH), lambda b, g, pt: (b, 0)),
            scratch_shapes=[pltpu.VMEM((LANES, MB_WIDTH), F32)],
        ),
        out_shape=jax.ShapeDtypeStruct((nb * ts, MB_WIDTH), F32),
        compiler_params=_params("arbitrary", "arbitrary"),
        name=name + "_values",
    )(page_table, z, v_s, probs, probs, denom, *([vc] * pps))


def kernel(x_prompt, x_sample, cache_k, cache_v, cache_mem_k, cache_mem_v, state_hgrn, page_table, mem_prompt,
           g_norm, w_in_a, hg_lb, g_hg_out, w_out_a, w_in_b, w_out_b, g_kv, w_kv, rel_bias, g_mem, w_mem_kv,
           g_final):
    bp, tp, d = x_prompt.shape
    bs_, ts, _ = x_sample.shape
    assert bp == 1 and w_in_a.shape[0] == 1 and w_in_b.shape[0] == 1
    n_pages = page_table.shape[1]
    assert (n_pages * PAGE_SIZE) % MB_BLOCK == 0 and tp % MB_BLOCK == 0
    rows_s = bs_ * ts

    x0_p = x_prompt.reshape(tp, d)
    x0_s = x_sample.reshape(rows_s, d)
    bf = lambda w: w.astype(BF16)

    mem_kv = [_norm_matmul(mem_prompt.reshape(-1, d), g_mem[l], bf(w_mem_kv[l]), name=f"mem_kv_{l}")
              for l in range(2)]
    mem_len = mem_kv[0].shape[0]

    def in_proj(xp, xs, g, w, tag):
        w = bf(w)
        return (_norm_matmul(xp, g, w, name=f"in_proj_{tag}_prompt"), _norm_matmul(xs, g, w, name=f"in_proj_{tag}_sample"))

    def mem_attn(zp, zs, col_q, l, tag):
        return (_mem_attn_prompt(zp, col_q, mem_kv[l], name=f"mem_attn_{tag}_prompt"),
                _mem_attn_sample(zs, col_q, cache_mem_k, cache_mem_v, l, ts, name=f"mem_attn_{tag}_sample"))

    z_p, z_s = in_proj(x0_p, x0_s, g_norm[0], w_in_a[0], "a")
    mix_p, s_prompt = _hgrn_prompt(z_p, hg_lb, g_hg_out[0], tp, 0, name="hgrn_prompt")
    mix_s, s_sample = _hgrn_sample(z_s, hg_lb, g_hg_out[0], state_hgrn[0], ts, 0, name="hgrn_sample")
    mem_p, mem_s = mem_attn(z_p, z_s, 4 * HG_WIDTH // MEM_WIDTH, 0, "a")
    x1_p, x1_s = _out_proj(mix_p, mix_s, mem_p, mem_s, bf(w_out_a[0]), x0_p, x0_s, name="out_proj_a")

    k_p, v_p, kb_p, vb_p = _shared_kv(x1_p, g_kv, bf(w_kv), True, name="shared_kv_prompt")
    k_s, v_s = _shared_kv(x1_s, g_kv, bf(w_kv), False, name="shared_kv_sample")
    kmean = _block_mean(k_p, tp // MB_BLOCK)

    blk = MB_BLOCK
    c = jnp.arange(2 * blk, dtype=jnp.int32)
    dist_p = (jnp.arange(NEAR_BLOCKS, dtype=jnp.int32)[:, None] * blk + blk - c[None, :]).reshape(-1)
    brows = _bias_table(rel_bias, dist_p)[:MB_HEADS].reshape(MB_HEADS, NEAR_BLOCKS, 2 * blk)
    near_pages = min(NEAR_PAGES, n_pages)
    n_keys = (near_pages + 1) * PAGE_SIZE
    key_x = jnp.arange(n_keys, dtype=jnp.int32)
    dist_s = (near_pages * PAGE_SIZE - key_x[None, :] + jnp.arange(ts, dtype=jnp.int32)[:, None]).reshape(-1)
    bs_tab = _bias_table(rel_bias, dist_s)[:MB_HEADS].reshape(MB_HEADS, ts, n_keys)
    bs_tab = jnp.pad(bs_tab.transpose(2, 0, 1).reshape(n_keys, MB_HEADS * ts), ((0, 0), (0, LANES - MB_HEADS * ts)))
    cfar = jnp.pad(jnp.repeat(brows[:, NEAR_BLOCKS - 1, 0], ts), (0, LANES - MB_HEADS * ts)).reshape(1, LANES)

    zb_p, zb_s = in_proj(x1_p, x1_s, g_norm[1], w_in_b[0], "b")
    mix_p = _moba_prompt(zb_p, kb_p, vb_p, kmean, brows, tp, name="moba_prompt")
    mix_s = _moba_sample(zb_s, k_s, v_s, cache_k, cache_v, page_table, bs_tab, cfar, ts, name="moba_sample")
    mem_p, mem_s = mem_attn(zb_p, zb_s, 2 * MB_WIDTH // MEM_WIDTH, 1, "b")
    y_p, y_s = _out_proj(mix_p, mix_s, mem_p, mem_s, bf(w_out_b[0]), x1_p, x1_s, g_final, name="out_proj_b")

    heads = lambda a, b_, t: a.reshape(MB_HEADS, b_, t, HEAD_DIM).transpose(1, 2, 0, 3)
    memh = lambda lo: jnp.stack([kv[:, lo:lo + MEM_WIDTH] for kv in mem_kv]).reshape(2, bp, mem_len, MEM_HEADS, HEAD_DIM)
    return (y_p.reshape(bp, tp, d), y_s.reshape(bs_, ts, d),
            heads(k_p, bp, tp), heads(v_p, bp, tp), heads(k_s, bs_, ts), heads(v_s, bs_, ts),
            s_prompt[None, None].astype(state_hgrn.dtype), s_sample[None].astype(state_hgrn.dtype),
            memh(0), memh(MEM_WIDTH))
```

```python
import functools
import math

import jax
import jax.numpy as jnp
from jax import lax
from jax.experimental import pallas as pl
from jax.experimental.pallas import tpu as pltpu

F32 = jnp.float32
BF16 = jnp.bfloat16

HEAD_DIM = 128
HG_HEADS = 12
MB_HEADS = 12
MEM_HEADS = 4
MB_BLOCK = 256
MB_TOPK = 3
PAGE_SIZE = 128
N_BUCKETS = 32
MAX_DISTANCE = 1024
EPS = 1e-6
HG_WIDTH = HG_HEADS * HEAD_DIM
MB_WIDTH = MB_HEADS * HEAD_DIM
MEM_WIDTH = MEM_HEADS * HEAD_DIM

SUBLANES = 8
LANES = 128
HG_CHUNK = 128
HG_BLOCK_CHUNKS = 16
HG_SAMPLE_UNROLL = 4
MEM_SAMPLE_SEQS = 2
NEAR_BLOCKS = 5
NEAR_PAGES = (NEAR_BLOCKS - 1) * MB_BLOCK // PAGE_SIZE
MAX_PAGES_PER_STEP = 16
VMEM_LIMIT = 48 * 1024 * 1024

NT_DIMS = (((1,), (1,)), ((), ()))
TN_DIMS = (((0,), (0,)), ((), ()))


def _dot(a, b):
    return jnp.dot(a, b, preferred_element_type=F32)


def _dot_nt(a, b):
    return lax.dot_general(a, b, NT_DIMS, preferred_element_type=F32)


def _dot_tn(a, b):
    return lax.dot_general(a, b, TN_DIMS, preferred_element_type=F32)


def _sigmoid(x):
    return 1.0 / (1.0 + jnp.exp(-x))


def _silu(x):
    return x * _sigmoid(x)


def _split2(x):
    hi = x.astype(BF16)
    lo = (x - hi.astype(F32)).astype(BF16)
    return hi, lo


def _split3(x):
    hi = x.astype(BF16)
    r = x - hi.astype(F32)
    mid = r.astype(BF16)
    lo = (r - mid.astype(F32)).astype(BF16)
    return hi, mid, lo


def _row_tile(m, cap, mult):
    best = None
    for t in range(mult, min(m, cap) + 1, mult):
        if m % t == 0:
            best = t
    assert best is not None, (m, cap, mult)
    return best


def _params(*sem, vmem_limit=VMEM_LIMIT):
    return pltpu.CompilerParams(dimension_semantics=sem, vmem_limit_bytes=vmem_limit)


def _pad_rows(x, rows):
    if rows == x.shape[0]:
        return x
    return jnp.concatenate([x, jnp.zeros((rows - x.shape[0], x.shape[1]), x.dtype)], axis=0)


def _normalize_rows(x_ref, g_ref, xn_ref):
    rows = x_ref.shape[0]
    step = 128 if rows % 128 == 0 else rows
    for r in range(0, rows, step):
        x = x_ref[r:r + step, :]
        ms = jnp.mean(x * x, axis=-1, keepdims=True)
        xn_ref[r:r + step, :] = (x * lax.rsqrt(ms + EPS) * g_ref[...]).astype(BF16)


def _norm_matmul_kernel(x_ref, g_ref, w_ref, o_ref, xn_ref):
    @pl.when(pl.program_id(1) == 0)
    def _():
        _normalize_rows(x_ref, g_ref, xn_ref)

    o_ref[...] = _dot(xn_ref[...], w_ref[...].astype(BF16))


def _norm_matmul(x, g, w, *, name):
    m, d = x.shape
    n = w.shape[1]
    tm = _row_tile(m, 1024, 256)
    tn = _row_tile(n, 1024, 256)
    return pl.pallas_call(
        _norm_matmul_kernel,
        grid=(m // tm, n // tn),
        in_specs=[pl.BlockSpec((tm, d), lambda i, j: (i, 0)),
                  pl.BlockSpec((1, d), lambda i, j: (0, 0)),
                  pl.BlockSpec((d, tn), lambda i, j: (0, j))],
        out_specs=pl.BlockSpec((tm, tn), lambda i, j: (i, j)),
        out_shape=jax.ShapeDtypeStruct((m, n), F32),
        scratch_shapes=[pltpu.VMEM((tm, d), BF16)],
        compiler_params=_params("parallel", "arbitrary", vmem_limit=MOBA_VMEM_LIMIT),
        name=name,
    )(x, g.reshape(1, d), w)


def _shared_kv_kernel(x_ref, g_ref, w_ref, *rest):
    *o_refs, xn_ref = rest
    j = pl.program_id(1)

    @pl.when(j == 0)
    def _():
        _normalize_rows(x_ref, g_ref, xn_ref)

    y = _dot(xn_ref[...], w_ref[...])

    def write(refs):
        for h in range(MB_HEADS):
            for ref in refs:
                ref[h] = y[:, h * HEAD_DIM:(h + 1) * HEAD_DIM].astype(ref.dtype)

    @pl.when(j == 0)
    def _():
        write(o_refs[0::2])

    @pl.when(j == 1)
    def _():
        write(o_refs[1::2])


def _shared_kv(x, g, w, with_bf16, *, name):
    rows, d = x.shape
    tm = _row_tile(rows, 512, 256)
    assert w.shape[1] == 2 * MB_WIDTH
    shape = (MB_HEADS, rows, HEAD_DIM)
    spec = pl.BlockSpec((MB_HEADS, tm, HEAD_DIM), lambda i, j: (0, i, 0))
    dtypes = [F32, F32] + ([BF16, BF16] if with_bf16 else [])
    return pl.pallas_call(
        _shared_kv_kernel,
        grid=(rows // tm, 2),
        in_specs=[pl.BlockSpec((tm, d), lambda i, j: (i, 0)),
                  pl.BlockSpec((1, d), lambda i, j: (0, 0)),
                  pl.BlockSpec((d, MB_WIDTH), lambda i, j: (0, j))],
        out_specs=[spec] * len(dtypes),
        out_shape=[jax.ShapeDtypeStruct(shape, t) for t in dtypes],
        scratch_shapes=[pltpu.VMEM((tm, d), BF16)],
        compiler_params=_params("parallel", "arbitrary"),
        name=name,
    )(x, g.reshape(1, d), w)


def _out_proj_kernel(ap_ref, as_ref, bp_ref, bs_ref, wa_ref, wb_ref, xp_ref, xs_ref, *rest, prompt_tiles,
                     final_norm):
    if final_norm:
        g_ref, yp_ref, ys_ref = rest
    else:
        yp_ref, ys_ref = rest
    i = pl.program_id(0)

    def run(a_ref, b_ref, x_ref, o_ref):
        y = _dot(a_ref[...].astype(BF16), wa_ref[...]) + _dot(b_ref[...].astype(BF16), wb_ref[...])
        y = x_ref[...] + y
        if final_norm:
            ms = jnp.mean(y * y, axis=-1, keepdims=True)
            y = y * lax.rsqrt(ms + EPS) * g_ref[...]
        o_ref[...] = y

    @pl.when(i < prompt_tiles)
    def _():
        run(ap_ref, bp_ref, xp_ref, yp_ref)

    @pl.when(i >= prompt_tiles)
    def _():
        run(as_ref, bs_ref, xs_ref, ys_ref)


def _out_proj(main_p, main_s, mem_p, mem_s, w, x_p, x_s, g_final=None, *, name):
    tp, d = x_p.shape
    rows_s = x_s.shape[0]
    wm, wmem = main_p.shape[1], mem_p.shape[1]
    assert wm % wmem == 0
    tm = _row_tile(math.gcd(tp, rows_s), 256, 8)
    pt = tp // tm
    p_map = lambda i: (jnp.minimum(i, pt - 1), 0)
    s_map = lambda i: (jnp.maximum(i - pt, 0), 0)
    in_specs = [pl.BlockSpec((tm, wm), p_map), pl.BlockSpec((tm, wm), s_map),
                pl.BlockSpec((tm, wmem), p_map), pl.BlockSpec((tm, wmem), s_map),
                pl.BlockSpec((wm, d), lambda i: (0, 0)),
                pl.BlockSpec((wmem, d), lambda i: (wm // wmem, 0)),
                pl.BlockSpec((tm, d), p_map), pl.BlockSpec((tm, d), s_map)]
    args = [main_p, main_s, mem_p, mem_s, w, w, x_p, x_s]
    if g_final is not None:
        in_specs.append(pl.BlockSpec((1, d), lambda i: (0, 0)))
        args.append(g_final.reshape(1, d))
    return pl.pallas_call(
        functools.partial(_out_proj_kernel, prompt_tiles=pt, final_norm=g_final is not None),
        grid=((tp + rows_s) // tm,),
        in_specs=in_specs,
        out_specs=[pl.BlockSpec((tm, d), p_map), pl.BlockSpec((tm, d), s_map)],
        out_shape=[jax.ShapeDtypeStruct((tp, d), F32), jax.ShapeDtypeStruct((rows_s, d), F32)],
        compiler_params=_params("arbitrary"),
        name=name,
    )(*args)


def _mem_attn_heads(q_ref, gate_ref, o_ref, rows, kv_of_head):
    scale = HEAD_DIM ** -0.5
    for h in range(MEM_HEADS):
        cols = slice(h * HEAD_DIM, (h + 1) * HEAD_DIM)
        k, v = kv_of_head(h)
        s = _dot_nt(q_ref[rows, cols].astype(BF16), k.astype(BF16)) * scale
        e = jnp.exp(s - jnp.max(s, axis=-1, keepdims=True))
        o = _dot(e.astype(BF16), v.astype(BF16)) / jnp.sum(e, axis=-1, keepdims=True)
        o_ref[rows, cols] = (o * _silu(gate_ref[rows, cols])).astype(o_ref.dtype)


def _mem_attn_prompt_kernel(q_ref, gate_ref, mkv_ref, o_ref):
    def kv_of_head(h):
        return (mkv_ref[:, h * HEAD_DIM:(h + 1) * HEAD_DIM],
                mkv_ref[:, MEM_WIDTH + h * HEAD_DIM:MEM_WIDTH + (h + 1) * HEAD_DIM])

    _mem_attn_heads(q_ref, gate_ref, o_ref, slice(None), kv_of_head)


def _mem_attn_sample_kernel(q_ref, gate_ref, mk_ref, mv_ref, o_ref, *, ts):
    mem_len = mk_ref.shape[1] // MEM_HEADS
    for s in range(mk_ref.shape[0]):
        def kv_of_head(h, s=s):
            rows = pl.ds(h, mem_len, stride=MEM_HEADS)
            return mk_ref[s, rows, :], mv_ref[s, rows, :]

        _mem_attn_heads(q_ref, gate_ref, o_ref, slice(s * ts, (s + 1) * ts), kv_of_head)


def _mem_attn_prompt(z, col_q, mkv, *, name):
    rows = z.shape[0]
    mem_len = mkv.shape[0]
    tm = _row_tile(rows, 512, 8)
    return pl.pallas_call(
        _mem_attn_prompt_kernel,
        grid=(rows // tm,),
        in_specs=[pl.BlockSpec((tm, MEM_WIDTH), lambda t: (t, col_q)),
                  pl.BlockSpec((tm, MEM_WIDTH), lambda t: (t, col_q + 1)),
                  pl.BlockSpec((mem_len, 2 * MEM_WIDTH), lambda t: (0, 0))],
        out_specs=pl.BlockSpec((tm, MEM_WIDTH), lambda t: (t, 0)),
        out_shape=jax.ShapeDtypeStruct((rows, MEM_WIDTH), BF16),
        compiler_params=_params("parallel"),
        name=name,
    )(z, z, mkv)


def _mem_attn_sample(z, col_q, cache_mk, cache_mv, layer, ts, *, name):
    _, nb, mem_len, heads, hd = cache_mk.shape
    assert heads == MEM_HEADS and hd == HEAD_DIM
    view = lambda c: c.reshape(c.shape[0], nb, mem_len * heads, hd)
    ns = math.gcd(nb, MEM_SAMPLE_SEQS)
    cache_spec = pl.BlockSpec((None, ns, mem_len * heads, hd), lambda b: (layer, b, 0, 0))
    return pl.pallas_call(
        functools.partial(_mem_attn_sample_kernel, ts=ts),
        grid=(nb // ns,),
        in_specs=[pl.BlockSpec((ns * ts, MEM_WIDTH), lambda b: (b, col_q)),
                  pl.BlockSpec((ns * ts, MEM_WIDTH), lambda b: (b, col_q + 1)),
                  cache_spec, cache_spec],
        out_specs=pl.BlockSpec((ns * ts, MEM_WIDTH), lambda b: (b, 0)),
        out_shape=jax.ShapeDtypeStruct((nb * ts, MEM_WIDTH), F32),
        compiler_params=_params("parallel"),
        name=name,
    )(z, z, view(cache_mk), view(cache_mv))


def _cumsum_rows(x):
    n = x.shape[0]
    row = lax.broadcasted_iota(jnp.int32, x.shape, 0)
    s = 1
    while s < n:
        x = x + jnp.where(row >= s, pltpu.roll(x, s, 0), 0.0)
        s *= 2
    return x


def _lower_bound(lbp, layer):
    e = jnp.exp(lbp - jnp.max(lbp, axis=0, keepdims=True))
    return jnp.sum(e[:layer + 1], axis=0, keepdims=True) / jnp.sum(e, axis=0, keepdims=True)


def _hgrn_gates(hq, hf, lb):
    q = _silu(hq)
    f = lb + (1.0 - lb) * _sigmoid(hf)
    return q, 1.0 - f, jnp.log2(f)


def _hgrn_finish(o, hg, go):
    ms = jnp.mean(o * o, axis=-1, keepdims=True)
    return o * lax.rsqrt(ms + EPS) * go * _silu(hg)


def _diag_tiles(q, k, b, width):
    sub = SUBLANES
    lane = lax.broadcasted_iota(jnp.int32, (sub, width), 1)
    trow = lax.broadcasted_iota(jnp.int32, (sub, width), 0)
    tiles = []
    for r0 in range(0, q.shape[0], sub):
        q8, k8, b8 = q[r0:r0 + sub], k[r0:r0 + sub], b[r0:r0 + sub]
        tile = jnp.zeros((sub, width), F32)
        for s in range(sub):
            e = jnp.exp2(b8 - b8[s:s + 1])
            a_col = jnp.sum(q8 * e * k8[s:s + 1], axis=1, keepdims=True)
            tile = jnp.where(lane == r0 + s, a_col, tile)
        tiles.append(jnp.where(trow >= lane - r0, tile, 0.0))
    return jnp.concatenate(tiles, axis=0)


def _level_ref(b, level):
    parts = []
    for start in range(0, b.shape[0], 2 * level):
        r = start + level - 1
        parts.append(jnp.broadcast_to(b[r:r + 1, :], (2 * level, b.shape[1])))
    return jnp.concatenate(parts, axis=0)


def _level_masks(c):
    row = lax.broadcasted_iota(jnp.int32, (c, c), 0)
    col = lax.broadcasted_iota(jnp.int32, (c, c), 1)
    masks = []
    level = SUBLANES
    while level < c:
        sh = level.bit_length() - 1
        same = (row >> (sh + 1)) == (col >> (sh + 1))
        masks.append((level, same & (((row >> sh) & 1) == 1) & (((col >> sh) & 1) == 0)))
        level *= 2
    return masks


def _hgrn_chunk(q, k, g, v, st, masks):
    c = HG_CHUNK
    b = _cumsum_rows(g)
    a = _diag_tiles(q, k, b, c)
    for level, mask in masks:
        e = jnp.exp2(-jnp.abs(b - _level_ref(b, level)))
        a = jnp.where(mask, _dot_nt((q * e).astype(BF16), (k * e).astype(BF16)), a)
    b_end = b[c - 1:c, :]
    vb = v.astype(BF16)
    o = _dot(a.astype(BF16), vb) + _dot_nt((q * jnp.exp2(b)).astype(BF16), st.astype(BF16))
    st_new = st * jnp.exp2(b_end) + _dot_tn(vb, (k * jnp.exp2(b_end - b)).astype(BF16))
    return o, st_new


def _hgrn_prompt_kernel(hq_ref, hf_ref, hi_ref, hg_ref, lbp_ref, go_ref, mix_ref, sout_ref, st_ref, *, layer):
    n = pl.program_id(1)

    @pl.when(n == 0)
    def _():
        st_ref[...] = jnp.zeros_like(st_ref)

    lb = _lower_bound(lbp_ref[...], layer)
    go = go_ref[...]
    masks = _level_masks(HG_CHUNK)
    for c in range(HG_BLOCK_CHUNKS):
        rows = slice(c * HG_CHUNK, (c + 1) * HG_CHUNK)
        q, k, g = _hgrn_gates(hq_ref[rows, :], hf_ref[rows, :], lb)
        o, st_new = _hgrn_chunk(q, k, g, hi_ref[rows, :], st_ref[...], masks)
        st_ref[...] = st_new
        mix_ref[rows, :] = _hgrn_finish(o, hg_ref[rows, :], go).astype(mix_ref.dtype)

    @pl.when(n == pl.num_programs(1) - 1)
    def _():
        sout_ref[0] = st_ref[...].T


def _hgrn_prompt(z, hg_lb, g_o, tp, layer, *, name):
    rb = HG_CHUNK * HG_BLOCK_CHUNKS
    assert tp % rb == 0
    h_ = HG_HEADS
    n_lb = hg_lb.shape[0]
    zspec = lambda off: pl.BlockSpec((rb, HEAD_DIM), lambda h, n: (n, off + h))
    return pl.pallas_call(
        functools.partial(_hgrn_prompt_kernel, layer=layer),
        grid=(h_, tp // rb),
        in_specs=[zspec(0), zspec(h_), zspec(2 * h_), zspec(3 * h_),
                  pl.BlockSpec((n_lb, HEAD_DIM), lambda h, n: (0, h)),
                  pl.BlockSpec((1, HEAD_DIM), lambda h, n: (0, h))],
        out_specs=[pl.BlockSpec((rb, HEAD_DIM), lambda h, n: (n, h)),
                   pl.BlockSpec((1, HEAD_DIM, HEAD_DIM), lambda h, n: (h, 0, 0))],
        out_shape=[jax.ShapeDtypeStruct((tp, HG_WIDTH), BF16),
                   jax.ShapeDtypeStruct((h_, HEAD_DIM, HEAD_DIM), F32)],
        scratch_shapes=[pltpu.VMEM((HEAD_DIM, HEAD_DIM), F32)],
        compiler_params=_params("parallel", "arbitrary"),
        name=name,
    )(z, z, z, z, hg_lb, g_o.reshape(1, HG_WIDTH))


def _hgrn_sample_kernel(hq_ref, hf_ref, hi_ref, hg_ref, lbp_ref, go_ref, s0_ref, mix_ref, sout_ref, *, layer, ts, nb):
    lb = _lower_bound(lbp_ref[...], layer)
    go = go_ref[...]

    def one_sequence(bi):
        rows = pl.ds(pl.multiple_of(bi * ts, ts), ts)
        q, k, g = _hgrn_gates(hq_ref[rows, :], hf_ref[rows, :], lb)
        b = _cumsum_rows(g)
        b_end = b[ts - 1:ts, :]
        st = s0_ref[bi, 0].T
        a = _diag_tiles(q, k, b, LANES)
        vb = _pad_rows(hi_ref[rows, :], LANES).astype(BF16)
        kh = _pad_rows(k * jnp.exp2(b_end - b), LANES).astype(BF16)
        o = _dot(a.astype(BF16), vb) + _dot_nt((q * jnp.exp2(b)).astype(BF16), st.astype(BF16))
        st_new = st * jnp.exp2(b_end) + _dot_tn(vb, kh)
        sout_ref[bi, 0] = st_new.T
        mix_ref[rows, :] = _hgrn_finish(o, hg_ref[rows, :], go)

    def body(p, carry):
        for e in range(HG_SAMPLE_UNROLL):
            one_sequence(HG_SAMPLE_UNROLL * p + e)
        return carry

    lax.fori_loop(0, nb // HG_SAMPLE_UNROLL, body, 0)


def _hgrn_sample(z, hg_lb, g_o, s0, ts, layer, *, name):
    nb = s0.shape[0]
    rows = nb * ts
    assert rows == z.shape[0] and ts == SUBLANES and nb % HG_SAMPLE_UNROLL == 0
    h_ = HG_HEADS
    n_lb = hg_lb.shape[0]
    zspec = lambda off: pl.BlockSpec((rows, HEAD_DIM), lambda h: (0, off + h))
    return pl.pallas_call(
        functools.partial(_hgrn_sample_kernel, layer=layer, ts=ts, nb=nb),
        grid=(h_,),
        in_specs=[zspec(0), zspec(h_), zspec(2 * h_), zspec(3 * h_),
                  pl.BlockSpec((n_lb, HEAD_DIM), lambda h: (0, h)),
                  pl.BlockSpec((1, HEAD_DIM), lambda h: (0, h)),
                  pl.BlockSpec((nb, 1, HEAD_DIM, HEAD_DIM), lambda h: (0, h, 0, 0))],
        out_specs=[pl.BlockSpec((rows, HEAD_DIM), lambda h: (0, h)),
                   pl.BlockSpec((nb, 1, HEAD_DIM, HEAD_DIM), lambda h: (0, h, 0, 0))],
        out_shape=[jax.ShapeDtypeStruct((rows, HG_WIDTH), F32),
                   jax.ShapeDtypeStruct(s0.shape, F32)],
        compiler_params=_params("parallel"),
        name=name,
    )(z, z, z, z, hg_lb, g_o.reshape(1, HG_WIDTH), s0)


def _t5_bucket(dist):
    exact = N_BUCKETS // 2
    d = jnp.maximum(dist, exact).astype(F32)
    large = exact + (jnp.log(d / exact) / math.log(MAX_DISTANCE / exact) * (N_BUCKETS - exact)).astype(jnp.int32)
    return jnp.where(dist < exact, dist, jnp.minimum(large, N_BUCKETS - 1))


def _bias_table_kernel(rb_ref, bucket_ref, o_ref):
    hi, mid, lo = _split3(rb_ref[...])
    row = lax.broadcasted_iota(jnp.int32, (LANES, bucket_ref.shape[1]), 0)
    oh = jnp.where(row == bucket_ref[...], 1.0, 0.0).astype(BF16)
    o_ref[...] = _dot(hi, oh) + _dot(mid, oh) + _dot(lo, oh)


def _bias_table(rel_bias, dist):
    n = dist.shape[0]
    rb = jnp.zeros((16, LANES), F32).at[:MB_HEADS, :N_BUCKETS].set(rel_bias.T)
    bucket = _t5_bucket(jnp.maximum(dist, 0).astype(jnp.int32)).reshape(1, n)
    return pl.pallas_call(
        _bias_table_kernel,
        out_shape=jax.ShapeDtypeStruct((16, n), F32),
        name="bias_table",
    )(rb, bucket)


def _block_mean_kernel(k_ref, o_ref):
    for n in range(o_ref.shape[0]):
        rows = slice(n * MB_BLOCK, (n + 1) * MB_BLOCK)
        o_ref[n:n + 1, :] = jnp.sum(k_ref[rows, :], axis=0, keepdims=True) * (1.0 / MB_BLOCK)


def _block_mean(k, n_blocks):
    return pl.pallas_call(
        _block_mean_kernel,
        grid=(MB_HEADS,),
        in_specs=[pl.BlockSpec((None, n_blocks * MB_BLOCK, HEAD_DIM), lambda h: (h, 0, 0))],
        out_specs=pl.BlockSpec((None, n_blocks, HEAD_DIM), lambda h: (h, 0, 0)),
        out_shape=jax.ShapeDtypeStruct((MB_HEADS, n_blocks, HEAD_DIM), F32),
        compiler_params=_params("parallel"),
        name="block_mean",
    )(k)


def _select_topk(gate, axis, n_blocks):
    idx = lax.broadcasted_iota(jnp.int32, gate.shape, axis)
    sel = jnp.zeros(gate.shape, F32)
    for _ in range(min(MB_TOPK, n_blocks)):
        mx = jnp.max(gate, axis=axis, keepdims=True)
        first = jnp.min(jnp.where(gate == mx, idx, n_blocks), axis=axis, keepdims=True)
        pick = idx == first
        sel = jnp.where(pick & (mx > -jnp.inf), 1.0, sel)
        gate = jnp.where(pick, -jnp.inf, gate)
    return sel


MASK_BIG = 2.0 ** 100
DUMMY_LANE = 125
FAR_LANES = (126, 127)
MOBA_GROUP_LOG2 = 3
MOBA_GROUP = 1 << MOBA_GROUP_LOG2
MOBA_HEADS_PER_STEP = 3
MOBA_VMEM_LIMIT = 56 * 1024 * 1024


def _moba_prompt_kernel(q_ref, gate_ref, k_ref, v_ref, km_ref, brow_ref, o_ref, bias_ref, s_ref, mx_ref, acc_ref,
                        *, n_blocks):
    i = pl.program_id(1)
    blk = MB_BLOCK
    near = NEAR_BLOCKS
    heads = range(k_ref.shape[0])
    inv_scale = HEAD_DIM ** 0.5
    exp2_scale = HEAD_DIM ** -0.5 * math.log2(math.e)
    filler = n_blocks + 1

    @pl.when(i == 0)
    def _build_bias():
        rowi = lax.broadcasted_iota(jnp.int32, (blk, 2 * blk), 0)
        for hh in heads:
            for d in range(near):
                x = jnp.broadcast_to(brow_ref[hh, d:d + 1, :], (blk, 2 * blk))
                for bit in range(blk.bit_length() - 1):
                    x = jnp.where(((rowi >> bit) & 1) == 1, pltpu.roll(x, 1 << bit, 1), x)
                bias_ref[hh, d] = x[:, blk:] * inv_scale
            s_ref[hh, filler] = jnp.full((blk, blk), -MASK_BIG, F32)

    tq = lax.broadcasted_iota(jnp.int32, (blk, blk), 0)
    tk = lax.broadcasted_iota(jnp.int32, (blk, blk), 1)
    lane = lax.broadcasted_iota(jnp.int32, (blk, LANES), 1)
    lane_r = lax.broadcasted_iota(jnp.int32, (1, LANES), 1)
    fold = lambda t: jnp.maximum(t[:, :LANES], t[:, LANES:])
    qa, far_row = [], []
    for hh in heads:
        qh, ql = _split2(q_ref[:, hh * HEAD_DIM:(hh + 1) * HEAD_DIM])

        own = _dot_nt(qh, k_ref[hh, pl.ds(pl.multiple_of(i * blk, blk), blk), :]) + bias_ref[hh, 0]
        own = jnp.where(tq >= tk, own, -MASK_BIG)
        s_ref[hh, i] = own
        mx_ref[hh] = fold(own)

        kmh, kml = _split2(_pad_rows(km_ref[hh], -(-n_blocks // 16) * 16))
        gate_t = _dot_nt(kmh, qh) + _dot_nt(kml, qh) + _dot_nt(kmh, ql)
        blk_id = lax.broadcasted_iota(jnp.int32, gate_t.shape, 0)
        sel_t = _select_topk(jnp.where(blk_id < i, gate_t, -jnp.inf), 0, n_blocks)
        sel = _pad_rows(sel_t, LANES).T

        qa.append(jnp.concatenate([qh, jnp.where(lane >= FAR_LANES[0], 1.0, sel - 1.0).astype(BF16)], axis=1))
        c_far = brow_ref[hh, near - 1:near, 0:1] * inv_scale
        c_hi = c_far.astype(BF16).astype(F32)
        far_row.append(jnp.where(lane_r == FAR_LANES[0], c_hi, jnp.where(lane_r == FAR_LANES[1], c_far - c_hi, 0.0)))

    def scores(hh, j, big_lane, row_vals):
        rows = pl.ds(pl.multiple_of(jnp.minimum(j, i) * blk, blk), blk)
        right = jnp.broadcast_to(jnp.where(lane_r == big_lane, MASK_BIG, row_vals), (blk, LANES)).astype(BF16)
        return _dot_nt(qa[hh], jnp.concatenate([k_ref[hh, rows, :], right], axis=1))

    first_near = jnp.maximum(i - (near - 1), 0)

    def grouped(count, blocks_fn):
        n_full = count >> MOBA_GROUP_LOG2
        rem = count & (MOBA_GROUP - 1)
        half = MOBA_GROUP // 2
        for group, lo, hi in ((MOBA_GROUP, 0, n_full + jnp.where(rem > half, 1, 0)),
                              (half, 2 * n_full, 2 * n_full + jnp.where((rem > 0) & (rem <= half), 1, 0))):
            def body(p, carry, group=group):
                blocks_fn(group * p, group)
                return carry

            lax.fori_loop(lo, hi, body, 0)

    def far_blocks(first, group):
        mx = [mx_ref[hh] for hh in heads]
        for e in range(group):
            j = first + e
            is_far = j < first_near
            for hh in heads:
                raw = scores(hh, j, jnp.where(is_far, j, DUMMY_LANE), far_row[hh])
                s_ref[hh, jnp.where(is_far, j, n_blocks)] = raw
                mx[hh] = jnp.maximum(mx[hh], fold(raw))
        for hh in heads:
            mx_ref[hh] = mx[hh]

    grouped(first_near, far_blocks)

    mx = [mx_ref[hh] for hh in heads]
    for dlt in range(1, near):
        j = i - dlt
        jc = jnp.maximum(j, 0)
        for hh in heads:
            raw = scores(hh, jc, jnp.where(j >= 0, jc, DUMMY_LANE), 0.0) + bias_ref[hh, dlt]
            s_ref[hh, jnp.where(j >= 0, jc, n_blocks)] = raw
            mx[hh] = jnp.maximum(mx[hh], fold(raw))
    for hh in heads:
        mx_ref[hh] = jnp.broadcast_to(jnp.max(mx[hh], axis=1, keepdims=True), (blk, LANES))

    ones = jnp.ones((blk, LANES), BF16)

    def pv_part(hh, first, count):
        top = jnp.concatenate([mx_ref[hh]] * 2, axis=1)
        probs, vals = [], []
        for e in range(count):
            j = first + e
            tile = s_ref[hh, jnp.where(j <= i, j, filler)]
            probs.append(jnp.exp2((tile - top) * exp2_scale).astype(BF16))
            rows = pl.ds(pl.multiple_of(jnp.minimum(j, i) * blk, blk), blk)
            vals.append(jnp.concatenate([v_ref[hh, rows, :], ones], axis=1))
        return _dot(jnp.concatenate(probs, axis=1), jnp.concatenate(vals, axis=0))

    def pv_blocks(first, group):
        half = group // 2
        for hh in heads:
            acc_ref[hh] = acc_ref[hh] + (pv_part(hh, first, half) + pv_part(hh, first + half, half))

    acc_ref[...] = jnp.zeros_like(acc_ref)
    grouped(i + 1, pv_blocks)
    for hh in heads:
        cols = slice(hh * HEAD_DIM, (hh + 1) * HEAD_DIM)
        acc = acc_ref[hh]
        o_ref[:, cols] = (acc[:, :HEAD_DIM] / acc[:, HEAD_DIM:] * _silu(gate_ref[:, cols])).astype(o_ref.dtype)


def _moba_prompt(z, kb, vb, kmean, brows, tp, *, name):
    nq = tp // MB_BLOCK
    hp = MOBA_HEADS_PER_STEP
    steps_h = MB_HEADS // hp
    assert nq <= DUMMY_LANE and MB_HEADS % hp == 0
    head_spec = lambda rows, **kw: pl.BlockSpec((hp, rows, HEAD_DIM), lambda h, i: (h, 0, 0), **kw)
    once = dict(pipeline_mode=pl.Buffered(1))
    return pl.pallas_call(
        functools.partial(_moba_prompt_kernel, n_blocks=nq),
        grid=(steps_h, nq),
        in_specs=[pl.BlockSpec((MB_BLOCK, hp * HEAD_DIM), lambda h, i: (i, h)),
                  pl.BlockSpec((MB_BLOCK, hp * HEAD_DIM), lambda h, i: (i, steps_h + h)),
                  head_spec(tp, **once), head_spec(tp, **once), head_spec(nq),
                  pl.BlockSpec((hp, NEAR_BLOCKS, 2 * MB_BLOCK), lambda h, i: (h, 0, 0))],
        out_specs=pl.BlockSpec((MB_BLOCK, hp * HEAD_DIM), lambda h, i: (i, h)),
        out_shape=jax.ShapeDtypeStruct((tp, MB_WIDTH), BF16),
        scratch_shapes=[pltpu.VMEM((hp, NEAR_BLOCKS, MB_BLOCK, MB_BLOCK), F32),
                        pltpu.VMEM((hp, nq + 2, MB_BLOCK, MB_BLOCK), F32),
                        pltpu.VMEM((hp, MB_BLOCK, LANES), F32),
                        pltpu.VMEM((hp, MB_BLOCK, MB_BLOCK), F32)],
        compiler_params=_params("parallel", "arbitrary", vmem_limit=MOBA_VMEM_LIMIT),
        name=name,
    )(z, z, kb, vb, kmean, brows)


def _cat_heads(ref, *lead):
    return jnp.concatenate([ref[lead + (h,)] for h in range(MB_HEADS)], axis=1)


def _sample_scores_kernel(pt_ref, q_ref, knew_ref, *rest, n_pages, pps, ts):
    del pt_ref
    kc = rest[:pps]
    bs_ref, cfar_ref, p_ref, l_ref, wq_ref, wql_ref, s_ref, km_ref, sel_ref = rest[pps:]
    g = pl.program_id(1)
    n_steps = n_pages // pps
    n_blocks = n_pages * PAGE_SIZE // MB_BLOCK
    near_pages = min(NEAR_PAGES, n_pages)
    scale = HEAD_DIM ** -0.5
    pg = PAGE_SIZE
    ppb = MB_BLOCK // pg

    @pl.when(g == 0)
    def _start_sequence():
        rep = _pad_rows(jnp.concatenate([q_ref[...]] * MB_HEADS, axis=0), LANES)
        r_h = lax.broadcasted_iota(jnp.int32, rep.shape, 0) // ts
        c_h = lax.broadcasted_iota(jnp.int32, rep.shape, 1) // HEAD_DIM
        hi, lo = _split2(jnp.where(r_h == c_h, rep, 0.0))
        wq_ref[...] = hi
        wql_ref[...] = lo

    pages = [_cat_heads(kc[u], 0) for u in range(pps)]
    st = _dot_nt(jnp.concatenate([kp.astype(BF16) for kp in pages], axis=0), wq_ref[...])
    for u in range(pps):
        p = g * pps + u
        near_idx = jnp.maximum(p - (n_pages - near_pages), 0)
        b_near = bs_ref[pl.ds(pl.multiple_of(near_idx * pg, pg), pg), :]
        bias = jnp.where(p >= n_pages - near_pages, b_near, cfar_ref[...])
        s_ref[pl.ds(pl.multiple_of(p * pg, pg), pg), :] = st[u * pg:(u + 1) * pg] * scale + bias
    for n in range(pps // ppb):
        ksum = sum(jnp.sum(pages[n * ppb + u], axis=0, keepdims=True) for u in range(ppb))
        km_ref[pl.ds(g * (pps // ppb) + n, 1), :] = ksum * (1.0 / MB_BLOCK)

    @pl.when(g == n_steps - 1)
    def _softmax():
        st_new = _dot_nt(_pad_rows(_cat_heads(knew_ref), pg).astype(BF16), wq_ref[...])
        krow = lax.broadcasted_iota(jnp.int32, (pg, LANES), 0)
        qcol = lax.broadcasted_iota(jnp.int32, (pg, LANES), 1)
        valid = (krow < ts) & (krow <= (qcol & (ts - 1)))
        s_cur = jnp.where(valid, st_new * scale + bs_ref[pl.ds(near_pages * pg, pg), :], -jnp.inf)

        kmh, kml = _split2(km_ref[...])
        gate = _dot_nt(kmh, wq_ref[...]) + _dot_nt(kmh, wql_ref[...]) + _dot_nt(kml, wq_ref[...])
        sel_ref[...] = _select_topk(gate, 0, n_blocks)

        def max_body(n, m):
            keep = sel_ref[pl.ds(n, 1), :] > 0.5
            for u in range(ppb):
                tile = s_ref[pl.ds(pl.multiple_of((n * ppb + u) * pg, pg), pg), :]
                m = jnp.maximum(m, jnp.where(keep, tile, -jnp.inf))
            return m

        m = lax.fori_loop(0, n_blocks, max_body, s_cur)
        mrow = jnp.max(m, axis=0, keepdims=True)

        def exp_body(n, l):
            keep = sel_ref[pl.ds(n, 1), :] > 0.5
            for u in range(ppb):
                rows = pl.ds(pl.multiple_of((n * ppb + u) * pg, pg), pg)
                e = jnp.exp(jnp.where(keep, s_ref[rows, :] - mrow, -jnp.inf))
                p_ref[0, rows, :] = e.astype(BF16)
                l = l + e
            return l

        e_cur = jnp.exp(s_cur - mrow)
        p_ref[0, pl.ds(n_pages * pg, pg), :] = e_cur.astype(BF16)
        l = lax.fori_loop(0, n_blocks, exp_body, e_cur)
        l_ref[0] = jnp.sum(l, axis=0, keepdims=True)


def _sample_values_kernel(pt_ref, gate_ref, vnew_ref, p_ref, pcur_ref, l_ref, *rest, n_pages, pps, ts):
    del pt_ref
    vc = rest[:pps]
    o_ref, acc_ref = rest[pps:]
    g = pl.program_id(1)
    pg = PAGE_SIZE

    @pl.when(g == 0)
    def _own_block():
        acc_ref[...] = _dot_tn(pcur_ref[0], _pad_rows(_cat_heads(vnew_ref), pg).astype(BF16))

    vals = jnp.concatenate([_cat_heads(vc[u], 0).astype(BF16) for u in range(pps)], axis=0)
    acc_ref[...] = acc_ref[...] + _dot_tn(p_ref[0], vals)

    @pl.when(g == n_pages // pps - 1)
    def _finish():
        r = lax.broadcasted_iota(jnp.int32, (LANES, LANES), 0)
        c = lax.broadcasted_iota(jnp.int32, (LANES, LANES), 1)
        lcol = jnp.sum(jnp.where(r == c, jnp.broadcast_to(l_ref[0], (LANES, LANES)), 0.0), axis=1, keepdims=True)
        for h in range(MB_HEADS):
            cols = slice(h * HEAD_DIM, (h + 1) * HEAD_DIM)
            o = acc_ref[h * ts:(h + 1) * ts, cols] / lcol[h * ts:(h + 1) * ts, :]
            o_ref[:, cols] = o * _silu(gate_ref[:, cols])


def _moba_sample(z, k_s, v_s, cache_k, cache_v, page_table, bs, cfar, ts, *, name):
    nb, n_pages = page_table.shape
    pps = math.gcd(n_pages, MAX_PAGES_PER_STEP)
    assert (pps * PAGE_SIZE) % MB_BLOCK == 0
    assert ts & (ts - 1) == 0 and MB_HEADS * ts <= LANES and z.shape[0] == nb * ts
    n_steps = n_pages // pps
    n_blocks = n_pages * PAGE_SIZE // MB_BLOCK
    n_keys = (n_pages + 1) * PAGE_SIZE
    kc = cache_k.transpose(0, 2, 1, 3)
    vc = cache_v.transpose(0, 2, 1, 3)
    row_spec = lambda col: pl.BlockSpec((ts, MB_WIDTH), lambda b, g, pt: (b, col))
    new_spec = pl.BlockSpec((MB_HEADS, ts, HEAD_DIM), lambda b, g, pt: (0, b, 0))
    page_spec = lambda u: pl.BlockSpec((1, MB_HEADS, PAGE_SIZE, HEAD_DIM), lambda b, g, pt: (pt[b, g * pps + u], 0, 0, 0))
    const2 = lambda b, g, pt: (0, 0)

    probs, denom = pl.pallas_call(
        functools.partial(_sample_scores_kernel, n_pages=n_pages, pps=pps, ts=ts),
        grid_spec=pltpu.PrefetchScalarGridSpec(
            num_scalar_prefetch=1,
            grid=(nb, n_steps),
            in_specs=[row_spec(0), new_spec] + [page_spec(u) for u in range(pps)]
                     + [pl.BlockSpec(bs.shape, const2), pl.BlockSpec(cfar.shape, const2)],
            out_specs=[pl.BlockSpec((1, n_keys, LANES), lambda b, g, pt: (b, 0, 0)),
                       pl.BlockSpec((1, 1, LANES), lambda b, g, pt: (b, 0, 0))],
            scratch_shapes=[pltpu.VMEM((LANES, MB_WIDTH), BF16),
                            pltpu.VMEM((LANES, MB_WIDTH), BF16),
                            pltpu.VMEM((n_pages * PAGE_SIZE, LANES), F32),
                            pltpu.VMEM((n_blocks, MB_WIDTH), F32),
                            pltpu.VMEM((n_blocks, LANES), F32)],
        ),
        out_shape=[jax.ShapeDtypeStruct((nb, n_keys, LANES), BF16),
                   jax.ShapeDtypeStruct((nb, 1, LANES), F32)],
        compiler_params=_params("arbitrary", "arbitrary"),
        name=name + "_scores",
    )(page_table, z, k_s, *([kc] * pps), bs, cfar)

    return pl.pallas_call(
        functools.partial(_sample_values_kernel, n_pages=n_pages, pps=pps, ts=ts),
        grid_spec=pltpu.PrefetchScalarGridSpec(
            num_scalar_prefetch=1,
            grid=(nb, n_steps),
            in_specs=[row_spec(1), new_spec,
                      pl.BlockSpec((1, pps * PAGE_SIZE, LANES), lambda b, g, pt: (b, g, 0)),
                      pl.BlockSpec((1, PAGE_SIZE, LANES), lambda b, g, pt: (b, n_pages, 0)),
                      pl.BlockSpec((1, 1, LANES), lambda b, g, pt: (b, 0, 0))]
                     + [page_spec(u) for u in range(pps)],
            out_specs=pl.BlockSpec((ts, MB_WIDTH), lambda b, g, pt: (b, 0)),
            scratch_shapes=[pltpu.VMEM((LANES, MB_WIDTH), F32)],
        ),
        out_shape=jax.ShapeDtypeStruct((nb * ts, MB_WIDTH), F32),
        compiler_params=_params("arbitrary", "arbitrary"),
        name=name + "_values",
    )(page_table, z, v_s, probs, probs, denom, *([vc] * pps))


def kernel(x_prompt, x_sample, cache_k, cache_v, cache_mem_k, cache_mem_v, state_hgrn, page_table, mem_prompt,
           g_norm, w_in_a, hg_lb, g_hg_out, w_out_a, w_in_b, w_out_b, g_kv, w_kv, rel_bias, g_mem, w_mem_kv,
           g_final):
    bp, tp, d = x_prompt.shape
    bs_, ts, _ = x_sample.shape
    assert bp == 1 and w_in_a.shape[0] == 1 and w_in_b.shape[0] == 1
    n_pages = page_table.shape[1]
    assert (n_pages * PAGE_SIZE) % MB_BLOCK == 0 and tp % MB_BLOCK == 0
    rows_s = bs_ * ts

    x0_p = x_prompt.reshape(tp, d)
    x0_s = x_sample.reshape(rows_s, d)
    bf = lambda w: w.astype(BF16)

    mem_kv = [_norm_matmul(mem_prompt.reshape(-1, d), g_mem[l], w_mem_kv[l], name=f"mem_kv_{l}")
              for l in range(2)]
    mem_len = mem_kv[0].shape[0]

    def in_proj(xp, xs, g, w, tag):
        return (_norm_matmul(xp, g, w, name=f"in_proj_{tag}_prompt"), _norm_matmul(xs, g, w, name=f"in_proj_{tag}_sample"))

    def mem_attn(zp, zs, col_q, l, tag):
        return (_mem_attn_prompt(zp, col_q, mem_kv[l], name=f"mem_attn_{tag}_prompt"),
                _mem_attn_sample(zs, col_q, cache_mem_k, cache_mem_v, l, ts, name=f"mem_attn_{tag}_sample"))

    z_p, z_s = in_proj(x0_p, x0_s, g_norm[0], w_in_a[0], "a")
    mix_p, s_prompt = _hgrn_prompt(z_p, hg_lb, g_hg_out[0], tp, 0, name="hgrn_prompt")
    mix_s, s_sample = _hgrn_sample(z_s, hg_lb, g_hg_out[0], state_hgrn[0], ts, 0, name="hgrn_sample")
    mem_p, mem_s = mem_attn(z_p, z_s, 4 * HG_WIDTH // MEM_WIDTH, 0, "a")
    x1_p, x1_s = _out_proj(mix_p, mix_s, mem_p, mem_s, bf(w_out_a[0]), x0_p, x0_s, name="out_proj_a")

    k_p, v_p, kb_p, vb_p = _shared_kv(x1_p, g_kv, bf(w_kv), True, name="shared_kv_prompt")
    k_s, v_s = _shared_kv(x1_s, g_kv, bf(w_kv), False, name="shared_kv_sample")
    kmean = _block_mean(k_p, tp // MB_BLOCK)

    blk = MB_BLOCK
    c = jnp.arange(2 * blk, dtype=jnp.int32)
    dist_p = (jnp.arange(NEAR_BLOCKS, dtype=jnp.int32)[:, None] * blk + blk - c[None, :]).reshape(-1)
    brows = _bias_table(rel_bias, dist_p)[:MB_HEADS].reshape(MB_HEADS, NEAR_BLOCKS, 2 * blk)
    near_pages = min(NEAR_PAGES, n_pages)
    n_keys = (near_pages + 1) * PAGE_SIZE
    key_x = jnp.arange(n_keys, dtype=jnp.int32)
    dist_s = (near_pages * PAGE_SIZE - key_x[None, :] + jnp.arange(ts, dtype=jnp.int32)[:, None]).reshape(-1)
    bs_tab = _bias_table(rel_bias, dist_s)[:MB_HEADS].reshape(MB_HEADS, ts, n_keys)
    bs_tab = jnp.pad(bs_tab.transpose(2, 0, 1).reshape(n_keys, MB_HEADS * ts), ((0, 0), (0, LANES - MB_HEADS * ts)))
    cfar = jnp.pad(jnp.repeat(brows[:, NEAR_BLOCKS - 1, 0], ts), (0, LANES - MB_HEADS * ts)).reshape(1, LANES)

    zb_p, zb_s = in_proj(x1_p, x1_s, g_norm[1], w_in_b[0], "b")
    mix_p = _moba_prompt(zb_p, kb_p, vb_p, kmean, brows, tp, name="moba_prompt")
    mix_s = _moba_sample(zb_s, k_s, v_s, cache_k, cache_v, page_table, bs_tab, cfar, ts, name="moba_sample")
    mem_p, mem_s = mem_attn(zb_p, zb_s, 2 * MB_WIDTH // MEM_WIDTH, 1, "b")
    y_p, y_s = _out_proj(mix_p, mix_s, mem_p, mem_s, bf(w_out_b[0]), x1_p, x1_s, g_final, name="out_proj_b")

    heads = lambda a, b_, t: a.reshape(MB_HEADS, b_, t, HEAD_DIM).transpose(1, 2, 0, 3)
    memh = lambda lo: jnp.stack([kv[:, lo:lo + MEM_WIDTH] for kv in mem_kv]).reshape(2, bp, mem_len, MEM_HEADS, HEAD_DIM)
    return (y_p.reshape(bp, tp, d), y_s.reshape(bs_, ts, d),
            heads(k_p, bp, tp), heads(v_p, bp, tp), heads(k_s, bs_, ts), heads(v_s, bs_, ts),
            s_prompt[None, None].astype(state_hgrn.dtype), s_sample[None].astype(state_hgrn.dtype),
            memh(0), memh(MEM_WIDTH))
```
